```python
import functools
import jax, jax.numpy as jnp
from jax import lax
import numpy as np

D_MODEL = 1024
BATCH = 2
SEQ = 8192
DEPTH = 2
DEC_BATCH = 32
DEC_SEQ = 4
PAST_LEN = 8192
PAGE_SIZE = 128

N_EVEN = (DEPTH + 1) // 2
N_ODD = DEPTH // 2
A_HEADS = 4
A_HEAD_DIM = D_MODEL // 8
A_WIDTH = A_HEADS * A_HEAD_DIM
B_HEADS = 4
B_HEAD_DIM = D_MODEL // 8
B_WIDTH = B_HEADS * B_HEAD_DIM
C_HEADS = 4
C_KEY_WIDTH = D_MODEL // 2
C_VAL_WIDTH = D_MODEL
C_KEY_DIM = C_KEY_WIDTH // C_HEADS
C_VAL_DIM = C_VAL_WIDTH // C_HEADS
C_GATE_RANK = 16
GLA_GATE_NORMALIZER = 16.0
CHUNK = 64
Q_BLOCK = 128
EVEN_IN = 4 * A_WIDTH + 4 * B_WIDTH + B_HEADS
ODD_IN = 2 * C_KEY_WIDTH + 2 * C_VAL_WIDTH + C_GATE_RANK
EPS = 1e-6

kernel_name = 'hgrn2_fox_gla_hybrid_step'


def _split(a, widths):
    idx = [int(i) for i in np.cumsum(widths)[:-1]]
    return jnp.split(a, idx, axis=-1)


def rmsnorm(x, g):
    xf = x.astype(jnp.float32)
    y = xf * lax.rsqrt(jnp.mean(xf * xf, axis=-1, keepdims=True) + EPS)
    return (y * g.astype(jnp.float32)).astype(x.dtype)


def head_rmsnorm(o, g):
    of = o.astype(jnp.float32)
    y = of * lax.rsqrt(jnp.mean(of * of, axis=-1, keepdims=True) + EPS)
    return y * g.reshape(o.shape[-2], o.shape[-1]).astype(jnp.float32)


def gated_linear_chunked(q, k, v, log_f, s0):
    bsz, L, H, dk = q.shape
    dv = v.shape[-1]
    c = min(CHUNK, L)
    pad = (-L) % c
    q, k, v, log_f = (a.astype(jnp.float32) for a in (q, k, v, log_f))
    if pad:
        pw = ((0, 0), (0, pad), (0, 0), (0, 0))
        q, k, v, log_f = (jnp.pad(a, pw) for a in (q, k, v, log_f))
    n = (L + pad) // c

    def to_chunks(a):
        return a.reshape(bsz, n, c, H, a.shape[-1]).swapaxes(0, 1)

    causal = jnp.tril(jnp.ones((c, c), dtype=bool))

    def step(S, inp):
        qc, kc, vc, gc = inp
        b = jnp.cumsum(gc, axis=1)
        o_inter = jnp.einsum('bthk,bhkv->bthv', qc * jnp.exp(b), S)
        diff = b[:, :, None] - b[:, None, :]
        decay = jnp.exp(jnp.where(causal[None, :, :, None, None], diff, -jnp.inf))
        scores = jnp.einsum('bthk,bshk,btshk->btsh', qc, kc, decay)
        o_intra = jnp.einsum('btsh,bshv->bthv', scores, vc)
        b_last = b[:, -1]
        S_new = S * jnp.exp(b_last)[..., None] + jnp.einsum(
            'bshk,bshv->bhkv', kc * jnp.exp(b_last[:, None] - b), vc)
        return S_new, o_inter + o_intra

    s_final, o = lax.scan(step, s0.astype(jnp.float32),
                          (to_chunks(q), to_chunks(k), to_chunks(v), to_chunks(log_f)))
    o = o.swapaxes(0, 1).reshape(bsz, n * c, H, dv)[:, :L]
    return o, s_final


def fox_prompt(q, k, v, log_f):
    bsz, L, H, dh = q.shape
    scale = dh ** -0.5
    c = jnp.cumsum(log_f, axis=1).swapaxes(1, 2)
    nb = L // Q_BLOCK
    qb = q.reshape(bsz, nb, Q_BLOCK, H, dh).swapaxes(0, 1)
    cb = c.reshape(bsz, H, nb, Q_BLOCK).transpose(2, 0, 1, 3)
    starts = jnp.arange(nb, dtype=jnp.int32) * Q_BLOCK
    key_pos = jnp.arange(L, dtype=jnp.int32)

    def block(args):
        qi, ci, s0 = args
        t = s0 + jnp.arange(Q_BLOCK, dtype=jnp.int32)
        logits = jnp.einsum('bthd,bshd->bhts', qi, k).astype(jnp.float32) * scale \
            + ci[..., None] - c[:, :, None, :]
        logits = jnp.where(t[:, None] >= key_pos[None, :], logits, -jnp.inf)
        p = jax.nn.softmax(logits, axis=-1).astype(v.dtype)
        return jnp.einsum('bhts,bshd->bthd', p, v)

    out = lax.map(block, (qb, cb, starts))
    return out.swapaxes(0, 1).reshape(bsz, L, H, dh)


def fox_sample(q, k, v, log_f, k_pages, v_pages, logf_pages, page_table):
    db, T, H, dh = q.shape
    scale = dh ** -0.5
    k_past = k_pages[page_table].reshape(db, -1, H, dh)
    v_past = v_pages[page_table].reshape(db, -1, H, dh)
    lf_past = logf_pages[page_table].reshape(db, -1, H).astype(jnp.float32)
    P = k_past.shape[1]
    cum_new = jnp.cumsum(log_f, axis=1).swapaxes(1, 2)
    suffix_past = (lax.cumsum(lf_past, axis=1, reverse=True) - lf_past).swapaxes(1, 2)
    logit_past = jnp.einsum('bthd,bshd->bhts', q, k_past).astype(jnp.float32) * scale \
        + cum_new[..., None] + suffix_past[:, :, None, :]
    logit_new = jnp.einsum('bthd,bshd->bhts', q, k).astype(jnp.float32) * scale \
        + cum_new[..., None] - cum_new[:, :, None, :]
    causal = jnp.tril(jnp.ones((T, T), dtype=bool))
    logit_new = jnp.where(causal, logit_new, -jnp.inf)
    p = jax.nn.softmax(jnp.concatenate([logit_past, logit_new], axis=-1), axis=-1).astype(v.dtype)
    return jnp.einsum('bhts,bshd->bthd', p[..., :P], v_past) + jnp.einsum('bhts,bshd->bthd', p[..., P:], v)


def even_mixer(h, w_in, b_fox, lb, hgrn_gain, w_out, hgrn_s0, fox_attend):
    bsz, L, _ = h.shape
    proj = h @ w_in
    qa, fa, ia, ga, qb, kb, vb, gb, fb = _split(proj, [A_WIDTH] * 4 + [B_WIDTH] * 4 + [B_HEADS])
    f = lb + (1.0 - lb) * jax.nn.sigmoid(fa.astype(jnp.float32))
    ha = lambda a: a.reshape(bsz, L, A_HEADS, A_HEAD_DIM)
    o_a, s_a = gated_linear_chunked(ha(qa), ha(1.0 - f), ha(ia), ha(jnp.log(f)), hgrn_s0)
    o_a = head_rmsnorm(o_a, hgrn_gain).reshape(bsz, L, A_WIDTH).astype(h.dtype) * jax.nn.silu(ga)
    hb = lambda a: a.reshape(bsz, L, B_HEADS, B_HEAD_DIM)
    log_fb = jax.nn.log_sigmoid((fb + b_fox).astype(jnp.float32))
    kb_h, vb_h = hb(kb), hb(vb)
    o_b = fox_attend(hb(qb), kb_h, vb_h, log_fb).reshape(bsz, L, B_WIDTH).astype(h.dtype) * jax.nn.silu(gb)
    out = jnp.concatenate([o_a, o_b], axis=-1) @ w_out
    return out, s_a, kb_h, vb_h, log_fb


def odd_mixer(h, w_in, w_gate, b_gate, gla_gain, w_out, gla_s0):
    bsz, L, _ = h.shape
    proj = h @ w_in
    q, k, v, g, r = _split(proj, [C_KEY_WIDTH, C_KEY_WIDTH, C_VAL_WIDTH, C_VAL_WIDTH, C_GATE_RANK])
    log_f = jax.nn.log_sigmoid((r @ w_gate + b_gate).astype(jnp.float32)) / GLA_GATE_NORMALIZER
    hk = lambda a: a.reshape(bsz, L, C_HEADS, C_KEY_DIM)
    hv = lambda a: a.reshape(bsz, L, C_HEADS, C_VAL_DIM)
    o, s = gated_linear_chunked(hk(q) * C_KEY_DIM ** -0.5, hk(k), hv(v), hk(log_f), gla_s0)
    o = head_rmsnorm(o, gla_gain).reshape(bsz, L, C_VAL_WIDTH).astype(h.dtype) * jax.nn.silu(g)
    return o @ w_out, s


def run_trunk(x, fox_attend, hgrn_s0, gla_s0, norm_even, w_in_even, b_fox_f, lb_logits, hgrn_gain,
              w_out_even, norm_odd, w_in_odd, w_gla_gate, b_gla_gate, gla_gain, w_out_odd, final_norm):
    lb_all = jnp.cumsum(jax.nn.softmax(lb_logits.astype(jnp.float32), axis=0), axis=0)
    ks, vs, lfs, hs, gs = [], [], [], [], []
    for l in range(DEPTH):
        i = l // 2
        if l % 2 == 0:
            out, s_a, kb, vb, lfb = even_mixer(rmsnorm(x, norm_even[i]), w_in_even[i], b_fox_f[i], lb_all[i],
                                               hgrn_gain[i], w_out_even[i], hgrn_s0[i],
                                               functools.partial(fox_attend, i))
            ks.append(kb); vs.append(vb); lfs.append(lfb); hs.append(s_a)
        else:
            out, s_c = odd_mixer(rmsnorm(x, norm_odd[i]), w_in_odd[i], w_gla_gate[i], b_gla_gate[i],
                                 gla_gain[i], w_out_odd[i], gla_s0[i])
            gs.append(s_c)
        x = x + out
    return rmsnorm(x, final_norm), jnp.stack(ks), jnp.stack(vs), jnp.stack(lfs), jnp.stack(hs), jnp.stack(gs)


def setup_inputs(seed: int = 0) -> dict:
    key = jax.random.key(seed)
    ks = jax.random.split(key, 24)
    n_pages = PAST_LEN // PAGE_SIZE
    n_used = DEC_BATCH * n_pages
    n_pool = n_used + (n_used + 3) // 4
    nrm = jax.random.normal
    page_table = jax.random.permutation(ks[0], n_pool)[:n_used].reshape(DEC_BATCH, n_pages).astype(jnp.int32)
    return {
        'x_prompt': nrm(ks[1], (BATCH, SEQ, D_MODEL), jnp.float32),
        'x_sample': nrm(ks[2], (DEC_BATCH, DEC_SEQ, D_MODEL), jnp.float32),
        'cache_fox_k': nrm(ks[3], (N_EVEN, n_pool, PAGE_SIZE, B_HEADS, B_HEAD_DIM), jnp.float32),
        'cache_fox_v': nrm(ks[4], (N_EVEN, n_pool, PAGE_SIZE, B_HEADS, B_HEAD_DIM), jnp.float32),
        'cache_fox_logf': jax.nn.log_sigmoid(8.0 + 0.5 * nrm(ks[5], (N_EVEN, n_pool, PAGE_SIZE, B_HEADS), jnp.float32)),
        'state_hgrn': 0.5 * nrm(ks[6], (N_EVEN, DEC_BATCH, A_HEADS, A_HEAD_DIM, A_HEAD_DIM), jnp.float32),
        'state_gla': nrm(ks[7], (N_ODD, DEC_BATCH, C_HEADS, C_KEY_DIM, C_VAL_DIM), jnp.float32),
        'page_table': page_table,
        'norm_even': 1.0 + 0.1 * nrm(ks[8], (N_EVEN, D_MODEL), jnp.float32),
        'w_in_even': nrm(ks[9], (N_EVEN, D_MODEL, EVEN_IN), jnp.float32) * D_MODEL ** -0.5,
        'b_fox_f': 4.0 + 0.5 * nrm(ks[10], (N_EVEN, B_HEADS), jnp.float32),
        'lb_logits': 0.5 * nrm(ks[11], (N_EVEN + 1, A_WIDTH), jnp.float32),
        'hgrn_gain': 1.0 + 0.1 * nrm(ks[12], (N_EVEN, A_WIDTH), jnp.float32),
        'w_out_even': nrm(ks[13], (N_EVEN, A_WIDTH + B_WIDTH, D_MODEL), jnp.float32) * (A_WIDTH + B_WIDTH) ** -0.5,
        'norm_odd': 1.0 + 0.1 * nrm(ks[14], (N_ODD, D_MODEL), jnp.float32),
        'w_in_odd': nrm(ks[15], (N_ODD, D_MODEL, ODD_IN), jnp.float32) * D_MODEL ** -0.5,
        'w_gla_gate': nrm(ks[16], (N_ODD, C_GATE_RANK, C_KEY_WIDTH), jnp.float32) * C_GATE_RANK ** -0.5,
        'b_gla_gate': 0.1 * nrm(ks[17], (N_ODD, C_KEY_WIDTH), jnp.float32),
        'gla_gain': 1.0 + 0.1 * nrm(ks[18], (N_ODD, C_VAL_WIDTH), jnp.float32),
        'w_out_odd': nrm(ks[19], (N_ODD, C_VAL_WIDTH, D_MODEL), jnp.float32) * C_VAL_WIDTH ** -0.5,
        'final_norm': 1.0 + 0.1 * nrm(ks[20], (D_MODEL,), jnp.float32),
    }


def reference(x_prompt, x_sample, cache_fox_k, cache_fox_v, cache_fox_logf, state_hgrn, state_gla, page_table,
              norm_even, w_in_even, b_fox_f, lb_logits, hgrn_gain, w_out_even, norm_odd, w_in_odd,
              w_gla_gate, b_gla_gate, gla_gain, w_out_odd, final_norm):
    weights = (norm_even, w_in_even, b_fox_f, lb_logits, hgrn_gain, w_out_even, norm_odd, w_in_odd,
               w_gla_gate, b_gla_gate, gla_gain, w_out_odd, final_norm)
    bsz, L, _ = x_prompt.shape
    hgrn0 = jnp.zeros((N_EVEN, bsz, A_HEADS, A_HEAD_DIM, A_HEAD_DIM), jnp.float32)
    gla0 = jnp.zeros((N_ODD, bsz, C_HEADS, C_KEY_DIM, C_VAL_DIM), jnp.float32)
    fox_p = lambda i, q, k, v, lf: fox_prompt(q, k, v, lf)
    y_prompt, kp, vp, lfp, hgrn_p, gla_p = run_trunk(x_prompt, fox_p, hgrn0, gla0, *weights)
    n_pp = L // PAGE_SIZE
    fox_k_prompt = kp.reshape(N_EVEN, bsz, n_pp, PAGE_SIZE, B_HEADS, B_HEAD_DIM)
    fox_v_prompt = vp.reshape(N_EVEN, bsz, n_pp, PAGE_SIZE, B_HEADS, B_HEAD_DIM)
    fox_logf_prompt = lfp.reshape(N_EVEN, bsz, n_pp, PAGE_SIZE, B_HEADS)
    fox_s = lambda i, q, k, v, lf: fox_sample(q, k, v, lf, cache_fox_k[i], cache_fox_v[i], cache_fox_logf[i], page_table)
    y_sample, fox_k_sample, fox_v_sample, fox_logf_sample, hgrn_s, gla_s = run_trunk(
        x_sample, fox_s, state_hgrn, state_gla, *weights)
    return (y_prompt, y_sample, fox_k_prompt, fox_v_prompt, fox_logf_prompt, hgrn_p, gla_p,
            fox_k_sample, fox_v_sample, fox_logf_sample, hgrn_s, gla_s)
```

```python
import functools

import numpy as np
import jax
import jax.numpy as jnp
from jax import lax
from jax.experimental import pallas as pl
from jax.experimental.pallas import tpu as pltpu

F32 = jnp.float32
BF16 = jnp.bfloat16

D_MODEL = 1024
PAGE_SIZE = 128
A_HEADS = 4
A_HEAD_DIM = 128
A_WIDTH = 512
B_HEADS = 4
B_HEAD_DIM = 128
B_WIDTH = 512
C_HEADS = 4
C_KEY_WIDTH = 512
C_VAL_WIDTH = 1024
C_KEY_DIM = 128
C_VAL_DIM = 256
C_GATE_RANK = 16
GLA_GATE_NORMALIZER = 16.0
EPS = 1e-6
HGRN_LAYER = 0

LANES = 128
SUBLANES = 8
VMEM_LIMIT = 56 * 1024 * 1024
CHUNK = 64
MAX_CHUNK_LOG_DECAY = 60.0

_NT = (((1,), (1,)), ((), ()))
_TN = (((0,), (0,)), ((), ()))


def _cparams(sem):
    return pltpu.CompilerParams(dimension_semantics=sem, vmem_limit_bytes=VMEM_LIMIT)


def _sigmoid(x):
    return 1.0 / (1.0 + jnp.exp(-x))


def _log_sigmoid(x):
    return jnp.minimum(x, 0.0) - jnp.log1p(jnp.exp(-jnp.abs(x)))


def _silu(x):
    return x * _sigmoid(x)


def _rmsnorm_rows(x, g):
    return x * lax.rsqrt(jnp.mean(x * x, axis=-1, keepdims=True) + EPS) * g


def _split3(x):
    p1 = x.astype(BF16)
    r1 = x - p1.astype(F32)
    p2 = r1.astype(BF16)
    p3 = (r1 - p2.astype(F32)).astype(BF16)
    return p1, p2, p3


def _dot01(m01, x):
    acc = None
    for p in _split3(x):
        t = jnp.dot(m01, p, preferred_element_type=F32)
        acc = t if acc is None else acc + t
    return acc


def _x_dot01(x, m01):
    acc = None
    for p in _split3(x):
        t = jnp.dot(p, m01, preferred_element_type=F32)
        acc = t if acc is None else acc + t
    return acc


def _lower_tri(n, seq):
    r = lax.broadcasted_iota(jnp.int32, (n, n), 0)
    c = lax.broadcasted_iota(jnp.int32, (n, n), 1)
    keep = r >= c
    if seq < n:
        keep = jnp.logical_and(keep, (r // seq) == (c // seq))
    return jnp.where(keep, 1.0, 0.0).astype(BF16)


def _proj_cols(h, w_ref, out_ref):
    n = w_ref.shape[1]
    step = 512
    for c0 in range(0, n, step):
        out_ref[:, c0:c0 + step] = jnp.dot(h, w_ref[:, c0:c0 + step], preferred_element_type=F32)


def _even_proj_kernel(x_ref, g_ref, w_ref, wfb_ref, bfox_ref, proj_ref, lf_ref, c_ref, carry_ref, *, tm, seq):
    i = pl.program_id(0)
    h = _rmsnorm_rows(x_ref[...], g_ref[...]).astype(BF16)
    _proj_cols(h, w_ref, proj_ref)
    fb = jnp.dot(h, wfb_ref[...], preferred_element_type=F32) + bfox_ref[...]
    lane = lax.broadcasted_iota(jnp.int32, fb.shape, 1)
    lf = jnp.where(lane < B_HEADS, _log_sigmoid(fb), 0.0)
    lf_ref[...] = lf
    cs = _dot01(_lower_tri(tm, seq), lf)
    if seq > tm:
        @pl.when((i * tm) % seq == 0)
        def _():
            carry_ref[...] = jnp.zeros_like(carry_ref)
        cs = cs + carry_ref[0:1, :]
        c_ref[...] = cs
        carry_ref[0:1, :] = cs[tm - 1:tm, :]
    else:
        c_ref[...] = cs


def _even_proj(x, g, w_main, w_fb, b_fox, *, seq, tm):
    m = x.shape[0]
    n = w_main.shape[1]
    const = lambda i: (0, 0)
    row = lambda i: (i, 0)
    return pl.pallas_call(
        functools.partial(_even_proj_kernel, tm=tm, seq=seq),
        grid=(m // tm,),
        in_specs=[pl.BlockSpec((tm, D_MODEL), row), pl.BlockSpec((1, D_MODEL), const),
                  pl.BlockSpec((D_MODEL, n), const), pl.BlockSpec((D_MODEL, LANES), const),
                  pl.BlockSpec((1, LANES), const)],
        out_specs=[pl.BlockSpec((tm, n), row), pl.BlockSpec((tm, LANES), row), pl.BlockSpec((tm, LANES), row)],
        out_shape=[jax.ShapeDtypeStruct((m, n), F32), jax.ShapeDtypeStruct((m, LANES), F32),
                   jax.ShapeDtypeStruct((m, LANES), F32)],
        scratch_shapes=[pltpu.VMEM((SUBLANES, LANES), F32)],
        compiler_params=_cparams(("arbitrary",)),
        name="even_proj",
    )(x, g, w_main, w_fb, b_fox)


def _odd_proj_kernel(x_ref, g_ref, w_ref, wr_ref, wg_ref, bg_ref, proj_ref, lf_ref):
    h = _rmsnorm_rows(x_ref[...], g_ref[...]).astype(BF16)
    _proj_cols(h, w_ref, proj_ref)
    r = jnp.dot(h, wr_ref[...], preferred_element_type=F32)
    z = jnp.dot(r.astype(BF16), wg_ref[...], preferred_element_type=F32) + bg_ref[...]
    lf_ref[...] = _log_sigmoid(z) / GLA_GATE_NORMALIZER


def _odd_proj(x, g, w_main, w_r, w_gate, b_gate, *, tm):
    m = x.shape[0]
    n = w_main.shape[1]
    const = lambda i: (0, 0)
    row = lambda i: (i, 0)
    return pl.pallas_call(
        _odd_proj_kernel,
        grid=(m // tm,),
        in_specs=[pl.BlockSpec((tm, D_MODEL), row), pl.BlockSpec((1, D_MODEL), const),
                  pl.BlockSpec((D_MODEL, n), const), pl.BlockSpec((D_MODEL, LANES), const),
                  pl.BlockSpec((LANES, C_KEY_WIDTH), const), pl.BlockSpec((1, C_KEY_WIDTH), const)],
        out_specs=[pl.BlockSpec((tm, n), row), pl.BlockSpec((tm, C_KEY_WIDTH), row)],
        out_shape=[jax.ShapeDtypeStruct((m, n), F32), jax.ShapeDtypeStruct((m, C_KEY_WIDTH), F32)],
        compiler_params=_cparams(("arbitrary",)),
        name="odd_proj",
    )(x, g, w_main, w_r, w_gate, b_gate)


def _lower_bound(logits, layer):
    e = jnp.exp(logits - jnp.max(logits, axis=0, keepdims=True))
    return jnp.sum(e[:layer + 1, :], axis=0, keepdims=True) / jnp.sum(e, axis=0, keepdims=True)


def _hgrn_gate(fa, lb):
    f = lb + (1.0 - lb) * _sigmoid(fa)
    return jnp.log(f), 1.0 - f


def _exact_group(q, k, v, g, st, lo, hi):
    n = SUBLANES
    row = lax.broadcasted_iota(jnp.int32, (n, 1), 0)
    valid = jnp.logical_and(row >= lo, row < hi)
    q = jnp.where(valid, q, 0.0)
    k = jnp.where(valid, k, 0.0)
    g = jnp.where(valid, g, 0.0)
    b = g
    for sh in (1, 2, 4):
        b = b + jnp.where(row >= sh, pltpu.roll(b, sh, 0), 0.0)
    o = lax.dot_general((q * jnp.exp(b)).astype(BF16), st.astype(BF16), _NT, preferred_element_type=F32)
    for s in range(lo, hi):
        w = jnp.exp(jnp.minimum(b - b[s:s + 1, :], 0.0))
        a = jnp.sum(q * k[s:s + 1, :] * w, axis=-1, keepdims=True)
        o = o + jnp.where(row >= s, a, 0.0) * v[s:s + 1, :]
    b_last = b[n - 1:n, :]
    k_hat = k * jnp.exp(b_last - b)
    st_new = st * jnp.exp(b_last) + lax.dot_general(v.astype(BF16), k_hat.astype(BF16), _TN,
                                                    preferred_element_type=F32)
    return o, st_new


def _head_norm_gate(o, gain, gate):
    y = o * lax.rsqrt(jnp.mean(o * o, axis=-1, keepdims=True) + EPS) * gain
    return y * _silu(gate)


def _gla_prompt_kernel(*refs, hgrn, heads, dk, dv, tb, q_scale):
    if hgrn:
        q_ref, fa_ref, v_ref, gate_ref, lb_ref, gain_ref, o_ref, s_ref, st_ref, b_ref, g_ref, k_ref = refs
    else:
        q_ref, k_ref, v_ref, gate_ref, g_ref, gain_ref, o_ref, s_ref, st_ref, b_ref = refs
    t = pl.program_id(1)
    n_chunks = tb // CHUNK
    if hgrn:
        lb = _lower_bound(lb_ref[...], HGRN_LAYER)

    @pl.when(t == 0)
    def _():
        st_ref[...] = jnp.zeros_like(st_ref)

    tri = _lower_tri(CHUNK, CHUNK)
    b_min = None
    for c in range(n_chunks):
        rows = slice(c * CHUNK, (c + 1) * CHUNK)
        if hgrn:
            g, k = _hgrn_gate(fa_ref[rows, :], lb)
            g_ref[rows, :] = g
            k_ref[rows, :] = k
        else:
            g = g_ref[rows, :]
        b = _dot01(tri, g)
        b_ref[rows, :] = b
        b_last = b[CHUNK - 1:CHUNK, :]
        b_min = b_last if b_min is None else jnp.minimum(b_min, b_last)
    chunk_form_ok = jnp.min(b_min) >= -MAX_CHUNK_LOG_DECAY

    def finish(rows, h, o_h):
        vsl = slice(h * dv, (h + 1) * dv)
        o_ref[rows, vsl] = _head_norm_gate(o_h, gain_ref[:, vsl], gate_ref[rows, vsl])

    @pl.when(chunk_form_ok)
    def _():
        r_i = lax.broadcasted_iota(jnp.int32, (CHUNK, CHUNK), 0)
        c_i = lax.broadcasted_iota(jnp.int32, (CHUNK, CHUNK), 1)
        causal = r_i >= c_i

        def chunk(c, carry):
            rows = pl.ds(pl.multiple_of(c * CHUNK, CHUNK), CHUNK)
            b = b_ref[rows, :]
            e_b = jnp.exp(b)
            b_last = b[CHUNK - 1:CHUNK, :]
            e_last = jnp.exp(b_last)
            q_t = q_ref[rows, :] * q_scale * e_b
            k_t = k_ref[rows, :] * jnp.exp(-b)
            k_hat = k_t * e_last
            for h in range(heads):
                ksl = slice(h * dk, (h + 1) * dk)
                vsl = slice(h * dv, (h + 1) * dv)
                qh = q_t[:, ksl].astype(BF16)
                vh = v_ref[rows, vsl].astype(BF16)
                a = lax.dot_general(qh, k_t[:, ksl].astype(BF16), _NT, preferred_element_type=F32)
                a = jnp.where(causal, a, 0.0).astype(BF16)
                st = st_ref[h]
                o_h = jnp.dot(a, vh, preferred_element_type=F32) + lax.dot_general(
                    qh, st.astype(BF16), _NT, preferred_element_type=F32)
                st_ref[h] = st * e_last[:, ksl] + lax.dot_general(
                    vh, k_hat[:, ksl].astype(BF16), _TN, preferred_element_type=F32)
                finish(rows, h, o_h)
            return carry

        lax.fori_loop(0, n_chunks, chunk, 0)

    @pl.when(jnp.logical_not(chunk_form_ok))
    def _():
        def group(i, carry):
            rows = pl.ds(pl.multiple_of(i * SUBLANES, SUBLANES), SUBLANES)
            for h in range(heads):
                ksl = slice(h * dk, (h + 1) * dk)
                vsl = slice(h * dv, (h + 1) * dv)
                o_h, st_new = _exact_group(q_ref[rows, ksl] * q_scale, k_ref[rows, ksl], v_ref[rows, vsl],
                                           g_ref[rows, ksl], st_ref[h], 0, SUBLANES)
                st_ref[h] = st_new
                finish(rows, h, o_h)
            return carry

        lax.fori_loop(0, tb // SUBLANES, group, 0)

    @pl.when(t == pl.num_programs(1) - 1)
    def _():
        for h in range(heads):
            s_ref[0, h] = st_ref[h].T


def _gla_prompt(hgrn, arrays, col_blocks, small, gain, *, bsz, seq, heads, dk, dv, tb, q_scale):
    nt = seq // tb
    kw, vw = heads * dk, heads * dv
    widths = [kw, kw, vw, vw] + ([] if hgrn else [kw])
    in_specs = [pl.BlockSpec((tb, w), functools.partial(lambda b, t, cb: (b * nt + t, cb), cb=cb))
                for w, cb in zip(widths, col_blocks)]
    operands = list(arrays)
    if hgrn:
        in_specs.append(pl.BlockSpec(small.shape, lambda b, t: (0, 0)))
        operands.append(small)
    in_specs.append(pl.BlockSpec((1, vw), lambda b, t: (0, 0)))
    operands.append(gain)
    scratch = [pltpu.VMEM((heads, dv, dk), F32), pltpu.VMEM((tb, kw), F32)]
    if hgrn:
        scratch += [pltpu.VMEM((tb, kw), F32), pltpu.VMEM((tb, kw), F32)]
    return pl.pallas_call(
        functools.partial(_gla_prompt_kernel, hgrn=hgrn, heads=heads, dk=dk, dv=dv, tb=tb, q_scale=q_scale),
        grid=(bsz, nt),
        in_specs=in_specs,
        out_specs=[pl.BlockSpec((tb, vw), lambda b, t: (b * nt + t, 0)),
                   pl.BlockSpec((1, heads, dk, dv), lambda b, t: (b, 0, 0, 0))],
        out_shape=[jax.ShapeDtypeStruct((bsz * seq, vw), F32), jax.ShapeDtypeStruct((bsz, heads, dk, dv), F32)],
        scratch_shapes=scratch,
        compiler_params=_cparams(("arbitrary", "arbitrary")),
        name="hgrn_prompt" if hgrn else "gla_prompt",
    )(*operands)


def _gla_sample_kernel(*refs, hgrn, heads, dk, dv, t_new, q_scale):
    if hgrn:
        q_ref, fa_ref, v_ref, gate_ref, lb_ref, gain_ref, s0_ref, o_ref, s_ref = refs
    else:
        q_ref, k_ref, v_ref, gate_ref, g_ref, gain_ref, s0_ref, o_ref, s_ref = refs
    for h in range(heads):
        ksl = slice(h * dk, (h + 1) * dk)
        vsl = slice(h * dv, (h + 1) * dv)
        if hgrn:
            g, k = _hgrn_gate(fa_ref[:, ksl], _lower_bound(lb_ref[...], HGRN_LAYER)[:, ksl])
        else:
            g, k = g_ref[:, ksl], k_ref[:, ksl]
        q = q_ref[:, ksl] * q_scale
        v = v_ref[:, vsl]
        o_h = None
        for e in range(SUBLANES // t_new):
            o_e, st_new = _exact_group(q, k, v, g, s0_ref[e, h].T, e * t_new, (e + 1) * t_new)
            s_ref[e, h] = st_new.T
            o_h = o_e if o_h is None else o_h + o_e
        o_ref[:, vsl] = _head_norm_gate(o_h, gain_ref[:, vsl], gate_ref[:, vsl])


def _gla_sample(hgrn, arrays, col_blocks, small, gain, s0, *, t_new, heads, dk, dv, q_scale):
    m = arrays[0].shape[0]
    per = SUBLANES // t_new
    kw, vw = heads * dk, heads * dv
    widths = [kw, kw, vw, vw] + ([] if hgrn else [kw])
    in_specs = [pl.BlockSpec((SUBLANES, w), functools.partial(lambda i, cb: (i, cb), cb=cb))
                for w, cb in zip(widths, col_blocks)]
    operands = list(arrays)
    if hgrn:
        in_specs.append(pl.BlockSpec(small.shape, lambda i: (0, 0)))
        operands.append(small)
    in_specs.append(pl.BlockSpec((1, vw), lambda i: (0, 0)))
    operands.append(gain)
    in_specs.append(pl.BlockSpec((per, heads, dk, dv), lambda i: (i, 0, 0, 0)))
    operands.append(s0)
    return pl.pallas_call(
        functools.partial(_gla_sample_kernel, hgrn=hgrn, heads=heads, dk=dk, dv=dv, t_new=t_new, q_scale=q_scale),
        grid=(m // SUBLANES,),
        in_specs=in_specs,
        out_specs=[pl.BlockSpec((SUBLANES, vw), lambda i: (i, 0)),
                   pl.BlockSpec((per, heads, dk, dv), lambda i: (i, 0, 0, 0))],
        out_shape=[jax.ShapeDtypeStruct((m, vw), F32), jax.ShapeDtypeStruct(s0.shape, F32)],
        compiler_params=_cparams(("arbitrary",)),
        name="hgrn_sample" if hgrn else "gla_sample",
    )(*operands)


def _fox_prompt_kernel(qi_ref, ki_ref, q_ref, k_ref, v_ref, gate_ref, c_ref, ct_ref, o_ref, m_ref, l_ref, acc_ref,
                       *, tq, heads, dh):
    p = pl.program_id(1)
    qi = qi_ref[p]
    ki = ki_ref[p]
    scale = dh ** -0.5

    @pl.when(ki == 0)
    def _():
        m_ref[...] = jnp.full_like(m_ref, -jnp.inf)
        l_ref[...] = jnp.zeros_like(l_ref)
        acc_ref[...] = jnp.zeros_like(acc_ref)

    def step(diag):
        if diag:
            r_i = lax.broadcasted_iota(jnp.int32, (tq, tq), 0)
            c_i = lax.broadcasted_iota(jnp.int32, (tq, tq), 1)
            causal = r_i >= c_i
        for h in range(heads):
            sl = slice(h * dh, (h + 1) * dh)
            q = q_ref[:, sl].astype(BF16)
            k = k_ref[:, sl].astype(BF16)
            s = lax.dot_general(q, k, _NT, preferred_element_type=F32) * scale
            logits = s + (c_ref[:, h:h + 1] - ct_ref[0, h:h + 1, :])
            if diag:
                logits = jnp.where(causal, logits, -jnp.inf)
            m_prev = m_ref[h]
            m_new = jnp.maximum(m_prev, jnp.max(logits, axis=-1, keepdims=True))
            alpha = jnp.exp(m_prev - m_new)
            pr = jnp.exp(logits - m_new)
            l_ref[h] = alpha * l_ref[h] + jnp.sum(pr, axis=-1, keepdims=True)
            acc_ref[:, sl] = alpha * acc_ref[:, sl] + jnp.dot(pr.astype(BF16), v_ref[:, sl].astype(BF16),
                                                             preferred_element_type=F32)
            m_ref[h] = m_new

    @pl.when(ki < qi)
    def _():
        step(False)

    @pl.when(ki == qi)
    def _():
        step(True)
        for h in range(heads):
            sl = slice(h * dh, (h + 1) * dh)
            o_ref[:, sl] = acc_ref[:, sl] / l_ref[h] * _silu(gate_ref[:, sl])


def _fox_prompt(proj, c, ct, *, bsz, seq, tq, q_blk, k_blk, v_blk, g_blk):
    nq = seq // tq
    pairs = [(qi, ki) for qi in range(nq) for ki in range(qi + 1)]
    qi_tab = jnp.asarray(np.array([p[0] for p in pairs], np.int32))
    ki_tab = jnp.asarray(np.array([p[1] for p in pairs], np.int32))
    w = B_WIDTH
    q_map = lambda cb: (lambda b, p, qt, kt: (b * nq + qt[p], cb))
    k_map = lambda cb: (lambda b, p, qt, kt: (b * nq + kt[p], cb))
    grid_spec = pltpu.PrefetchScalarGridSpec(
        num_scalar_prefetch=2,
        grid=(bsz, len(pairs)),
        in_specs=[pl.BlockSpec((tq, w), q_map(q_blk)), pl.BlockSpec((tq, w), k_map(k_blk)),
                  pl.BlockSpec((tq, w), k_map(v_blk)), pl.BlockSpec((tq, w), q_map(g_blk)),
                  pl.BlockSpec((tq, LANES), q_map(0)),
                  pl.BlockSpec((1, SUBLANES, tq), lambda b, p, qt, kt: (b, 0, kt[p]))],
        out_specs=pl.BlockSpec((tq, w), q_map(0)),
        scratch_shapes=[pltpu.VMEM((B_HEADS, tq, 1), F32), pltpu.VMEM((B_HEADS, tq, 1), F32),
                        pltpu.VMEM((tq, w), F32)],
    )
    return pl.pallas_call(
        functools.partial(_fox_prompt_kernel, tq=tq, heads=B_HEADS, dh=B_HEAD_DIM),
        grid_spec=grid_spec,
        out_shape=jax.ShapeDtypeStruct((bsz * seq, w), F32),
        compiler_params=_cparams(("arbitrary", "arbitrary")),
        name="fox_prompt",
    )(qi_tab, ki_tab, proj, proj, proj, proj, c, ct)


def _page_suffix_kernel(lf_ref, later_ref, total_ref, within_ref, tot_ref):
    lf = lf_ref[...]
    within_ref[...] = _x_dot01(lf, later_ref[...])
    tot_ref[...] = _x_dot01(lf, total_ref[...])


def _page_suffix(lf_pages, heads):
    n_pool, pw = lf_pages.shape
    tm = 512 if n_pool % 512 == 0 else n_pool
    j = np.arange(pw)
    head_eq = (j[:, None] % heads) == (j[None, :] % heads)
    later = jnp.asarray(head_eq & ((j[:, None] // heads) > (j[None, :] // heads)), BF16)
    total = jnp.asarray(head_eq, BF16)
    row = lambda i: (i, 0)
    const = lambda i: (0, 0)
    return pl.pallas_call(
        _page_suffix_kernel,
        grid=(n_pool // tm,),
        in_specs=[pl.BlockSpec((tm, pw), row), pl.BlockSpec((pw, pw), const), pl.BlockSpec((pw, pw), const)],
        out_specs=[pl.BlockSpec((tm, pw), row), pl.BlockSpec((tm, pw), row)],
        out_shape=[jax.ShapeDtypeStruct((n_pool, pw), F32)] * 2,
        compiler_params=_cparams(("arbitrary",)),
        name="page_suffix",
    )(lf_pages, later, total)


def _fox_sample_kernel(pt_ref, *refs, pages_per_step, heads, dh, t_new):
    pp = pages_per_step
    k_refs = refs[0:pp]
    v_refs = refs[pp:2 * pp]
    within_refs = refs[2 * pp:3 * pp]
    tot_refs = refs[3 * pp:4 * pp]
    q_ref, kn_ref, vn_ref, gate_ref, cn_col_ref, cn_row_ref, o_ref, m_ref, l_ref, acc_ref, carry_ref = refs[4 * pp:]
    j = pl.program_id(1)
    nrow = t_new * heads
    pw = PAGE_SIZE * heads
    scale = dh ** -0.5

    @pl.when(j == 0)
    def _():
        m_ref[...] = jnp.full_like(m_ref, -jnp.inf)
        l_ref[...] = jnp.zeros_like(l_ref)
        acc_ref[...] = jnp.zeros_like(acc_ref)
        carry_ref[...] = jnp.zeros_like(carry_ref)

    q = (q_ref[0] * scale).astype(BF16)
    cn_col = cn_col_ref[0]
    row_head = lax.broadcasted_iota(jnp.int32, (nrow, pw), 0) % heads
    col_head = lax.broadcasted_iota(jnp.int32, (nrow, pw), 1) % heads
    same_head = row_head == col_head

    def online(logits, v):
        m_prev = m_ref[...]
        m_new = jnp.maximum(m_prev, jnp.max(logits, axis=-1, keepdims=True))
        alpha = jnp.exp(m_prev - m_new)
        pr = jnp.exp(logits - m_new)
        l_ref[...] = alpha * l_ref[...] + jnp.sum(pr, axis=-1, keepdims=True)
        acc_ref[...] = alpha * acc_ref[...] + jnp.dot(pr.astype(BF16), v.astype(BF16), preferred_element_type=F32)
        m_ref[...] = m_new

    for i in range(pp):
        suffix = carry_ref[...] + within_refs[i][0]
        carry_ref[...] = carry_ref[...] + tot_refs[i][0]
        s = lax.dot_general(q, k_refs[i][0].astype(BF16), _NT, preferred_element_type=F32)
        logits = s + cn_col + suffix
        online(jnp.where(same_head, logits, -jnp.inf), v_refs[i][0])

    @pl.when(j == pl.num_programs(1) - 1)
    def _():
        s = lax.dot_general(q, kn_ref[0].astype(BF16), _NT, preferred_element_type=F32)
        logits = s + cn_col - cn_row_ref[0]
        r = lax.broadcasted_iota(jnp.int32, (nrow, nrow), 0)
        c = lax.broadcasted_iota(jnp.int32, (nrow, nrow), 1)
        keep = jnp.logical_and((r % heads) == (c % heads), (r // heads) >= (c // heads))
        online(jnp.where(keep, logits, -jnp.inf), vn_ref[0])
        o_ref[0] = acc_ref[...] / l_ref[...] * _silu(gate_ref[0])


def _fox_sample(page_table, k_pages, v_pages, lf_pages, q, k_new, v_new, gate, cn_col, cn_row, *, pages_per_step):
    n_pool = lf_pages.shape[0]
    within, tot = (a.reshape(n_pool, 1, -1) for a in _page_suffix(lf_pages, B_HEADS))
    db, n_pages = page_table.shape
    pp = pages_per_step
    nrow = q.shape[1]
    pw = PAGE_SIZE * B_HEADS
    dh = B_HEAD_DIM
    page = lambda i: (lambda b, j, pt: (pt[b, n_pages - 1 - (j * pp + i)], 0, 0))
    per_b = lambda b, j, pt: (b, 0, 0)
    in_specs = ([pl.BlockSpec((1, pw, dh), page(i)) for i in range(pp)]
                + [pl.BlockSpec((1, pw, dh), page(i)) for i in range(pp)]
                + [pl.BlockSpec((1, 1, pw), page(i)) for i in range(pp)]
                + [pl.BlockSpec((1, 1, pw), page(i)) for i in range(pp)]
                + [pl.BlockSpec((1, nrow, dh), per_b)] * 4
                + [pl.BlockSpec((1, nrow, 1), per_b), pl.BlockSpec((1, 1, nrow), per_b)])
    grid_spec = pltpu.PrefetchScalarGridSpec(
        num_scalar_prefetch=1,
        grid=(db, n_pages // pp),
        in_specs=in_specs,
        out_specs=pl.BlockSpec((1, nrow, dh), per_b),
        scratch_shapes=[pltpu.VMEM((nrow, 1), F32), pltpu.VMEM((nrow, 1), F32), pltpu.VMEM((nrow, dh), F32),
                        pltpu.VMEM((1, pw), F32)],
    )
    return pl.pallas_call(
        functools.partial(_fox_sample_kernel, pages_per_step=pp, heads=B_HEADS, dh=dh, t_new=nrow // B_HEADS),
        grid_spec=grid_spec,
        out_shape=jax.ShapeDtypeStruct((db, nrow, dh), F32),
        compiler_params=_cparams(("arbitrary", "arbitrary")),
        name="fox_sample",
    )(page_table, *([k_pages] * pp), *([v_pages] * pp), *([within] * pp), *([tot] * pp), q, k_new, v_new, gate,
      cn_col, cn_row)


def _out_proj_kernel(*refs, n_in, final):
    ins = refs[:n_in]
    w_ref, x_ref = refs[n_in], refs[n_in + 1]
    y = x_ref[...]
    k0 = 0
    for a_ref in ins:
        kw = a_ref.shape[1]
        y = y + jnp.dot(a_ref[...].astype(BF16), w_ref[k0:k0 + kw, :], preferred_element_type=F32)
        k0 += kw
    if final:
        g_ref, o_ref = refs[n_in + 2], refs[n_in + 3]
        o_ref[...] = _rmsnorm_rows(y, g_ref[...])
    else:
        refs[n_in + 2][...] = y


def _out_proj(ins, w, x, final_gain, *, tm):
    m = x.shape[0]
    const = lambda i: (0, 0)
    row = lambda i: (i, 0)
    in_specs = [pl.BlockSpec((tm, a.shape[1]), row) for a in ins]
    in_specs += [pl.BlockSpec(w.shape, const), pl.BlockSpec((tm, D_MODEL), row)]
    operands = list(ins) + [w, x]
    if final_gain is not None:
        in_specs.append(pl.BlockSpec((1, D_MODEL), const))
        operands.append(final_gain)
    return pl.pallas_call(
        functools.partial(_out_proj_kernel, n_in=len(ins), final=final_gain is not None),
        grid=(m // tm,),
        in_specs=in_specs,
        out_specs=pl.BlockSpec((tm, D_MODEL), row),
        out_shape=jax.ShapeDtypeStruct((m, D_MODEL), F32),
        compiler_params=_cparams(("arbitrary",)),
        name="out_proj_final" if final_gain is not None else "out_proj",
    )(*operands)


def _pad_cols(w, n):
    return jnp.pad(w, ((0, 0), (0, n - w.shape[1])))


def _trunk(x, weights, *, bsz, seq, prompt, sample_ctx):
    (norm_even, w_in_even, b_fox_f, lb_logits, hgrn_gain, w_out_even, norm_odd, w_in_odd, w_gla_gate, b_gla_gate,
     gla_gain, w_out_odd, final_norm) = weights
    m = bsz * seq
    tm = 512 if m % 512 == 0 else m
    x2 = x.reshape(m, D_MODEL)

    n_main = 4 * A_WIDTH + 4 * B_WIDTH
    w_e = w_in_even[0]
    proj, lf, c = _even_proj(
        x2, norm_even[0].reshape(1, D_MODEL), w_e[:, :n_main].astype(BF16),
        _pad_cols(w_e[:, n_main:], LANES).astype(BF16), _pad_cols(b_fox_f[0].reshape(1, B_HEADS), LANES),
        seq=seq, tm=tm)
    lb = lb_logits
    gain_a = hgrn_gain[0].reshape(1, A_WIDTH)
    hg = dict(heads=A_HEADS, dk=A_HEAD_DIM, dv=A_HEAD_DIM, q_scale=1.0)
    if prompt:
        o_a, s_a = _gla_prompt(True, [proj] * 4, [0, 1, 2, 3], lb, gain_a, bsz=bsz, seq=seq, tb=512, **hg)
        ct = jnp.pad(c[:, :B_HEADS].reshape(bsz, seq, B_HEADS).transpose(0, 2, 1),
                     ((0, 0), (0, SUBLANES - B_HEADS), (0, 0)))
        o_b = _fox_prompt(proj, c, ct, bsz=bsz, seq=seq, tq=512, q_blk=4, k_blk=5, v_blk=6, g_blk=7)
    else:
        o_a, s_a = _gla_sample(True, [proj] * 4, [0, 1, 2, 3], lb, gain_a, sample_ctx["state_hgrn"],
                               t_new=seq, **hg)
        nrow = seq * B_HEADS
        blk = lambda i: proj[:, i * B_WIDTH:(i + 1) * B_WIDTH].reshape(bsz, nrow, B_HEAD_DIM)
        cn = c[:, :B_HEADS].reshape(bsz, nrow)
        o_b = _fox_sample(sample_ctx["page_table"], sample_ctx["k_pages"], sample_ctx["v_pages"],
                          sample_ctx["lf_pages"], blk(4), blk(5), blk(6), blk(7),
                          cn.reshape(bsz, nrow, 1), cn.reshape(bsz, 1, nrow), pages_per_step=4)
        o_b = o_b.reshape(m, B_WIDTH)
    x1 = _out_proj([o_a, o_b], w_out_even[0].astype(BF16), x2, None, tm=tm)
    kb = proj[:, 5 * B_WIDTH:6 * B_WIDTH].reshape(bsz, seq, B_HEADS, B_HEAD_DIM)
    vb = proj[:, 6 * B_WIDTH:7 * B_WIDTH].reshape(bsz, seq, B_HEADS, B_HEAD_DIM)
    lfb = lf[:, :B_HEADS].reshape(bsz, seq, B_HEADS)

    n_main = 2 * C_KEY_WIDTH + 2 * C_VAL_WIDTH
    w_o = w_in_odd[0]
    w_gate = jnp.pad(w_gla_gate[0], ((0, LANES - C_GATE_RANK), (0, 0))).astype(BF16)
    proj1, lf1 = _odd_proj(x1, norm_odd[0].reshape(1, D_MODEL), w_o[:, :n_main].astype(BF16),
                           _pad_cols(w_o[:, n_main:], LANES).astype(BF16), w_gate,
                           b_gla_gate[0].reshape(1, C_KEY_WIDTH), tm=tm)
    gain_c = gla_gain[0].reshape(1, C_VAL_WIDTH)
    gl = dict(heads=C_HEADS, dk=C_KEY_DIM, dv=C_VAL_DIM, q_scale=C_KEY_DIM ** -0.5)
    arrays = [proj1, proj1, proj1, proj1, lf1]
    col_blocks = [0, 1, 1, 2, 0]
    if prompt:
        o_c, s_c = _gla_prompt(False, arrays, col_blocks, None, gain_c, bsz=bsz, seq=seq, tb=512, **gl)
    else:
        o_c, s_c = _gla_sample(False, arrays, col_blocks, None, gain_c, sample_ctx["state_gla"], t_new=seq, **gl)
    y = _out_proj([o_c], w_out_odd[0].astype(BF16), x1, final_norm.reshape(1, D_MODEL), tm=tm)
    return y.reshape(bsz, seq, D_MODEL), kb, vb, lfb, s_a, s_c


def kernel(x_prompt, x_sample, cache_fox_k, cache_fox_v, cache_fox_logf, state_hgrn, state_gla, page_table,
           norm_even, w_in_even, b_fox_f, lb_logits, hgrn_gain, w_out_even, norm_odd, w_in_odd, w_gla_gate,
           b_gla_gate, gla_gain, w_out_odd, final_norm):
    weights = (norm_even, w_in_even, b_fox_f, lb_logits, hgrn_gain, w_out_even, norm_odd, w_in_odd, w_gla_gate,
               b_gla_gate, gla_gain, w_out_odd, final_norm)
    bsz, seq, _ = x_prompt.shape
    y_p, kp, vp, lfp, hgrn_p, gla_p = _trunk(x_prompt, weights, bsz=bsz, seq=seq, prompt=True, sample_ctx=None)
    n_pp = seq // PAGE_SIZE
    fox_k_prompt = kp.reshape(1, bsz, n_pp, PAGE_SIZE, B_HEADS, B_HEAD_DIM)
    fox_v_prompt = vp.reshape(1, bsz, n_pp, PAGE_SIZE, B_HEADS, B_HEAD_DIM)
    fox_logf_prompt = lfp.reshape(1, bsz, n_pp, PAGE_SIZE, B_HEADS)

    db, t_new, _ = x_sample.shape
    n_pool = cache_fox_k.shape[1]
    pw = PAGE_SIZE * B_HEADS
    ctx = dict(
        page_table=page_table,
        k_pages=cache_fox_k[0].reshape(n_pool, pw, B_HEAD_DIM),
        v_pages=cache_fox_v[0].reshape(n_pool, pw, B_HEAD_DIM),
        lf_pages=cache_fox_logf[0].reshape(n_pool, pw),
        state_hgrn=state_hgrn[0], state_gla=state_gla[0])
    y_s, ks, vs, lfs, hgrn_s, gla_s = _trunk(x_sample, weights, bsz=db, seq=t_new, prompt=False, sample_ctx=ctx)
    return (y_p, y_s, fox_k_prompt, fox_v_prompt, fox_logf_prompt, hgrn_p[None], gla_p[None],
            ks[None], vs[None], lfs[None], hgrn_s[None], gla_s[None])
```

```python
import functools

import numpy as np
import jax
import jax.numpy as jnp
from jax import lax
from jax.experimental import pallas as pl
from jax.experimental.pallas import tpu as pltpu

F32 = jnp.float32
BF16 = jnp.bfloat16

D_MODEL = 1024
PAGE_SIZE = 128
A_HEADS = 4
A_HEAD_DIM = 128
A_WIDTH = 512
B_HEADS = 4
B_HEAD_DIM = 128
B_WIDTH = 512
C_HEADS = 4
C_KEY_WIDTH = 512
C_VAL_WIDTH = 1024
C_KEY_DIM = 128
C_VAL_DIM = 256
C_GATE_RANK = 16
GLA_GATE_NORMALIZER = 16.0
EPS = 1e-6
HGRN_LAYER = 0
LOG2E = 1.4426950408889634

LANES = 128
SUBLANES = 8
VMEM_LIMIT = 56 * 1024 * 1024
CHUNK = 64
MAX_CHUNK_LOG_DECAY = 60.0

_NT = (((1,), (1,)), ((), ()))
_TN = (((0,), (0,)), ((), ()))


def _cparams(sem):
    return pltpu.CompilerParams(dimension_semantics=sem, vmem_limit_bytes=VMEM_LIMIT)


def _sigmoid(x):
    return 1.0 / (1.0 + jnp.exp(-x))


def _log_sigmoid(x):
    return jnp.minimum(x, 0.0) - jnp.log1p(jnp.exp(-jnp.abs(x)))


def _silu(x):
    return x * _sigmoid(x)


def _rmsnorm_rows(x, g):
    return x * lax.rsqrt(jnp.mean(x * x, axis=-1, keepdims=True) + EPS) * g


def _split3(x):
    p1 = x.astype(BF16)
    r1 = x - p1.astype(F32)
    p2 = r1.astype(BF16)
    p3 = (r1 - p2.astype(F32)).astype(BF16)
    return p1, p2, p3


def _dot01(m01, x):
    acc = None
    for p in _split3(x):
        t = jnp.dot(m01, p, preferred_element_type=F32)
        acc = t if acc is None else acc + t
    return acc


def _x_dot01(x, m01):
    acc = None
    for p in _split3(x):
        t = jnp.dot(p, m01, preferred_element_type=F32)
        acc = t if acc is None else acc + t
    return acc


def _lower_tri(n, seq):
    r = lax.broadcasted_iota(jnp.int32, (n, n), 0)
    c = lax.broadcasted_iota(jnp.int32, (n, n), 1)
    keep = r >= c
    if seq < n:
        keep = jnp.logical_and(keep, (r // seq) == (c // seq))
    return jnp.where(keep, 1.0, 0.0).astype(BF16)


def _proj_cols(h, w_ref, out_ref):
    n = w_ref.shape[1]
    step = 512
    for c0 in range(0, n, step):
        out_ref[:, c0:c0 + step] = jnp.dot(h, w_ref[:, c0:c0 + step], preferred_element_type=F32)


def _cast_cols_kernel(w_ref, o_ref, *, valid):
    w = w_ref[...]
    if valid < w.shape[1]:
        lane = lax.broadcasted_iota(jnp.int32, w.shape, 1)
        w = jnp.where(lane < valid, w, 0.0)
    o_ref[...] = w.astype(BF16)


def _cast_cols(w, col0, ncols, tc):
    rows = w.shape[0]
    nblk = -(-ncols // tc)
    return pl.pallas_call(
        functools.partial(_cast_cols_kernel, valid=ncols - (nblk - 1) * tc),
        grid=(nblk,),
        in_specs=[pl.BlockSpec((rows, tc), lambda j: (0, col0 // tc + j))],
        out_specs=pl.BlockSpec((rows, tc), lambda j: (0, j)),
        out_shape=jax.ShapeDtypeStruct((rows, nblk * tc), BF16),
        compiler_params=_cparams(("arbitrary",)),
        name="cast_cols",
    )(w)


def _even_proj_kernel(x_ref, g_ref, w_ref, wfb_ref, bfox_ref, pa_ref, gate_ref, qkv_ref, kout_ref, vout_ref,
                      lf_ref, c_ref, carry_ref, *, tm, seq):
    i = pl.program_id(0)
    h = _rmsnorm_rows(x_ref[...], g_ref[...]).astype(BF16)
    na = 4 * A_WIDTH
    bw = B_WIDTH
    for c0 in range(0, na, 512):
        pa_ref[:, c0:c0 + 512] = jnp.dot(h, w_ref[:, c0:c0 + 512], preferred_element_type=F32)
    q = jnp.dot(h, w_ref[:, na:na + bw], preferred_element_type=F32)
    qkv_ref[:, 0:bw] = (q * (B_HEAD_DIM ** -0.5 * LOG2E)).astype(BF16)
    for j, out_ref in ((1, kout_ref), (2, vout_ref)):
        kv = jnp.dot(h, w_ref[:, na + j * bw:na + (j + 1) * bw], preferred_element_type=F32)
        qkv_ref[:, j * bw:(j + 1) * bw] = kv.astype(BF16)
        for hd in range(B_HEADS):
            out_ref[pl.ds(hd, tm, stride=B_HEADS), :] = kv[:, hd * B_HEAD_DIM:(hd + 1) * B_HEAD_DIM]
    gate_ref[...] = jnp.dot(h, w_ref[:, na + 3 * bw:na + 4 * bw], preferred_element_type=F32)
    fb = jnp.dot(h, wfb_ref[...], preferred_element_type=F32) + bfox_ref[...]
    lane = lax.broadcasted_iota(jnp.int32, fb.shape, 1)
    lf = jnp.where(lane < B_HEADS, _log_sigmoid(fb), 0.0)
    lf_ref[...] = lf
    cs = _dot01(_lower_tri(tm, seq), lf)
    if seq > tm:
        @pl.when((i * tm) % seq == 0)
        def _():
            carry_ref[...] = jnp.zeros_like(carry_ref)
        cs = cs + carry_ref[0:1, :]
        c_ref[...] = cs
        carry_ref[0:1, :] = cs[tm - 1:tm, :]
    else:
        c_ref[...] = cs


def _even_proj(x, g, w_main, w_fb, b_fox, *, seq, tm):
    m = x.shape[0]
    n = w_main.shape[1]
    const = lambda i: (0, 0)
    row = lambda i: (i, 0)
    return pl.pallas_call(
        functools.partial(_even_proj_kernel, tm=tm, seq=seq),
        grid=(m // tm,),
        in_specs=[pl.BlockSpec((tm, D_MODEL), row), pl.BlockSpec((1, D_MODEL), const),
                  pl.BlockSpec((D_MODEL, n), const), pl.BlockSpec((D_MODEL, LANES), const),
                  pl.BlockSpec((1, LANES), const)],
        out_specs=[pl.BlockSpec((tm, 4 * A_WIDTH), row), pl.BlockSpec((tm, B_WIDTH), row),
                   pl.BlockSpec((tm, 3 * B_WIDTH), row), pl.BlockSpec((tm * B_HEADS, B_HEAD_DIM), row),
                   pl.BlockSpec((tm * B_HEADS, B_HEAD_DIM), row), pl.BlockSpec((tm, LANES), row),
                   pl.BlockSpec((tm, LANES), row)],
        out_shape=[jax.ShapeDtypeStruct((m, 4 * A_WIDTH), F32), jax.ShapeDtypeStruct((m, B_WIDTH), F32),
                   jax.ShapeDtypeStruct((m, 3 * B_WIDTH), BF16),
                   jax.ShapeDtypeStruct((m * B_HEADS, B_HEAD_DIM), F32),
                   jax.ShapeDtypeStruct((m * B_HEADS, B_HEAD_DIM), F32),
                   jax.ShapeDtypeStruct((m, LANES), F32), jax.ShapeDtypeStruct((m, LANES), F32)],
        scratch_shapes=[pltpu.VMEM((SUBLANES, LANES), F32)],
        compiler_params=_cparams(("arbitrary",)),
        name="even_proj",
    )(x, g, w_main, w_fb, b_fox)


def _odd_proj_kernel(x_ref, g_ref, w_ref, wr_ref, wg_ref, bg_ref, proj_ref, lf_ref):
    h = _rmsnorm_rows(x_ref[...], g_ref[...]).astype(BF16)
    _proj_cols(h, w_ref, proj_ref)
    r = jnp.dot(h, wr_ref[...], preferred_element_type=F32)
    z = jnp.dot(r.astype(BF16), wg_ref[...], preferred_element_type=F32) + bg_ref[...]
    lf_ref[...] = _log_sigmoid(z) / GLA_GATE_NORMALIZER


def _odd_proj(x, g, w_main, w_r, w_gate, b_gate, *, tm):
    m = x.shape[0]
    n = w_main.shape[1]
    const = lambda i: (0, 0)
    row = lambda i: (i, 0)
    return pl.pallas_call(
        _odd_proj_kernel,
        grid=(m // tm,),
        in_specs=[pl.BlockSpec((tm, D_MODEL), row), pl.BlockSpec((1, D_MODEL), const),
                  pl.BlockSpec((D_MODEL, n), const), pl.BlockSpec((D_MODEL, LANES), const),
                  pl.BlockSpec((LANES, C_KEY_WIDTH), const), pl.BlockSpec((1, C_KEY_WIDTH), const)],
        out_specs=[pl.BlockSpec((tm, n), row), pl.BlockSpec((tm, C_KEY_WIDTH), row)],
        out_shape=[jax.ShapeDtypeStruct((m, n), F32), jax.ShapeDtypeStruct((m, C_KEY_WIDTH), F32)],
        compiler_params=_cparams(("arbitrary",)),
        name="odd_proj",
    )(x, g, w_main, w_r, w_gate, b_gate)


def _lower_bound(logits, layer):
    e = jnp.exp(logits - jnp.max(logits, axis=0, keepdims=True))
    return jnp.sum(e[:layer + 1, :], axis=0, keepdims=True) / jnp.sum(e, axis=0, keepdims=True)


def _hgrn_gate(fa, lb):
    f = lb + (1.0 - lb) * _sigmoid(fa)
    return jnp.log(f), 1.0 - f


def _exact_group(q, k, v, g, st, lo, hi):
    n = SUBLANES
    row = lax.broadcasted_iota(jnp.int32, (n, 1), 0)
    valid = jnp.logical_and(row >= lo, row < hi)
    q = jnp.where(valid, q, 0.0)
    k = jnp.where(valid, k, 0.0)
    g = jnp.where(valid, g, 0.0)
    b = g
    for sh in (1, 2, 4):
        b = b + jnp.where(row >= sh, pltpu.roll(b, sh, 0), 0.0)
    o = lax.dot_general((q * jnp.exp(b)).astype(BF16), st.astype(BF16), _NT, preferred_element_type=F32)
    for s in range(lo, hi):
        w = jnp.exp(jnp.minimum(b - b[s:s + 1, :], 0.0))
        a = jnp.sum(q * k[s:s + 1, :] * w, axis=-1, keepdims=True)
        o = o + jnp.where(row >= s, a, 0.0) * v[s:s + 1, :]
    b_last = b[n - 1:n, :]
    k_hat = k * jnp.exp(b_last - b)
    st_new = st * jnp.exp(b_last) + lax.dot_general(v.astype(BF16), k_hat.astype(BF16), _TN,
                                                    preferred_element_type=F32)
    return o, st_new


def _head_norm_gate(o, gain, gate):
    y = o * lax.rsqrt(jnp.mean(o * o, axis=-1, keepdims=True) + EPS) * gain
    return y * _silu(gate)


def _gla_prompt_kernel(*refs, hgrn, heads, dk, dv, tb, q_scale):
    if hgrn:
        q_ref, fa_ref, v_ref, gate_ref, lb_ref, gain_ref, o_ref, s_ref, st_ref, b_ref, g_ref, k_ref = refs
    else:
        q_ref, k_ref, v_ref, gate_ref, g_ref, gain_ref, o_ref, s_ref, st_ref, b_ref = refs
    t = pl.program_id(1)
    n_chunks = tb // CHUNK
    if hgrn:
        lb = _lower_bound(lb_ref[...], HGRN_LAYER)

    @pl.when(t == 0)
    def _():
        st_ref[...] = jnp.zeros_like(st_ref)

    tri = _lower_tri(CHUNK, CHUNK)
    b_min = None
    for c in range(n_chunks):
        rows = slice(c * CHUNK, (c + 1) * CHUNK)
        if hgrn:
            g, k = _hgrn_gate(fa_ref[rows, :], lb)
            g_ref[rows, :] = g
            k_ref[rows, :] = k
        else:
            g = g_ref[rows, :]
        b = _dot01(tri, g)
        b_ref[rows, :] = b
        b_last = b[CHUNK - 1:CHUNK, :]
        b_min = b_last if b_min is None else jnp.minimum(b_min, b_last)
    chunk_form_ok = jnp.min(b_min) >= -MAX_CHUNK_LOG_DECAY

    def finish(rows, h, o_h):
        vsl = slice(h * dv, (h + 1) * dv)
        o_ref[rows, vsl] = _head_norm_gate(o_h, gain_ref[:, vsl], gate_ref[rows, vsl])

    @pl.when(chunk_form_ok)
    def _():
        r_i = lax.broadcasted_iota(jnp.int32, (CHUNK, CHUNK), 0)
        c_i = lax.broadcasted_iota(jnp.int32, (CHUNK, CHUNK), 1)
        causal = r_i >= c_i

        def chunk(c, carry):
            rows = pl.ds(pl.multiple_of(c * CHUNK, CHUNK), CHUNK)
            b = b_ref[rows, :]
            e_b = jnp.exp(b)
            b_last = b[CHUNK - 1:CHUNK, :]
            e_last = jnp.exp(b_last)
            q_t = q_ref[rows, :] * q_scale * e_b
            k_t = k_ref[rows, :] * jnp.exp(-b)
            k_hat = k_t * e_last
            for h in range(heads):
                ksl = slice(h * dk, (h + 1) * dk)
                vsl = slice(h * dv, (h + 1) * dv)
                qh = q_t[:, ksl].astype(BF16)
                vh = v_ref[rows, vsl].astype(BF16)
                a = lax.dot_general(qh, k_t[:, ksl].astype(BF16), _NT, preferred_element_type=F32)
                a = jnp.where(causal, a, 0.0).astype(BF16)
                st = st_ref[h]
                o_h = jnp.dot(a, vh, preferred_element_type=F32) + lax.dot_general(
                    qh, st.astype(BF16), _NT, preferred_element_type=F32)
                st_ref[h] = st * e_last[:, ksl] + lax.dot_general(
                    vh, k_hat[:, ksl].astype(BF16), _TN, preferred_element_type=F32)
                finish(rows, h, o_h)
            return carry

        lax.fori_loop(0, n_chunks, chunk, 0)

    @pl.when(jnp.logical_not(chunk_form_ok))
    def _():
        def group(i, carry):
            rows = pl.ds(pl.multiple_of(i * SUBLANES, SUBLANES), SUBLANES)
            for h in range(heads):
                ksl = slice(h * dk, (h + 1) * dk)
                vsl = slice(h * dv, (h + 1) * dv)
                o_h, st_new = _exact_group(q_ref[rows, ksl] * q_scale, k_ref[rows, ksl], v_ref[rows, vsl],
                                           g_ref[rows, ksl], st_ref[h], 0, SUBLANES)
                st_ref[h] = st_new
                finish(rows, h, o_h)
            return carry

        lax.fori_loop(0, tb // SUBLANES, group, 0)

    @pl.when(t == pl.num_programs(1) - 1)
    def _():
        for h in range(heads):
            s_ref[0, h] = st_ref[h].T


def _gla_prompt(hgrn, arrays, col_blocks, small, gain, *, bsz, seq, heads, dk, dv, tb, q_scale):
    nt = seq // tb
    kw, vw = heads * dk, heads * dv
    widths = [kw, kw, vw, vw] + ([] if hgrn else [kw])
    in_specs = [pl.BlockSpec((tb, w), functools.partial(lambda b, t, cb: (b * nt + t, cb), cb=cb))
                for w, cb in zip(widths, col_blocks)]
    operands = list(arrays)
    if hgrn:
        in_specs.append(pl.BlockSpec(small.shape, lambda b, t: (0, 0)))
        operands.append(small)
    in_specs.append(pl.BlockSpec((1, vw), lambda b, t: (0, 0)))
    operands.append(gain)
    scratch = [pltpu.VMEM((heads, dv, dk), F32), pltpu.VMEM((tb, kw), F32)]
    if hgrn:
        scratch += [pltpu.VMEM((tb, kw), F32), pltpu.VMEM((tb, kw), F32)]
    return pl.pallas_call(
        functools.partial(_gla_prompt_kernel, hgrn=hgrn, heads=heads, dk=dk, dv=dv, tb=tb, q_scale=q_scale),
        grid=(bsz, nt),
        in_specs=in_specs,
        out_specs=[pl.BlockSpec((tb, vw), lambda b, t: (b * nt + t, 0)),
                   pl.BlockSpec((1, heads, dk, dv), lambda b, t: (b, 0, 0, 0))],
        out_shape=[jax.ShapeDtypeStruct((bsz * seq, vw), F32), jax.ShapeDtypeStruct((bsz, heads, dk, dv), F32)],
        scratch_shapes=scratch,
        compiler_params=_cparams(("arbitrary", "arbitrary")),
        name="hgrn_prompt" if hgrn else "gla_prompt",
    )(*operands)


def _gla_sample_kernel(*refs, hgrn, heads, dk, dv, t_new, q_scale):
    if hgrn:
        q_ref, fa_ref, v_ref, gate_ref, lb_ref, gain_ref, s0_ref, o_ref, s_ref = refs
    else:
        q_ref, k_ref, v_ref, gate_ref, g_ref, gain_ref, s0_ref, o_ref, s_ref = refs
    for h in range(heads):
        ksl = slice(h * dk, (h + 1) * dk)
        vsl = slice(h * dv, (h + 1) * dv)
        if hgrn:
            g, k = _hgrn_gate(fa_ref[:, ksl], _lower_bound(lb_ref[...], HGRN_LAYER)[:, ksl])
        else:
            g, k = g_ref[:, ksl], k_ref[:, ksl]
        q = q_ref[:, ksl] * q_scale
        v = v_ref[:, vsl]
        o_h = None
        for e in range(SUBLANES // t_new):
            o_e, st_new = _exact_group(q, k, v, g, s0_ref[e, h].T, e * t_new, (e + 1) * t_new)
            s_ref[e, h] = st_new.T
            o_h = o_e if o_h is None else o_h + o_e
        o_ref[:, vsl] = _head_norm_gate(o_h, gain_ref[:, vsl], gate_ref[:, vsl])


def _gla_sample(hgrn, arrays, col_blocks, small, gain, s0, *, t_new, heads, dk, dv, q_scale):
    m = arrays[0].shape[0]
    per = SUBLANES // t_new
    kw, vw = heads * dk, heads * dv
    widths = [kw, kw, vw, vw] + ([] if hgrn else [kw])
    in_specs = [pl.BlockSpec((SUBLANES, w), functools.partial(lambda i, cb: (i, cb), cb=cb))
                for w, cb in zip(widths, col_blocks)]
    operands = list(arrays)
    if hgrn:
        in_specs.append(pl.BlockSpec(small.shape, lambda i: (0, 0)))
        operands.append(small)
    in_specs.append(pl.BlockSpec((1, vw), lambda i: (0, 0)))
    operands.append(gain)
    in_specs.append(pl.BlockSpec((per, heads, dk, dv), lambda i: (i, 0, 0, 0)))
    operands.append(s0)
    return pl.pallas_call(
        functools.partial(_gla_sample_kernel, hgrn=hgrn, heads=heads, dk=dk, dv=dv, t_new=t_new, q_scale=q_scale),
        grid=(m // SUBLANES,),
        in_specs=in_specs,
        out_specs=[pl.BlockSpec((SUBLANES, vw), lambda i: (i, 0)),
                   pl.BlockSpec((per, heads, dk, dv), lambda i: (i, 0, 0, 0))],
        out_shape=[jax.ShapeDtypeStruct((m, vw), F32), jax.ShapeDtypeStruct(s0.shape, F32)],
        compiler_params=_cparams(("arbitrary",)),
        name="hgrn_sample" if hgrn else "gla_sample",
    )(*operands)


def _fox_prompt_kernel(qi_ref, ki_ref, q_ref, k_ref, v_ref, gate_ref, c_ref, ct_ref, o_ref, m_ref, l_ref, acc_ref,
                       cq_ref, *, tq, heads, dh):
    p = pl.program_id(1)
    qi = qi_ref[p]
    ki = ki_ref[p]
    ncb = tq // LANES

    @pl.when(ki == 0)
    def _():
        m_ref[...] = jnp.full_like(m_ref, -jnp.inf)
        l_ref[...] = jnp.zeros_like(l_ref)
        acc_ref[...] = jnp.zeros_like(acc_ref)
        for h in range(heads):
            cq_ref[h] = jnp.broadcast_to(c_ref[:, h:h + 1] * LOG2E, (tq, LANES))

    def step(diag):
        if diag:
            r_i = lax.broadcasted_iota(jnp.int32, (tq, LANES), 0)
            c_i = lax.broadcasted_iota(jnp.int32, (tq, LANES), 1)
        ck_all = ct_ref[0] * LOG2E
        for h in range(heads):
            sl = slice(h * dh, (h + 1) * dh)
            s = lax.dot_general(q_ref[:, sl], k_ref[:, sl], _NT, preferred_element_type=F32)
            cq = cq_ref[h]
            blocks = []
            m_cur = None
            for j in range(ncb):
                cs = slice(j * LANES, (j + 1) * LANES)
                lg = s[:, cs] + (cq - ck_all[h:h + 1, cs])
                if diag:
                    lg = jnp.where(r_i >= c_i + j * LANES, lg, -jnp.inf)
                blocks.append(lg)
                m_cur = lg if m_cur is None else jnp.maximum(m_cur, lg)
            m_prev = m_ref[h]
            m_new = jnp.maximum(m_prev, jnp.max(m_cur, axis=-1, keepdims=True))
            alpha = jnp.exp2(m_prev - m_new)
            probs = [jnp.exp2(lg - m_new) for lg in blocks]
            row_sum = probs[0]
            for pj in probs[1:]:
                row_sum = row_sum + pj
            l_ref[h] = alpha * l_ref[h] + jnp.sum(row_sum, axis=-1, keepdims=True)
            pr = jnp.concatenate([pj.astype(BF16) for pj in probs], axis=1)
            acc_ref[:, sl] = alpha * acc_ref[:, sl] + jnp.dot(pr, v_ref[:, sl], preferred_element_type=F32)
            m_ref[h] = m_new

    @pl.when(ki < qi)
    def _():
        step(False)

    @pl.when(ki == qi)
    def _():
        step(True)
        for h in range(heads):
            sl = slice(h * dh, (h + 1) * dh)
            o_ref[:, sl] = acc_ref[:, sl] / l_ref[h] * _silu(gate_ref[:, sl])


def _fox_prompt(qkv, gate, c, ct, *, bsz, seq, tq):
    nq = seq // tq
    pairs = [(qi, ki) for qi in range(nq) for ki in range(qi + 1)]
    qi_tab = jnp.asarray(np.array([p[0] for p in pairs], np.int32))
    ki_tab = jnp.asarray(np.array([p[1] for p in pairs], np.int32))
    w = B_WIDTH
    q_map = lambda cb: (lambda b, p, qt, kt: (b * nq + qt[p], cb))
    k_map = lambda cb: (lambda b, p, qt, kt: (b * nq + kt[p], cb))
    grid_spec = pltpu.PrefetchScalarGridSpec(
        num_scalar_prefetch=2,
        grid=(bsz, len(pairs)),
        in_specs=[pl.BlockSpec((tq, w), q_map(0)), pl.BlockSpec((tq, w), k_map(1)),
                  pl.BlockSpec((tq, w), k_map(2)), pl.BlockSpec((tq, w), q_map(0)),
                  pl.BlockSpec((tq, LANES), q_map(0)),
                  pl.BlockSpec((1, SUBLANES, tq), lambda b, p, qt, kt: (b, 0, kt[p]))],
        out_specs=pl.BlockSpec((tq, w), q_map(0)),
        scratch_shapes=[pltpu.VMEM((B_HEADS, tq, LANES), F32), pltpu.VMEM((B_HEADS, tq, LANES), F32),
                        pltpu.VMEM((tq, w), F32), pltpu.VMEM((B_HEADS, tq, LANES), F32)],
    )
    return pl.pallas_call(
        functools.partial(_fox_prompt_kernel, tq=tq, heads=B_HEADS, dh=B_HEAD_DIM),
        grid_spec=grid_spec,
        out_shape=jax.ShapeDtypeStruct((bsz * seq, w), F32),
        compiler_params=_cparams(("arbitrary", "arbitrary")),
        name="fox_prompt",
    )(qi_tab, ki_tab, qkv, qkv, qkv, gate, c, ct)


def _page_suffix_kernel(lf_ref, later_ref, total_ref, out_ref):
    lf = lf_ref[...]
    pw = lf.shape[1]
    out_ref[:, 0:pw] = _x_dot01(lf, later_ref[...])
    out_ref[:, pw:2 * pw] = _x_dot01(lf, total_ref[...])


def _page_suffix(lf_pages, heads):
    n_pool, pw = lf_pages.shape
    tm = 512 if n_pool % 512 == 0 else n_pool
    j = np.arange(pw)
    head_eq = (j[:, None] % heads) == (j[None, :] % heads)
    later = jnp.asarray(head_eq & ((j[:, None] // heads) > (j[None, :] // heads)), BF16)
    total = jnp.asarray(head_eq, BF16)
    row = lambda i: (i, 0)
    const = lambda i: (0, 0)
    return pl.pallas_call(
        _page_suffix_kernel,
        grid=(n_pool // tm,),
        in_specs=[pl.BlockSpec((tm, pw), row), pl.BlockSpec((pw, pw), const), pl.BlockSpec((pw, pw), const)],
        out_specs=pl.BlockSpec((tm, 2 * pw), row),
        out_shape=jax.ShapeDtypeStruct((n_pool, 2 * pw), F32),
        compiler_params=_cparams(("arbitrary",)),
        name="page_suffix",
    )(lf_pages, later, total)


def _fox_sample_kernel(pt_ref, *refs, pages_per_step, heads, dh, t_new):
    pp = pages_per_step
    k_refs = refs[0:pp]
    v_refs = refs[pp:2 * pp]
    sfx_refs = refs[2 * pp:3 * pp]
    q_ref, kn_ref, vn_ref, gate_ref, cn_col_ref, cn_row_ref, o_ref, m_ref, l_ref, acc_ref, carry_ref = refs[3 * pp:]
    j = pl.program_id(1)
    nrow = t_new * heads
    pw = PAGE_SIZE * heads

    @pl.when(j == 0)
    def _():
        m_ref[...] = jnp.full_like(m_ref, -jnp.inf)
        l_ref[...] = jnp.zeros_like(l_ref)
        acc_ref[...] = jnp.zeros_like(acc_ref)
        carry_ref[...] = jnp.zeros_like(carry_ref)

    q = q_ref[0]
    cn_col = cn_col_ref[0] * LOG2E
    row_head = lax.broadcasted_iota(jnp.int32, (nrow, pw), 0) % heads
    col_head = lax.broadcasted_iota(jnp.int32, (nrow, pw), 1) % heads
    same_head = row_head == col_head

    def online(logit_list, v_list):
        m_cur = logit_list[0]
        for lg in logit_list[1:]:
            m_cur = jnp.maximum(m_cur, lg)
        m_prev = m_ref[...]
        m_new = jnp.maximum(m_prev, jnp.max(m_cur, axis=-1, keepdims=True))
        alpha = jnp.exp2(m_prev - m_new)
        probs = [jnp.exp2(lg - m_new) for lg in logit_list]
        row_sum = probs[0]
        for pj in probs[1:]:
            row_sum = row_sum + pj
        l_ref[...] = alpha * l_ref[...] + jnp.sum(row_sum, axis=-1, keepdims=True)
        pv = None
        for pj, v in zip(probs, v_list):
            t = jnp.dot(pj.astype(BF16), v.astype(BF16), preferred_element_type=F32)
            pv = t if pv is None else pv + t
        acc_ref[...] = alpha * acc_ref[...] + pv
        m_ref[...] = m_new

    logit_list = []
    carry = carry_ref[...]
    for i in range(pp):
        suffix = (carry + sfx_refs[i][0, :, 0:pw]) * LOG2E
        carry = carry + sfx_refs[i][0, :, pw:2 * pw]
        s = lax.dot_general(q, k_refs[i][0].astype(BF16), _NT, preferred_element_type=F32)
        logit_list.append(jnp.where(same_head, s + cn_col + suffix, -jnp.inf))
    carry_ref[...] = carry
    online(logit_list, [v_refs[i][0] for i in range(pp)])

    @pl.when(j == pl.num_programs(1) - 1)
    def _():
        s = lax.dot_general(q, kn_ref[0].astype(BF16), _NT, preferred_element_type=F32)
        logits = s + cn_col - cn_row_ref[0] * LOG2E
        r = lax.broadcasted_iota(jnp.int32, (nrow, nrow), 0)
        c = lax.broadcasted_iota(jnp.int32, (nrow, nrow), 1)
        keep = jnp.logical_and((r % heads) == (c % heads), (r // heads) >= (c // heads))
        online([jnp.where(keep, logits, -jnp.inf)], [vn_ref[0]])
        o_ref[0] = acc_ref[...] / l_ref[...] * _silu(gate_ref[0])


def _fox_sample(page_table, k_pages, v_pages, lf_pages, q, k_new, v_new, gate, cn_col, cn_row, *, pages_per_step):
    n_pool = lf_pages.shape[0]
    sfx = _page_suffix(lf_pages, B_HEADS).reshape(n_pool, 1, -1)
    db, n_pages = page_table.shape
    pp = pages_per_step
    nrow = q.shape[1]
    pw = PAGE_SIZE * B_HEADS
    dh = B_HEAD_DIM
    page = lambda i: (lambda b, j, pt: (pt[b, n_pages - 1 - (j * pp + i)], 0, 0))
    per_b = lambda b, j, pt: (b, 0, 0)
    in_specs = ([pl.BlockSpec((1, pw, dh), page(i)) for i in range(pp)]
                + [pl.BlockSpec((1, pw, dh), page(i)) for i in range(pp)]
                + [pl.BlockSpec((1, 1, 2 * pw), page(i)) for i in range(pp)]
                + [pl.BlockSpec((1, nrow, dh), per_b)] * 4
                + [pl.BlockSpec((1, nrow, 1), per_b), pl.BlockSpec((1, 1, nrow), per_b)])
    grid_spec = pltpu.PrefetchScalarGridSpec(
        num_scalar_prefetch=1,
        grid=(db, n_pages // pp),
        in_specs=in_specs,
        out_specs=pl.BlockSpec((1, nrow, dh), per_b),
        scratch_shapes=[pltpu.VMEM((nrow, 1), F32), pltpu.VMEM((nrow, 1), F32), pltpu.VMEM((nrow, dh), F32),
                        pltpu.VMEM((1, pw), F32)],
    )
    return pl.pallas_call(
        functools.partial(_fox_sample_kernel, pages_per_step=pp, heads=B_HEADS, dh=dh, t_new=nrow // B_HEADS),
        grid_spec=grid_spec,
        out_shape=jax.ShapeDtypeStruct((db, nrow, dh), F32),
        compiler_params=_cparams(("arbitrary", "arbitrary")),
        name="fox_sample",
    )(page_table, *([k_pages] * pp), *([v_pages] * pp), *([sfx] * pp), q, k_new, v_new, gate, cn_col, cn_row)


def _out_proj_kernel(*refs, n_in, final):
    ins = refs[:n_in]
    w_ref, x_ref = refs[n_in], refs[n_in + 1]
    y = x_ref[...]
    k0 = 0
    for a_ref in ins:
        kw = a_ref.shape[1]
        y = y + jnp.dot(a_ref[...].astype(BF16), w_ref[k0:k0 + kw, :], preferred_element_type=F32)
        k0 += kw
    if final:
        g_ref, o_ref = refs[n_in + 2], refs[n_in + 3]
        o_ref[...] = _rmsnorm_rows(y, g_ref[...])
    else:
        refs[n_in + 2][...] = y


def _out_proj(ins, w, x, final_gain, *, tm):
    m = x.shape[0]
    const = lambda i: (0, 0)
    row = lambda i: (i, 0)
    in_specs = [pl.BlockSpec((tm, a.shape[1]), row) for a in ins]
    in_specs += [pl.BlockSpec(w.shape, const), pl.BlockSpec((tm, D_MODEL), row)]
    operands = list(ins) + [w, x]
    if final_gain is not None:
        in_specs.append(pl.BlockSpec((1, D_MODEL), const))
        operands.append(final_gain)
    return pl.pallas_call(
        functools.partial(_out_proj_kernel, n_in=len(ins), final=final_gain is not None),
        grid=(m // tm,),
        in_specs=in_specs,
        out_specs=pl.BlockSpec((tm, D_MODEL), row),
        out_shape=jax.ShapeDtypeStruct((m, D_MODEL), F32),
        compiler_params=_cparams(("arbitrary",)),
        name="out_proj_final" if final_gain is not None else "out_proj",
    )(*operands)


def _pad_cols(w, n):
    return jnp.pad(w, ((0, 0), (0, n - w.shape[1])))


def _prep_weights(weights):
    (norm_even, w_in_even, b_fox_f, lb_logits, hgrn_gain, w_out_even, norm_odd, w_in_odd, w_gla_gate, b_gla_gate,
     gla_gain, w_out_odd, final_norm) = weights
    n_even = 4 * A_WIDTH + 4 * B_WIDTH
    n_odd = 2 * C_KEY_WIDTH + 2 * C_VAL_WIDTH
    return dict(
        norm_even=norm_even[0].reshape(1, D_MODEL),
        w_even=_cast_cols(w_in_even[0], 0, n_even, 512),
        w_fb=_cast_cols(w_in_even[0], n_even, B_HEADS, LANES),
        b_fox=_pad_cols(b_fox_f[0].reshape(1, B_HEADS), LANES),
        lb_logits=lb_logits,
        hgrn_gain=hgrn_gain[0].reshape(1, A_WIDTH),
        w_out_even=w_out_even[0].astype(BF16),
        norm_odd=norm_odd[0].reshape(1, D_MODEL),
        w_odd=_cast_cols(w_in_odd[0], 0, n_odd, 512),
        w_r=_cast_cols(w_in_odd[0], n_odd, C_GATE_RANK, LANES),
        w_gate=jnp.pad(w_gla_gate[0], ((0, LANES - C_GATE_RANK), (0, 0))).astype(BF16),
        b_gate=b_gla_gate[0].reshape(1, C_KEY_WIDTH),
        gla_gain=gla_gain[0].reshape(1, C_VAL_WIDTH),
        w_out_odd=w_out_odd[0].astype(BF16),
        final_norm=final_norm.reshape(1, D_MODEL),
    )


def _trunk(x, w, *, bsz, seq, prompt, sample_ctx):
    m = bsz * seq
    tm = 512 if m % 512 == 0 else m
    x2 = x.reshape(m, D_MODEL)

    pa, gate_b, qkv, k_rows, v_rows, lf, c = _even_proj(x2, w["norm_even"], w["w_even"], w["w_fb"], w["b_fox"],
                                                        seq=seq, tm=tm)
    hg = dict(heads=A_HEADS, dk=A_HEAD_DIM, dv=A_HEAD_DIM, q_scale=1.0)
    if prompt:
        o_a, s_a = _gla_prompt(True, [pa] * 4, [0, 1, 2, 3], w["lb_logits"], w["hgrn_gain"], bsz=bsz, seq=seq,
                               tb=512, **hg)
        ct = jnp.pad(c[:, :B_HEADS].reshape(bsz, seq, B_HEADS).transpose(0, 2, 1),
                     ((0, 0), (0, SUBLANES - B_HEADS), (0, 0)))
        o_b = _fox_prompt(qkv, gate_b, c, ct, bsz=bsz, seq=seq, tq=512)
    else:
        o_a, s_a = _gla_sample(True, [pa] * 4, [0, 1, 2, 3], w["lb_logits"], w["hgrn_gain"],
                               sample_ctx["state_hgrn"], t_new=seq, **hg)
        nrow = seq * B_HEADS
        rows = lambda a: a.reshape(bsz, nrow, B_HEAD_DIM)
        cn = c[:, :B_HEADS].reshape(bsz, nrow)
        o_b = _fox_sample(sample_ctx["page_table"], sample_ctx["k_pages"], sample_ctx["v_pages"],
                          sample_ctx["lf_pages"], rows(qkv[:, :B_WIDTH]), rows(k_rows), rows(v_rows), rows(gate_b),
                          cn.reshape(bsz, nrow, 1), cn.reshape(bsz, 1, nrow), pages_per_step=8)
        o_b = o_b.reshape(m, B_WIDTH)
    x1 = _out_proj([o_a, o_b], w["w_out_even"], x2, None, tm=tm)
    kb = k_rows.reshape(bsz, seq, B_HEADS, B_HEAD_DIM)
    vb = v_rows.reshape(bsz, seq, B_HEADS, B_HEAD_DIM)
    lfb = lf[:, :B_HEADS].reshape(bsz, seq, B_HEADS)

    proj1, lf1 = _odd_proj(x1, w["norm_odd"], w["w_odd"], w["w_r"], w["w_gate"], w["b_gate"], tm=tm)
    gl = dict(heads=C_HEADS, dk=C_KEY_DIM, dv=C_VAL_DIM, q_scale=C_KEY_DIM ** -0.5)
    arrays = [proj1, proj1, proj1, proj1, lf1]
    col_blocks = [0, 1, 1, 2, 0]
    if prompt:
        o_c, s_c = _gla_prompt(False, arrays, col_blocks, None, w["gla_gain"], bsz=bsz, seq=seq, tb=512, **gl)
    else:
        o_c, s_c = _gla_sample(False, arrays, col_blocks, None, w["gla_gain"], sample_ctx["state_gla"],
                               t_new=seq, **gl)
    y = _out_proj([o_c], w["w_out_odd"], x1, w["final_norm"], tm=tm)
    return y.reshape(bsz, seq, D_MODEL), kb, vb, lfb, s_a, s_c


def kernel(x_prompt, x_sample, cache_fox_k, cache_fox_v, cache_fox_logf, state_hgrn, state_gla, page_table,
           norm_even, w_in_even, b_fox_f, lb_logits, hgrn_gain, w_out_even, norm_odd, w_in_odd, w_gla_gate,
           b_gla_gate, gla_gain, w_out_odd, final_norm):
    weights = _prep_weights((norm_even, w_in_even, b_fox_f, lb_logits, hgrn_gain, w_out_even, norm_odd, w_in_odd,
                             w_gla_gate, b_gla_gate, gla_gain, w_out_odd, final_norm))
    bsz, seq, _ = x_prompt.shape
    y_p, kp, vp, lfp, hgrn_p, gla_p = _trunk(x_prompt, weights, bsz=bsz, seq=seq, prompt=True, sample_ctx=None)
    n_pp = seq // PAGE_SIZE
    fox_k_prompt = kp.reshape(1, bsz, n_pp, PAGE_SIZE, B_HEADS, B_HEAD_DIM)
    fox_v_prompt = vp.reshape(1, bsz, n_pp, PAGE_SIZE, B_HEADS, B_HEAD_DIM)
    fox_logf_prompt = lfp.reshape(1, bsz, n_pp, PAGE_SIZE, B_HEADS)

    db, t_new, _ = x_sample.shape
    n_pool = cache_fox_k.shape[1]
    pw = PAGE_SIZE * B_HEADS
    ctx = dict(
        page_table=page_table,
        k_pages=cache_fox_k[0].reshape(n_pool, pw, B_HEAD_DIM),
        v_pages=cache_fox_v[0].reshape(n_pool, pw, B_HEAD_DIM),
        lf_pages=cache_fox_logf[0].reshape(n_pool, pw),
        state_hgrn=state_hgrn[0], state_gla=state_gla[0])
    y_s, ks, vs, lfs, hgrn_s, gla_s = _trunk(x_sample, weights, bsz=db, seq=t_new, prompt=False, sample_ctx=ctx)
    return (y_p, y_s, fox_k_prompt, fox_v_prompt, fox_logf_prompt, hgrn_p[None], gla_p[None],
            ks[None], vs[None], lfs[None], hgrn_s[None], gla_s[None])
```

```python
import functools

import numpy as np
import jax
import jax.numpy as jnp
from jax import lax
from jax.experimental import pallas as pl
from jax.experimental.pallas import tpu as pltpu

F32 = jnp.float32
BF16 = jnp.bfloat16

D_MODEL = 1024
PAGE_SIZE = 128
A_HEADS = 4
A_HEAD_DIM = 128
A_WIDTH = 512
B_HEADS = 4
B_HEAD_DIM = 128
B_WIDTH = 512
C_HEADS = 4
C_KEY_WIDTH = 512
C_VAL_WIDTH = 1024
C_KEY_DIM = 128
C_VAL_DIM = 256
C_GATE_RANK = 16
GLA_GATE_NORMALIZER = 16.0
EPS = 1e-6
HGRN_LAYER = 0
LOG2E = 1.4426950408889634

LANES = 128
SUBLANES = 8
VMEM_LIMIT = 56 * 1024 * 1024
CHUNK = 64
MAX_CHUNK_LOG_DECAY = 60.0

_NT = (((1,), (1,)), ((), ()))
_TN = (((0,), (0,)), ((), ()))


def _cparams(sem):
    return pltpu.CompilerParams(dimension_semantics=sem, vmem_limit_bytes=VMEM_LIMIT)


def _sigmoid(x):
    return 1.0 / (1.0 + jnp.exp(-x))


def _log_sigmoid(x):
    return jnp.minimum(x, 0.0) - jnp.log1p(jnp.exp(-jnp.abs(x)))


def _silu(x):
    return x * _sigmoid(x)


def _rmsnorm_rows(x, g):
    return x * lax.rsqrt(jnp.mean(x * x, axis=-1, keepdims=True) + EPS) * g


def _split3(x):
    p1 = x.astype(BF16)
    r1 = x - p1.astype(F32)
    p2 = r1.astype(BF16)
    p3 = (r1 - p2.astype(F32)).astype(BF16)
    return p1, p2, p3


def _dot01(m01, x):
    acc = None
    for p in _split3(x):
        t = jnp.dot(m01, p, preferred_element_type=F32)
        acc = t if acc is None else acc + t
    return acc


def _x_dot01(x, m01):
    acc = None
    for p in _split3(x):
        t = jnp.dot(p, m01, preferred_element_type=F32)
        acc = t if acc is None else acc + t
    return acc


def _lower_tri(n, seq):
    r = lax.broadcasted_iota(jnp.int32, (n, n), 0)
    c = lax.broadcasted_iota(jnp.int32, (n, n), 1)
    keep = r >= c
    if seq < n:
        keep = jnp.logical_and(keep, (r // seq) == (c // seq))
    return jnp.where(keep, 1.0, 0.0).astype(BF16)


def _dot_wt(h, wt_ref, c0, n):
    return lax.dot_general(h, wt_ref[c0:c0 + n, :], _NT, preferred_element_type=F32)


def _proj_cols(h, wt_ref, out_ref):
    step = 512
    for c0 in range(0, wt_ref.shape[0], step):
        out_ref[:, c0:c0 + step] = _dot_wt(h, wt_ref, c0, step)


def _even_proj_kernel(x_ref, g_ref, w_ref, wfb_ref, bfox_ref, pa_ref, gate_ref, qkv_ref, kout_ref, vout_ref,
                      lf_ref, c_ref, carry_ref, *, tm, seq):
    i = pl.program_id(0)
    h = _rmsnorm_rows(x_ref[...], g_ref[...]).astype(BF16)
    na = 4 * A_WIDTH
    bw = B_WIDTH
    for c0 in range(0, na, 512):
        pa_ref[:, c0:c0 + 512] = _dot_wt(h, w_ref, c0, 512)
    q = _dot_wt(h, w_ref, na, bw)
    qkv_ref[:, 0:bw] = (q * (B_HEAD_DIM ** -0.5 * LOG2E)).astype(BF16)
    for j, out_ref in ((1, kout_ref), (2, vout_ref)):
        kv = _dot_wt(h, w_ref, na + j * bw, bw)
        qkv_ref[:, j * bw:(j + 1) * bw] = kv.astype(BF16)
        for hd in range(B_HEADS):
            out_ref[pl.ds(hd, tm, stride=B_HEADS), :] = kv[:, hd * B_HEAD_DIM:(hd + 1) * B_HEAD_DIM]
    gate_ref[...] = _dot_wt(h, w_ref, na + 3 * bw, bw)
    fb = _dot_wt(h, wfb_ref, 0, LANES) + bfox_ref[...]
    lane = lax.broadcasted_iota(jnp.int32, fb.shape, 1)
    lf = jnp.where(lane < B_HEADS, _log_sigmoid(fb), 0.0)
    lf_ref[...] = lf
    cs = _dot01(_lower_tri(tm, seq), lf)
    if seq > tm:
        @pl.when((i * tm) % seq == 0)
        def _():
            carry_ref[...] = jnp.zeros_like(carry_ref)
        cs = cs + carry_ref[0:1, :]
        c_ref[...] = cs
        carry_ref[0:1, :] = cs[tm - 1:tm, :]
    else:
        c_ref[...] = cs


def _even_proj(x, g, w_main, w_fb, b_fox, *, seq, tm):
    m = x.shape[0]
    const = lambda i: (0, 0)
    row = lambda i: (i, 0)
    return pl.pallas_call(
        functools.partial(_even_proj_kernel, tm=tm, seq=seq),
        grid=(m // tm,),
        in_specs=[pl.BlockSpec((tm, D_MODEL), row), pl.BlockSpec((1, D_MODEL), const),
                  pl.BlockSpec(w_main.shape, const), pl.BlockSpec(w_fb.shape, const),
                  pl.BlockSpec((1, LANES), const)],
        out_specs=[pl.BlockSpec((tm, 4 * A_WIDTH), row), pl.BlockSpec((tm, B_WIDTH), row),
                   pl.BlockSpec((tm, 3 * B_WIDTH), row), pl.BlockSpec((tm * B_HEADS, B_HEAD_DIM), row),
                   pl.BlockSpec((tm * B_HEADS, B_HEAD_DIM), row), pl.BlockSpec((tm, LANES), row),
                   pl.BlockSpec((tm, LANES), row)],
        out_shape=[jax.ShapeDtypeStruct((m, 4 * A_WIDTH), F32), jax.ShapeDtypeStruct((m, B_WIDTH), F32),
                   jax.ShapeDtypeStruct((m, 3 * B_WIDTH), BF16),
                   jax.ShapeDtypeStruct((m * B_HEADS, B_HEAD_DIM), F32),
                   jax.ShapeDtypeStruct((m * B_HEADS, B_HEAD_DIM), F32),
                   jax.ShapeDtypeStruct((m, LANES), F32), jax.ShapeDtypeStruct((m, LANES), F32)],
        scratch_shapes=[pltpu.VMEM((SUBLANES, LANES), F32)],
        compiler_params=_cparams(("arbitrary",)),
        name="even_proj",
    )(x, g, w_main, w_fb, b_fox)


def _odd_proj_kernel(x_ref, g_ref, w_ref, wr_ref, wg_ref, bg_ref, proj_ref, lf_ref):
    h = _rmsnorm_rows(x_ref[...], g_ref[...]).astype(BF16)
    _proj_cols(h, w_ref, proj_ref)
    r = _dot_wt(h, wr_ref, 0, LANES)
    z = jnp.dot(r.astype(BF16), wg_ref[...], preferred_element_type=F32) + bg_ref[...]
    lf_ref[...] = _log_sigmoid(z) / GLA_GATE_NORMALIZER


def _odd_proj(x, g, w_main, w_r, w_gate, b_gate, *, tm):
    m = x.shape[0]
    n = w_main.shape[0]
    const = lambda i: (0, 0)
    row = lambda i: (i, 0)
    return pl.pallas_call(
        _odd_proj_kernel,
        grid=(m // tm,),
        in_specs=[pl.BlockSpec((tm, D_MODEL), row), pl.BlockSpec((1, D_MODEL), const),
                  pl.BlockSpec(w_main.shape, const), pl.BlockSpec(w_r.shape, const),
                  pl.BlockSpec((LANES, C_KEY_WIDTH), const), pl.BlockSpec((1, C_KEY_WIDTH), const)],
        out_specs=[pl.BlockSpec((tm, n), row), pl.BlockSpec((tm, C_KEY_WIDTH), row)],
        out_shape=[jax.ShapeDtypeStruct((m, n), F32), jax.ShapeDtypeStruct((m, C_KEY_WIDTH), F32)],
        compiler_params=_cparams(("arbitrary",)),
        name="odd_proj",
    )(x, g, w_main, w_r, w_gate, b_gate)


def _lower_bound(logits, layer):
    e = jnp.exp(logits - jnp.max(logits, axis=0, keepdims=True))
    return jnp.sum(e[:layer + 1, :], axis=0, keepdims=True) / jnp.sum(e, axis=0, keepdims=True)


def _hgrn_gate(fa, lb):
    f = lb + (1.0 - lb) * _sigmoid(fa)
    return jnp.log(f), 1.0 - f


def _exact_group(q, k, v, g, st, lo, hi):
    n = SUBLANES
    row = lax.broadcasted_iota(jnp.int32, (n, 1), 0)
    valid = jnp.logical_and(row >= lo, row < hi)
    q = jnp.where(valid, q, 0.0)
    k = jnp.where(valid, k, 0.0)
    g = jnp.where(valid, g, 0.0)
    b = g
    for sh in (1, 2, 4):
        b = b + jnp.where(row >= sh, pltpu.roll(b, sh, 0), 0.0)
    o = lax.dot_general((q * jnp.exp(b)).astype(BF16), st.astype(BF16), _NT, preferred_element_type=F32)
    for s in range(lo, hi):
        w = jnp.exp(jnp.minimum(b - b[s:s + 1, :], 0.0))
        a = jnp.sum(q * k[s:s + 1, :] * w, axis=-1, keepdims=True)
        o = o + jnp.where(row >= s, a, 0.0) * v[s:s + 1, :]
    b_last = b[n - 1:n, :]
    k_hat = k * jnp.exp(b_last - b)
    st_new = st * jnp.exp(b_last) + lax.dot_general(v.astype(BF16), k_hat.astype(BF16), _TN,
                                                    preferred_element_type=F32)
    return o, st_new


def _head_norm_gate(o, gain, gate):
    y = o * lax.rsqrt(jnp.mean(o * o, axis=-1, keepdims=True) + EPS) * gain
    return y * _silu(gate)


def _gla_prompt_kernel(*refs, hgrn, heads, dk, dv, tb, q_scale):
    if hgrn:
        q_ref, fa_ref, v_ref, gate_ref, lb_ref, gain_ref, o_ref, s_ref, st_ref, b_ref, g_ref, k_ref = refs
    else:
        q_ref, k_ref, v_ref, gate_ref, g_ref, gain_ref, o_ref, s_ref, st_ref, b_ref = refs
    t = pl.program_id(1)
    n_chunks = tb // CHUNK
    if hgrn:
        lb = _lower_bound(lb_ref[...], HGRN_LAYER)

    @pl.when(t == 0)
    def _():
        st_ref[...] = jnp.zeros_like(st_ref)

    tri = _lower_tri(CHUNK, CHUNK)
    b_min = None
    for c in range(n_chunks):
        rows = slice(c * CHUNK, (c + 1) * CHUNK)
        if hgrn:
            g, k = _hgrn_gate(fa_ref[rows, :], lb)
            g_ref[rows, :] = g
            k_ref[rows, :] = k
        else:
            g = g_ref[rows, :]
        b = _dot01(tri, g)
        b_ref[rows, :] = b
        b_last = b[CHUNK - 1:CHUNK, :]
        b_min = b_last if b_min is None else jnp.minimum(b_min, b_last)
    chunk_form_ok = jnp.min(b_min) >= -MAX_CHUNK_LOG_DECAY

    def finish(rows, h, o_h):
        vsl = slice(h * dv, (h + 1) * dv)
        o_ref[rows, vsl] = _head_norm_gate(o_h, gain_ref[:, vsl], gate_ref[rows, vsl])

    @pl.when(chunk_form_ok)
    def _():
        r_i = lax.broadcasted_iota(jnp.int32, (CHUNK, CHUNK), 0)
        c_i = lax.broadcasted_iota(jnp.int32, (CHUNK, CHUNK), 1)
        causal = r_i >= c_i

        for c in range(n_chunks):
            rows = slice(c * CHUNK, (c + 1) * CHUNK)
            b = b_ref[rows, :]
            e_b = jnp.exp(b)
            b_last = b[CHUNK - 1:CHUNK, :]
            e_last = jnp.exp(b_last)
            q_t = q_ref[rows, :] * q_scale * e_b
            k_t = k_ref[rows, :] * jnp.exp(-b)
            k_hat = k_t * e_last
            for h in range(heads):
                ksl = slice(h * dk, (h + 1) * dk)
                vsl = slice(h * dv, (h + 1) * dv)
                qh = q_t[:, ksl].astype(BF16)
                vh = v_ref[rows, vsl].astype(BF16)
                a = lax.dot_general(qh, k_t[:, ksl].astype(BF16), _NT, preferred_element_type=F32)
                a = jnp.where(causal, a, 0.0).astype(BF16)
                st = st_ref[h]
                o_h = jnp.dot(a, vh, preferred_element_type=F32) + lax.dot_general(
                    qh, st.astype(BF16), _NT, preferred_element_type=F32)
                st_ref[h] = st * e_last[:, ksl] + lax.dot_general(
                    vh, k_hat[:, ksl].astype(BF16), _TN, preferred_element_type=F32)
                finish(rows, h, o_h)

    @pl.when(jnp.logical_not(chunk_form_ok))
    def _():
        def group(i, carry):
            rows = pl.ds(pl.multiple_of(i * SUBLANES, SUBLANES), SUBLANES)
            for h in range(heads):
                ksl = slice(h * dk, (h + 1) * dk)
                vsl = slice(h * dv, (h + 1) * dv)
                o_h, st_new = _exact_group(q_ref[rows, ksl] * q_scale, k_ref[rows, ksl], v_ref[rows, vsl],
                                           g_ref[rows, ksl], st_ref[h], 0, SUBLANES)
                st_ref[h] = st_new
                finish(rows, h, o_h)
            return carry

        lax.fori_loop(0, tb // SUBLANES, group, 0)

    @pl.when(t == pl.num_programs(1) - 1)
    def _():
        for h in range(heads):
            s_ref[0, h] = st_ref[h].T


def _gla_prompt(hgrn, arrays, col_blocks, small, gain, *, bsz, seq, heads, dk, dv, tb, q_scale):
    nt = seq // tb
    kw, vw = heads * dk, heads * dv
    widths = [kw, kw, vw, vw] + ([] if hgrn else [kw])
    in_specs = [pl.BlockSpec((tb, w), functools.partial(lambda b, t, cb: (b * nt + t, cb), cb=cb))
                for w, cb in zip(widths, col_blocks)]
    operands = list(arrays)
    if hgrn:
        in_specs.append(pl.BlockSpec(small.shape, lambda b, t: (0, 0)))
        operands.append(small)
    in_specs.append(pl.BlockSpec((1, vw), lambda b, t: (0, 0)))
    operands.append(gain)
    scratch = [pltpu.VMEM((heads, dv, dk), F32), pltpu.VMEM((tb, kw), F32)]
    if hgrn:
        scratch += [pltpu.VMEM((tb, kw), F32), pltpu.VMEM((tb, kw), F32)]
    return pl.pallas_call(
        functools.partial(_gla_prompt_kernel, hgrn=hgrn, heads=heads, dk=dk, dv=dv, tb=tb, q_scale=q_scale),
        grid=(bsz, nt),
        in_specs=in_specs,
        out_specs=[pl.BlockSpec((tb, vw), lambda b, t: (b * nt + t, 0)),
                   pl.BlockSpec((1, heads, dk, dv), lambda b, t: (b, 0, 0, 0))],
        out_shape=[jax.ShapeDtypeStruct((bsz * seq, vw), F32), jax.ShapeDtypeStruct((bsz, heads, dk, dv), F32)],
        scratch_shapes=scratch,
        compiler_params=_cparams(("arbitrary", "arbitrary")),
        name="hgrn_prompt" if hgrn else "gla_prompt",
    )(*operands)


def _gla_sample_kernel(*refs, hgrn, heads, dk, dv, t_new, q_scale):
    if hgrn:
        q_ref, fa_ref, v_ref, gate_ref, lb_ref, gain_ref, s0_ref, o_ref, s_ref = refs
    else:
        q_ref, k_ref, v_ref, gate_ref, g_ref, gain_ref, s0_ref, o_ref, s_ref = refs
    for h in range(heads):
        ksl = slice(h * dk, (h + 1) * dk)
        vsl = slice(h * dv, (h + 1) * dv)
        if hgrn:
            g, k = _hgrn_gate(fa_ref[:, ksl], _lower_bound(lb_ref[...], HGRN_LAYER)[:, ksl])
        else:
            g, k = g_ref[:, ksl], k_ref[:, ksl]
        q = q_ref[:, ksl] * q_scale
        v = v_ref[:, vsl]
        o_h = None
        for e in range(SUBLANES // t_new):
            o_e, st_new = _exact_group(q, k, v, g, s0_ref[e, h].T, e * t_new, (e + 1) * t_new)
            s_ref[e, h] = st_new.T
            o_h = o_e if o_h is None else o_h + o_e
        o_ref[:, vsl] = _head_norm_gate(o_h, gain_ref[:, vsl], gate_ref[:, vsl])


def _gla_sample(hgrn, arrays, col_blocks, small, gain, s0, *, t_new, heads, dk, dv, q_scale):
    m = arrays[0].shape[0]
    per = SUBLANES // t_new
    kw, vw = heads * dk, heads * dv
    widths = [kw, kw, vw, vw] + ([] if hgrn else [kw])
    in_specs = [pl.BlockSpec((SUBLANES, w), functools.partial(lambda i, cb: (i, cb), cb=cb))
                for w, cb in zip(widths, col_blocks)]
    operands = list(arrays)
    if hgrn:
        in_specs.append(pl.BlockSpec(small.shape, lambda i: (0, 0)))
        operands.append(small)
    in_specs.append(pl.BlockSpec((1, vw), lambda i: (0, 0)))
    operands.append(gain)
    in_specs.append(pl.BlockSpec((per, heads, dk, dv), lambda i: (i, 0, 0, 0)))
    operands.append(s0)
    return pl.pallas_call(
        functools.partial(_gla_sample_kernel, hgrn=hgrn, heads=heads, dk=dk, dv=dv, t_new=t_new, q_scale=q_scale),
        grid=(m // SUBLANES,),
        in_specs=in_specs,
        out_specs=[pl.BlockSpec((SUBLANES, vw), lambda i: (i, 0)),
                   pl.BlockSpec((per, heads, dk, dv), lambda i: (i, 0, 0, 0))],
        out_shape=[jax.ShapeDtypeStruct((m, vw), F32), jax.ShapeDtypeStruct(s0.shape, F32)],
        compiler_params=_cparams(("arbitrary",)),
        name="hgrn_sample" if hgrn else "gla_sample",
    )(*operands)


def _fox_prompt_kernel(qi_ref, ki_ref, q_ref, k_ref, v_ref, gate_ref, c_ref, ct_ref, o_ref, m_ref, l_ref, acc_ref,
                       cq_ref, *, tq, heads, dh):
    p = pl.program_id(1)
    qi = qi_ref[p]
    ki = ki_ref[p]
    ncb = tq // LANES

    @pl.when(ki == 0)
    def _():
        m_ref[...] = jnp.full_like(m_ref, -jnp.inf)
        l_ref[...] = jnp.zeros_like(l_ref)
        acc_ref[...] = jnp.zeros_like(acc_ref)
        for h in range(heads):
            cq_ref[h] = jnp.broadcast_to(c_ref[:, h:h + 1] * LOG2E, (tq, LANES))

    def step(diag):
        if diag:
            r_i = lax.broadcasted_iota(jnp.int32, (tq, LANES), 0)
            c_i = lax.broadcasted_iota(jnp.int32, (tq, LANES), 1)
        ck_all = ct_ref[0] * LOG2E
        for h in range(heads):
            sl = slice(h * dh, (h + 1) * dh)
            s = lax.dot_general(q_ref[:, sl], k_ref[:, sl], _NT, preferred_element_type=F32)
            cq = cq_ref[h]
            blocks = []
            m_cur = None
            for j in range(ncb):
                cs = slice(j * LANES, (j + 1) * LANES)
                lg = s[:, cs] + (cq - ck_all[h:h + 1, cs])
                if diag:
                    lg = jnp.where(r_i >= c_i + j * LANES, lg, -jnp.inf)
                blocks.append(lg)
                m_cur = lg if m_cur is None else jnp.maximum(m_cur, lg)
            m_prev = m_ref[h]
            m_new = jnp.maximum(m_prev, jnp.max(m_cur, axis=-1, keepdims=True))
            alpha = jnp.exp2(m_prev - m_new)
            probs = [jnp.exp2(lg - m_new) for lg in blocks]
            row_sum = probs[0]
            for pj in probs[1:]:
                row_sum = row_sum + pj
            l_ref[h] = alpha * l_ref[h] + jnp.sum(row_sum, axis=-1, keepdims=True)
            pr = jnp.concatenate([pj.astype(BF16) for pj in probs], axis=1)
            acc_ref[:, sl] = alpha * acc_ref[:, sl] + jnp.dot(pr, v_ref[:, sl], preferred_element_type=F32)
            m_ref[h] = m_new

    @pl.when(ki < qi)
    def _():
        step(False)

    @pl.when(ki == qi)
    def _():
        step(True)
        for h in range(heads):
            sl = slice(h * dh, (h + 1) * dh)
            o_ref[:, sl] = acc_ref[:, sl] / l_ref[h] * _silu(gate_ref[:, sl])


def _fox_prompt(qkv, gate, c, ct, *, bsz, seq, tq):
    nq = seq // tq
    pairs = [(qi, ki) for qi in range(nq) for ki in range(qi + 1)]
    qi_tab = jnp.asarray(np.array([p[0] for p in pairs], np.int32))
    ki_tab = jnp.asarray(np.array([p[1] for p in pairs], np.int32))
    w = B_WIDTH
    q_map = lambda cb: (lambda b, p, qt, kt: (b * nq + qt[p], cb))
    k_map = lambda cb: (lambda b, p, qt, kt: (b * nq + kt[p], cb))
    grid_spec = pltpu.PrefetchScalarGridSpec(
        num_scalar_prefetch=2,
        grid=(bsz, len(pairs)),
        in_specs=[pl.BlockSpec((tq, w), q_map(0)), pl.BlockSpec((tq, w), k_map(1)),
                  pl.BlockSpec((tq, w), k_map(2)), pl.BlockSpec((tq, w), q_map(0)),
                  pl.BlockSpec((tq, LANES), q_map(0)),
                  pl.BlockSpec((1, SUBLANES, tq), lambda b, p, qt, kt: (b, 0, kt[p]))],
        out_specs=pl.BlockSpec((tq, w), q_map(0)),
        scratch_shapes=[pltpu.VMEM((B_HEADS, tq, LANES), F32), pltpu.VMEM((B_HEADS, tq, LANES), F32),
                        pltpu.VMEM((tq, w), F32), pltpu.VMEM((B_HEADS, tq, LANES), F32)],
    )
    return pl.pallas_call(
        functools.partial(_fox_prompt_kernel, tq=tq, heads=B_HEADS, dh=B_HEAD_DIM),
        grid_spec=grid_spec,
        out_shape=jax.ShapeDtypeStruct((bsz * seq, w), F32),
        compiler_params=_cparams(("arbitrary", "arbitrary")),
        name="fox_prompt",
    )(qi_tab, ki_tab, qkv, qkv, qkv, gate, c, ct)


def _page_suffix_kernel(lf_ref, later_ref, total_ref, out_ref):
    lf = lf_ref[...]
    pw = lf.shape[1]
    out_ref[:, 0:pw] = _x_dot01(lf, later_ref[...])
    out_ref[:, pw:2 * pw] = _x_dot01(lf, total_ref[...])


def _page_suffix(lf_pages, heads):
    n_pool, pw = lf_pages.shape
    tm = 512 if n_pool % 512 == 0 else n_pool
    j = np.arange(pw)
    in_head, in_tok = j[:, None] // (pw // heads), j[:, None] % (pw // heads)
    out_head, out_tok = j[None, :] % heads, j[None, :] // heads
    head_eq = in_head == out_head
    later = jnp.asarray(head_eq & (in_tok > out_tok), BF16)
    total = jnp.asarray(head_eq, BF16)
    row = lambda i: (i, 0)
    const = lambda i: (0, 0)
    return pl.pallas_call(
        _page_suffix_kernel,
        grid=(n_pool // tm,),
        in_specs=[pl.BlockSpec((tm, pw), row), pl.BlockSpec((pw, pw), const), pl.BlockSpec((pw, pw), const)],
        out_specs=pl.BlockSpec((tm, 2 * pw), row),
        out_shape=jax.ShapeDtypeStruct((n_pool, 2 * pw), F32),
        compiler_params=_cparams(("arbitrary",)),
        name="page_suffix",
    )(lf_pages, later, total)


def _fox_sample_kernel(pt_ref, *refs, pages_per_step, heads, dh, t_new):
    pp = pages_per_step
    k_refs = refs[0:pp]
    v_refs = refs[pp:2 * pp]
    sfx_refs = refs[2 * pp:3 * pp]
    q_ref, kn_ref, vn_ref, gate_ref, cn_col_ref, cn_row_ref, o_ref, m_ref, l_ref, acc_ref, carry_ref = refs[3 * pp:]
    j = pl.program_id(1)
    nrow = t_new * heads
    pw = PAGE_SIZE * heads

    @pl.when(j == 0)
    def _():
        m_ref[...] = jnp.full_like(m_ref, -jnp.inf)
        l_ref[...] = jnp.zeros_like(l_ref)
        acc_ref[...] = jnp.zeros_like(acc_ref)
        carry_ref[...] = jnp.zeros_like(carry_ref)

    q = q_ref[0]
    cn_col = cn_col_ref[0] * LOG2E
    row_head = lax.broadcasted_iota(jnp.int32, (nrow, pw), 0) % heads
    col_head = lax.broadcasted_iota(jnp.int32, (nrow, pw), 1) % heads
    same_head = row_head == col_head

    def online(logit_list, v_list):
        m_cur = logit_list[0]
        for lg in logit_list[1:]:
            m_cur = jnp.maximum(m_cur, lg)
        m_prev = m_ref[...]
        m_new = jnp.maximum(m_prev, jnp.max(m_cur, axis=-1, keepdims=True))
        alpha = jnp.exp2(m_prev - m_new)
        probs = [jnp.exp2(lg - m_new) for lg in logit_list]
        row_sum = probs[0]
        for pj in probs[1:]:
            row_sum = row_sum + pj
        l_ref[...] = alpha * l_ref[...] + jnp.sum(row_sum, axis=-1, keepdims=True)
        pv = None
        for pj, v in zip(probs, v_list):
            t = jnp.dot(pj.astype(BF16), v.astype(BF16), preferred_element_type=F32)
            pv = t if pv is None else pv + t
        acc_ref[...] = alpha * acc_ref[...] + pv
        m_ref[...] = m_new

    logit_list = []
    carry = carry_ref[...]
    for i in range(pp):
        suffix = (carry + sfx_refs[i][0, :, 0:pw]) * LOG2E
        carry = carry + sfx_refs[i][0, :, pw:2 * pw]
        s = lax.dot_general(q, k_refs[i][0].astype(BF16), _NT, preferred_element_type=F32)
        logit_list.append(jnp.where(same_head, s + cn_col + suffix, -jnp.inf))
    carry_ref[...] = carry
    online(logit_list, [v_refs[i][0] for i in range(pp)])

    @pl.when(j == pl.num_programs(1) - 1)
    def _():
        s = lax.dot_general(q, kn_ref[0].astype(BF16), _NT, preferred_element_type=F32)
        logits = s + cn_col - cn_row_ref[0] * LOG2E
        r = lax.broadcasted_iota(jnp.int32, (nrow, nrow), 0)
        c = lax.broadcasted_iota(jnp.int32, (nrow, nrow), 1)
        keep = jnp.logical_and((r % heads) == (c % heads), (r // heads) >= (c // heads))
        online([jnp.where(keep, logits, -jnp.inf)], [vn_ref[0]])
        o_ref[0] = acc_ref[...] / l_ref[...] * _silu(gate_ref[0])


def _fox_sample(page_table, k_pages, v_pages, lf_pages, q, k_new, v_new, gate, cn_col, cn_row, *, pages_per_step):
    n_pool = lf_pages.shape[0]
    sfx = _page_suffix(lf_pages, B_HEADS).reshape(n_pool, 1, -1)
    db, n_pages = page_table.shape
    pp = pages_per_step
    nrow = q.shape[1]
    pw = PAGE_SIZE * B_HEADS
    dh = B_HEAD_DIM
    page = lambda i: (lambda b, j, pt: (pt[b, n_pages - 1 - (j * pp + i)], 0, 0))
    per_b = lambda b, j, pt: (b, 0, 0)
    in_specs = ([pl.BlockSpec((1, pw, dh), page(i)) for i in range(pp)]
                + [pl.BlockSpec((1, pw, dh), page(i)) for i in range(pp)]
                + [pl.BlockSpec((1, 1, 2 * pw), page(i)) for i in range(pp)]
                + [pl.BlockSpec((1, nrow, dh), per_b)] * 4
                + [pl.BlockSpec((1, nrow, 1), per_b), pl.BlockSpec((1, 1, nrow), per_b)])
    grid_spec = pltpu.PrefetchScalarGridSpec(
        num_scalar_prefetch=1,
        grid=(db, n_pages // pp),
        in_specs=in_specs,
        out_specs=pl.BlockSpec((1, nrow, dh), per_b),
        scratch_shapes=[pltpu.VMEM((nrow, 1), F32), pltpu.VMEM((nrow, 1), F32), pltpu.VMEM((nrow, dh), F32),
                        pltpu.VMEM((1, pw), F32)],
    )
    return pl.pallas_call(
        functools.partial(_fox_sample_kernel, pages_per_step=pp, heads=B_HEADS, dh=dh, t_new=nrow // B_HEADS),
        grid_spec=grid_spec,
        out_shape=jax.ShapeDtypeStruct((db, nrow, dh), F32),
        compiler_params=_cparams(("arbitrary", "arbitrary")),
        name="fox_sample",
    )(page_table, *([k_pages] * pp), *([v_pages] * pp), *([sfx] * pp), q, k_new, v_new, gate, cn_col, cn_row)


def _out_proj_kernel(*refs, n_in, final):
    ins = refs[:n_in]
    w_ref, x_ref = refs[n_in], refs[n_in + 1]
    y = x_ref[...]
    k0 = 0
    for a_ref in ins:
        kw = a_ref.shape[1]
        y = y + jnp.dot(a_ref[...].astype(BF16), w_ref[k0:k0 + kw, :], preferred_element_type=F32)
        k0 += kw
    if final:
        g_ref, o_ref = refs[n_in + 2], refs[n_in + 3]
        o_ref[...] = _rmsnorm_rows(y, g_ref[...])
    else:
        refs[n_in + 2][...] = y


def _out_proj(ins, w, x, final_gain, *, tm):
    m = x.shape[0]
    const = lambda i: (0, 0)
    row = lambda i: (i, 0)
    in_specs = [pl.BlockSpec((tm, a.shape[1]), row) for a in ins]
    in_specs += [pl.BlockSpec(w.shape, const), pl.BlockSpec((tm, D_MODEL), row)]
    operands = list(ins) + [w, x]
    if final_gain is not None:
        in_specs.append(pl.BlockSpec((1, D_MODEL), const))
        operands.append(final_gain)
    return pl.pallas_call(
        functools.partial(_out_proj_kernel, n_in=len(ins), final=final_gain is not None),
        grid=(m // tm,),
        in_specs=in_specs,
        out_specs=pl.BlockSpec((tm, D_MODEL), row),
        out_shape=jax.ShapeDtypeStruct((m, D_MODEL), F32),
        compiler_params=_cparams(("arbitrary",)),
        name="out_proj_final" if final_gain is not None else "out_proj",
    )(*operands)


def _pad_cols(w, n):
    return jnp.pad(w, ((0, 0), (0, n - w.shape[1])))


def _prep_weights(weights):
    (norm_even, w_in_even, b_fox_f, lb_logits, hgrn_gain, w_out_even, norm_odd, w_in_odd, w_gla_gate, b_gla_gate,
     gla_gain, w_out_odd, final_norm) = weights
    n_even = 4 * A_WIDTH + 4 * B_WIDTH
    n_odd = 2 * C_KEY_WIDTH + 2 * C_VAL_WIDTH
    w_even_t = w_in_even[0].T.astype(BF16)
    w_odd_t = w_in_odd[0].T.astype(BF16)
    pad_rows = lambda a: jnp.pad(a, ((0, LANES - a.shape[0]), (0, 0)))
    return dict(
        norm_even=norm_even[0].reshape(1, D_MODEL),
        w_even=w_even_t[:n_even],
        w_fb=pad_rows(w_even_t[n_even:]),
        b_fox=_pad_cols(b_fox_f[0].reshape(1, B_HEADS), LANES),
        lb_logits=lb_logits,
        hgrn_gain=hgrn_gain[0].reshape(1, A_WIDTH),
        w_out_even=w_out_even[0].astype(BF16),
        norm_odd=norm_odd[0].reshape(1, D_MODEL),
        w_odd=w_odd_t[:n_odd],
        w_r=pad_rows(w_odd_t[n_odd:]),
        w_gate=jnp.pad(w_gla_gate[0], ((0, LANES - C_GATE_RANK), (0, 0))).astype(BF16),
        b_gate=b_gla_gate[0].reshape(1, C_KEY_WIDTH),
        gla_gain=gla_gain[0].reshape(1, C_VAL_WIDTH),
        w_out_odd=w_out_odd[0].astype(BF16),
        final_norm=final_norm.reshape(1, D_MODEL),
    )


def _trunk(x, w, *, bsz, seq, prompt, sample_ctx):
    m = bsz * seq
    tm = 512 if m % 512 == 0 else m
    x2 = x.reshape(m, D_MODEL)

    pa, gate_b, qkv, k_rows, v_rows, lf, c = _even_proj(x2, w["norm_even"], w["w_even"], w["w_fb"], w["b_fox"],
                                                        seq=seq, tm=tm)
    hg = dict(heads=A_HEADS, dk=A_HEAD_DIM, dv=A_HEAD_DIM, q_scale=1.0)
    if prompt:
        o_a, s_a = _gla_prompt(True, [pa] * 4, [0, 1, 2, 3], w["lb_logits"], w["hgrn_gain"], bsz=bsz, seq=seq,
                               tb=512, **hg)
        ct = jnp.pad(c[:, :B_HEADS].reshape(bsz, seq, B_HEADS).transpose(0, 2, 1),
                     ((0, 0), (0, SUBLANES - B_HEADS), (0, 0)))
        o_b = _fox_prompt(qkv, gate_b, c, ct, bsz=bsz, seq=seq, tq=512)
    else:
        o_a, s_a = _gla_sample(True, [pa] * 4, [0, 1, 2, 3], w["lb_logits"], w["hgrn_gain"],
                               sample_ctx["state_hgrn"], t_new=seq, **hg)
        nrow = seq * B_HEADS
        rows = lambda a: a.reshape(bsz, nrow, B_HEAD_DIM)
        cn = c[:, :B_HEADS].reshape(bsz, nrow)
        o_b = _fox_sample(sample_ctx["page_table"], sample_ctx["k_pages"], sample_ctx["v_pages"],
                          sample_ctx["lf_pages"], rows(qkv[:, :B_WIDTH]), rows(k_rows), rows(v_rows), rows(gate_b),
                          cn.reshape(bsz, nrow, 1), cn.reshape(bsz, 1, nrow), pages_per_step=8)
        o_b = o_b.reshape(m, B_WIDTH)
    x1 = _out_proj([o_a, o_b], w["w_out_even"], x2, None, tm=tm)
    kb = k_rows.reshape(bsz, seq, B_HEADS, B_HEAD_DIM)
    vb = v_rows.reshape(bsz, seq, B_HEADS, B_HEAD_DIM)
    lfb = lf[:, :B_HEADS].reshape(bsz, seq, B_HEADS)

    proj1, lf1 = _odd_proj(x1, w["norm_odd"], w["w_odd"], w["w_r"], w["w_gate"], w["b_gate"], tm=tm)
    gl = dict(heads=C_HEADS, dk=C_KEY_DIM, dv=C_VAL_DIM, q_scale=C_KEY_DIM ** -0.5)
    arrays = [proj1, proj1, proj1, proj1, lf1]
    col_blocks = [0, 1, 1, 2, 0]
    if prompt:
        o_c, s_c = _gla_prompt(False, arrays, col_blocks, None, w["gla_gain"], bsz=bsz, seq=seq, tb=512, **gl)
    else:
        o_c, s_c = _gla_sample(False, arrays, col_blocks, None, w["gla_gain"], sample_ctx["state_gla"],
                               t_new=seq, **gl)
    y = _out_proj([o_c], w["w_out_odd"], x1, w["final_norm"], tm=tm)
    return y.reshape(bsz, seq, D_MODEL), kb, vb, lfb, s_a, s_c


def kernel(x_prompt, x_sample, cache_fox_k, cache_fox_v, cache_fox_logf, state_hgrn, state_gla, page_table,
           norm_even, w_in_even, b_fox_f, lb_logits, hgrn_gain, w_out_even, norm_odd, w_in_odd, w_gla_gate,
           b_gla_gate, gla_gain, w_out_odd, final_norm):
    weights = _prep_weights((norm_even, w_in_even, b_fox_f, lb_logits, hgrn_gain, w_out_even, norm_odd, w_in_odd,
                             w_gla_gate, b_gla_gate, gla_gain, w_out_odd, final_norm))
    bsz, seq, _ = x_prompt.shape
    y_p, kp, vp, lfp, hgrn_p, gla_p = _trunk(x_prompt, weights, bsz=bsz, seq=seq, prompt=True, sample_ctx=None)
    n_pp = seq // PAGE_SIZE
    fox_k_prompt = kp.reshape(1, bsz, n_pp, PAGE_SIZE, B_HEADS, B_HEAD_DIM)
    fox_v_prompt = vp.reshape(1, bsz, n_pp, PAGE_SIZE, B_HEADS, B_HEAD_DIM)
    fox_logf_prompt = lfp.reshape(1, bsz, n_pp, PAGE_SIZE, B_HEADS)

    db, t_new, _ = x_sample.shape
    n_pool = cache_fox_k.shape[1]
    pw = PAGE_SIZE * B_HEADS
    ctx = dict(
        page_table=page_table,
        k_pages=cache_fox_k[0].reshape(n_pool, pw, B_HEAD_DIM),
        v_pages=cache_fox_v[0].reshape(n_pool, pw, B_HEAD_DIM),
        lf_pages=cache_fox_logf[0].transpose(0, 2, 1).reshape(n_pool, pw),
        state_hgrn=state_hgrn[0], state_gla=state_gla[0])
    y_s, ks, vs, lfs, hgrn_s, gla_s = _trunk(x_sample, weights, bsz=db, seq=t_new, prompt=False, sample_ctx=ctx)
    return (y_p, y_s, fox_k_prompt, fox_v_prompt, fox_logf_prompt, hgrn_p[None], gla_p[None],
            ks[None], vs[None], lfs[None], hgrn_s[None], gla_s[None])
```

```python
import functools

import numpy as np
import jax
import jax.numpy as jnp
from jax import lax
from jax.experimental import pallas as pl
from jax.experimental.pallas import tpu as pltpu

F32 = jnp.float32
BF16 = jnp.bfloat16

D_MODEL = 1024
PAGE_SIZE = 128
A_HEADS = 4
A_HEAD_DIM = 128
A_WIDTH = 512
B_HEADS = 4
B_HEAD_DIM = 128
B_WIDTH = 512
C_HEADS = 4
C_KEY_WIDTH = 512
C_VAL_WIDTH = 1024
C_KEY_DIM = 128
C_VAL_DIM = 256
C_GATE_RANK = 16
GLA_GATE_NORMALIZER = 16.0
EPS = 1e-6
HGRN_LAYER = 0
LOG2E = 1.4426950408889634
AUG_PIECES = 3

LANES = 128
SUBLANES = 8
VMEM_LIMIT = 56 * 1024 * 1024
CHUNK = 64
MAX_CHUNK_LOG_DECAY = 60.0

_NT = (((1,), (1,)), ((), ()))
_TN = (((0,), (0,)), ((), ()))


def _cparams(sem):
    return pltpu.CompilerParams(dimension_semantics=sem, vmem_limit_bytes=VMEM_LIMIT)


def _sigmoid(x):
    return 1.0 / (1.0 + jnp.exp(-x))


def _log_sigmoid(x):
    return jnp.minimum(x, 0.0) - jnp.log1p(jnp.exp(-jnp.abs(x)))


def _silu(x):
    return x * _sigmoid(x)


def _rmsnorm_rows(x, g):
    return x * lax.rsqrt(jnp.mean(x * x, axis=-1, keepdims=True) + EPS) * g


def _split3(x):
    p1 = x.astype(BF16)
    r1 = x - p1.astype(F32)
    p2 = r1.astype(BF16)
    p3 = (r1 - p2.astype(F32)).astype(BF16)
    return p1, p2, p3


def _dot01(m01, x):
    acc = None
    for p in _split3(x):
        t = jnp.dot(m01, p, preferred_element_type=F32)
        acc = t if acc is None else acc + t
    return acc


def _x_dot01(x, m01):
    acc = None
    for p in _split3(x):
        t = jnp.dot(p, m01, preferred_element_type=F32)
        acc = t if acc is None else acc + t
    return acc


def _lower_tri(n, seq):
    r = lax.broadcasted_iota(jnp.int32, (n, n), 0)
    c = lax.broadcasted_iota(jnp.int32, (n, n), 1)
    keep = r >= c
    if seq < n:
        keep = jnp.logical_and(keep, (r // seq) == (c // seq))
    return jnp.where(keep, 1.0, 0.0).astype(BF16)


def _dot_wt(h, wt_ref, c0, n):
    return lax.dot_general(h, wt_ref[c0:c0 + n, :], _NT, preferred_element_type=F32)


def _proj_cols(h, wt_ref, out_ref):
    step = 512
    for c0 in range(0, wt_ref.shape[0], step):
        out_ref[:, c0:c0 + step] = _dot_wt(h, wt_ref, c0, step)


def _aug_selectors():
    sel_q = np.zeros((LANES, B_WIDTH), np.float32)
    sel_k = np.zeros((LANES, B_WIDTH), np.float32)
    for p in range(AUG_PIECES):
        for hd in range(B_HEADS):
            sel_q[p * B_HEADS + hd, hd * B_HEAD_DIM + p] = 1.0
            sel_k[p * B_HEADS + hd, hd * B_HEAD_DIM + AUG_PIECES + p] = -1.0
    return jnp.asarray(sel_q, BF16), jnp.asarray(sel_k, BF16)


def _even_proj_kernel(x_ref, g_ref, w_ref, wfb_ref, bfox_ref, selq_ref, selk_ref, pa_ref, gate_ref, qa_ref, ka_ref,
                      vb_ref, kout_ref, vout_ref, lf_ref, c_ref, carry_ref, *, tm, seq):
    i = pl.program_id(0)
    h = _rmsnorm_rows(x_ref[...], g_ref[...]).astype(BF16)
    na = 4 * A_WIDTH
    bw = B_WIDTH
    dh = B_HEAD_DIM
    for c0 in range(0, na, 512):
        pa_ref[:, c0:c0 + 512] = _dot_wt(h, w_ref, c0, 512)
    q = (_dot_wt(h, w_ref, na, bw) * (dh ** -0.5 * LOG2E)).astype(BF16)
    for hd in range(B_HEADS):
        qa_ref[:, 2 * hd * dh:(2 * hd + 1) * dh] = q[:, hd * dh:(hd + 1) * dh]
    for j, out_ref in ((1, kout_ref), (2, vout_ref)):
        kv = _dot_wt(h, w_ref, na + j * bw, bw)
        kv16 = kv.astype(BF16)
        if j == 1:
            for hd in range(B_HEADS):
                ka_ref[:, 2 * hd * dh:(2 * hd + 1) * dh] = kv16[:, hd * dh:(hd + 1) * dh]
        else:
            vb_ref[...] = kv16
        for hd in range(B_HEADS):
            out_ref[pl.ds(hd, tm, stride=B_HEADS), :] = kv[:, hd * dh:(hd + 1) * dh]
    gate_ref[...] = _dot_wt(h, w_ref, na + 3 * bw, bw)
    fb = _dot_wt(h, wfb_ref, 0, LANES) + bfox_ref[...]
    lane = lax.broadcasted_iota(jnp.int32, fb.shape, 1)
    lf = jnp.where(lane < B_HEADS, _log_sigmoid(fb), 0.0)
    lf_ref[...] = lf
    cs = _dot01(_lower_tri(tm, seq), lf)
    if seq > tm:
        @pl.when((i * tm) % seq == 0)
        def _():
            carry_ref[...] = jnp.zeros_like(carry_ref)
        cs = cs + carry_ref[0:1, :]
        carry_ref[0:1, :] = cs[tm - 1:tm, :]
    c_ref[...] = cs
    pos = lax.broadcasted_iota(jnp.int32, (1, bw), 1) % dh
    aug_q = jnp.where(jnp.logical_and(pos >= AUG_PIECES, pos < 2 * AUG_PIECES), 1.0, 0.0)
    aug_k = jnp.where(pos < AUG_PIECES, 1.0, 0.0)
    packed = None
    for p, piece in enumerate(_split3(cs * LOG2E)):
        shifted = piece.astype(F32) if p == 0 else pltpu.roll(piece.astype(F32), p * B_HEADS, 1)
        packed = shifted if packed is None else packed + shifted
    packed = packed.astype(BF16)
    aug_q = (aug_q + jnp.dot(packed, selq_ref[...], preferred_element_type=F32)).astype(BF16)
    aug_k = (aug_k + jnp.dot(packed, selk_ref[...], preferred_element_type=F32)).astype(BF16)
    for hd in range(B_HEADS):
        qa_ref[:, (2 * hd + 1) * dh:(2 * hd + 2) * dh] = aug_q[:, hd * dh:(hd + 1) * dh]
        ka_ref[:, (2 * hd + 1) * dh:(2 * hd + 2) * dh] = aug_k[:, hd * dh:(hd + 1) * dh]


def _even_proj(x, g, w_main, w_fb, b_fox, *, seq, tm):
    m = x.shape[0]
    const = lambda i: (0, 0)
    row = lambda i: (i, 0)
    sel_q, sel_k = _aug_selectors()
    return pl.pallas_call(
        functools.partial(_even_proj_kernel, tm=tm, seq=seq),
        grid=(m // tm,),
        in_specs=[pl.BlockSpec((tm, D_MODEL), row), pl.BlockSpec((1, D_MODEL), const),
                  pl.BlockSpec(w_main.shape, const), pl.BlockSpec(w_fb.shape, const),
                  pl.BlockSpec((1, LANES), const), pl.BlockSpec(sel_q.shape, const),
                  pl.BlockSpec(sel_k.shape, const)],
        out_specs=[pl.BlockSpec((tm, 4 * A_WIDTH), row), pl.BlockSpec((tm, B_WIDTH), row),
                   pl.BlockSpec((tm, 2 * B_WIDTH), row), pl.BlockSpec((tm, 2 * B_WIDTH), row),
                   pl.BlockSpec((tm, B_WIDTH), row), pl.BlockSpec((tm * B_HEADS, B_HEAD_DIM), row),
                   pl.BlockSpec((tm * B_HEADS, B_HEAD_DIM), row), pl.BlockSpec((tm, LANES), row),
                   pl.BlockSpec((tm, LANES), row)],
        out_shape=[jax.ShapeDtypeStruct((m, 4 * A_WIDTH), F32), jax.ShapeDtypeStruct((m, B_WIDTH), F32),
                   jax.ShapeDtypeStruct((m, 2 * B_WIDTH), BF16), jax.ShapeDtypeStruct((m, 2 * B_WIDTH), BF16),
                   jax.ShapeDtypeStruct((m, B_WIDTH), BF16),
                   jax.ShapeDtypeStruct((m * B_HEADS, B_HEAD_DIM), F32),
                   jax.ShapeDtypeStruct((m * B_HEADS, B_HEAD_DIM), F32),
                   jax.ShapeDtypeStruct((m, LANES), F32), jax.ShapeDtypeStruct((m, LANES), F32)],
        scratch_shapes=[pltpu.VMEM((SUBLANES, LANES), F32)],
        compiler_params=_cparams(("arbitrary",)),
        name="even_proj",
    )(x, g, w_main, w_fb, b_fox, sel_q, sel_k)


def _odd_proj_kernel(x_ref, g_ref, w_ref, wr_ref, wg_ref, bg_ref, proj_ref, lf_ref):
    h = _rmsnorm_rows(x_ref[...], g_ref[...]).astype(BF16)
    _proj_cols(h, w_ref, proj_ref)
    r = _dot_wt(h, wr_ref, 0, LANES)
    z = jnp.dot(r.astype(BF16), wg_ref[...], preferred_element_type=F32) + bg_ref[...]
    lf_ref[...] = _log_sigmoid(z) / GLA_GATE_NORMALIZER


def _odd_proj(x, g, w_main, w_r, w_gate, b_gate, *, tm):
    m = x.shape[0]
    n = w_main.shape[0]
    const = lambda i: (0, 0)
    row = lambda i: (i, 0)
    return pl.pallas_call(
        _odd_proj_kernel,
        grid=(m // tm,),
        in_specs=[pl.BlockSpec((tm, D_MODEL), row), pl.BlockSpec((1, D_MODEL), const),
                  pl.BlockSpec(w_main.shape, const), pl.BlockSpec(w_r.shape, const),
                  pl.BlockSpec((LANES, C_KEY_WIDTH), const), pl.BlockSpec((1, C_KEY_WIDTH), const)],
        out_specs=[pl.BlockSpec((tm, n), row), pl.BlockSpec((tm, C_KEY_WIDTH), row)],
        out_shape=[jax.ShapeDtypeStruct((m, n), F32), jax.ShapeDtypeStruct((m, C_KEY_WIDTH), F32)],
        compiler_params=_cparams(("arbitrary",)),
        name="odd_proj",
    )(x, g, w_main, w_r, w_gate, b_gate)


def _lower_bound(logits, layer):
    e = jnp.exp(logits - jnp.max(logits, axis=0, keepdims=True))
    return jnp.sum(e[:layer + 1, :], axis=0, keepdims=True) / jnp.sum(e, axis=0, keepdims=True)


def _hgrn_gate(fa, lb):
    f = lb + (1.0 - lb) * _sigmoid(fa)
    return jnp.log(f), 1.0 - f


def _exact_group(q, k, v, g, st, lo, hi):
    n = SUBLANES
    row = lax.broadcasted_iota(jnp.int32, (n, 1), 0)
    valid = jnp.logical_and(row >= lo, row < hi)
    q = jnp.where(valid, q, 0.0)
    k = jnp.where(valid, k, 0.0)
    g = jnp.where(valid, g, 0.0)
    b = g
    for sh in (1, 2, 4):
        b = b + jnp.where(row >= sh, pltpu.roll(b, sh, 0), 0.0)
    o = lax.dot_general((q * jnp.exp(b)).astype(BF16), st.astype(BF16), _NT, preferred_element_type=F32)
    for s in range(lo, hi):
        w = jnp.exp(jnp.minimum(b - b[s:s + 1, :], 0.0))
        a = jnp.sum(q * k[s:s + 1, :] * w, axis=-1, keepdims=True)
        o = o + jnp.where(row >= s, a, 0.0) * v[s:s + 1, :]
    b_last = b[n - 1:n, :]
    k_hat = k * jnp.exp(b_last - b)
    st_new = st * jnp.exp(b_last) + lax.dot_general(v.astype(BF16), k_hat.astype(BF16), _TN,
                                                    preferred_element_type=F32)
    return o, st_new


def _head_norm_gate(o, gain, gate):
    y = o * lax.rsqrt(jnp.mean(o * o, axis=-1, keepdims=True) + EPS) * gain
    return y * _silu(gate)


def _gla_prompt_kernel(*refs, hgrn, heads, dk, dv, tb, q_scale):
    if hgrn:
        q_ref, fa_ref, v_ref, gate_ref, lb_ref, gain_ref, o_ref, s_ref, st_ref, b_ref, g_ref, k_ref = refs
    else:
        q_ref, k_ref, v_ref, gate_ref, g_ref, gain_ref, o_ref, s_ref, st_ref, b_ref = refs
    t = pl.program_id(1)
    n_chunks = tb // CHUNK
    if hgrn:
        lb = _lower_bound(lb_ref[...], HGRN_LAYER)

    @pl.when(t == 0)
    def _():
        st_ref[...] = jnp.zeros_like(st_ref)

    tri = _lower_tri(CHUNK, CHUNK)
    b_min = None
    for c in range(n_chunks):
        rows = slice(c * CHUNK, (c + 1) * CHUNK)
        if hgrn:
            g, k = _hgrn_gate(fa_ref[rows, :], lb)
            g_ref[rows, :] = g
            k_ref[rows, :] = k
        else:
            g = g_ref[rows, :]
        b = _dot01(tri, g)
        b_ref[rows, :] = b
        b_last = b[CHUNK - 1:CHUNK, :]
        b_min = b_last if b_min is None else jnp.minimum(b_min, b_last)
    chunk_form_ok = jnp.min(b_min) >= -MAX_CHUNK_LOG_DECAY

    def finish(rows, h, o_h):
        vsl = slice(h * dv, (h + 1) * dv)
        o_ref[rows, vsl] = _head_norm_gate(o_h, gain_ref[:, vsl], gate_ref[rows, vsl])

    @pl.when(chunk_form_ok)
    def _():
        r_i = lax.broadcasted_iota(jnp.int32, (CHUNK, CHUNK), 0)
        c_i = lax.broadcasted_iota(jnp.int32, (CHUNK, CHUNK), 1)
        causal = r_i >= c_i

        for c in range(n_chunks):
            rows = slice(c * CHUNK, (c + 1) * CHUNK)
            b = b_ref[rows, :]
            e_b = jnp.exp(b)
            b_last = b[CHUNK - 1:CHUNK, :]
            e_last = jnp.exp(b_last)
            q_t = q_ref[rows, :] * q_scale * e_b
            k_t = k_ref[rows, :] * jnp.exp(-b)
            k_hat = k_t * e_last
            for h in range(heads):
                ksl = slice(h * dk, (h + 1) * dk)
                vsl = slice(h * dv, (h + 1) * dv)
                qh = q_t[:, ksl].astype(BF16)
                vh = v_ref[rows, vsl].astype(BF16)
                a = lax.dot_general(qh, k_t[:, ksl].astype(BF16), _NT, preferred_element_type=F32)
                a = jnp.where(causal, a, 0.0).astype(BF16)
                st = st_ref[h]
                o_h = jnp.dot(a, vh, preferred_element_type=F32) + lax.dot_general(
                    qh, st.astype(BF16), _NT, preferred_element_type=F32)
                st_ref[h] = st * e_last[:, ksl] + lax.dot_general(
                    vh, k_hat[:, ksl].astype(BF16), _TN, preferred_element_type=F32)
                finish(rows, h, o_h)

    @pl.when(jnp.logical_not(chunk_form_ok))
    def _():
        def group(i, carry):
            rows = pl.ds(pl.multiple_of(i * SUBLANES, SUBLANES), SUBLANES)
            for h in range(heads):
                ksl = slice(h * dk, (h + 1) * dk)
                vsl = slice(h * dv, (h + 1) * dv)
                o_h, st_new = _exact_group(q_ref[rows, ksl] * q_scale, k_ref[rows, ksl], v_ref[rows, vsl],
                                           g_ref[rows, ksl], st_ref[h], 0, SUBLANES)
                st_ref[h] = st_new
                finish(rows, h, o_h)
            return carry

        lax.fori_loop(0, tb // SUBLANES, group, 0)

    @pl.when(t == pl.num_programs(1) - 1)
    def _():
        for h in range(heads):
            s_ref[0, h] = st_ref[h].T


def _gla_prompt(hgrn, arrays, col_blocks, small, gain, *, bsz, seq, heads, dk, dv, tb, q_scale):
    nt = seq // tb
    kw, vw = heads * dk, heads * dv
    widths = [kw, kw, vw, vw] + ([] if hgrn else [kw])
    in_specs = [pl.BlockSpec((tb, w), functools.partial(lambda b, t, cb: (b * nt + t, cb), cb=cb))
                for w, cb in zip(widths, col_blocks)]
    operands = list(arrays)
    if hgrn:
        in_specs.append(pl.BlockSpec(small.shape, lambda b, t: (0, 0)))
        operands.append(small)
    in_specs.append(pl.BlockSpec((1, vw), lambda b, t: (0, 0)))
    operands.append(gain)
    scratch = [pltpu.VMEM((heads, dv, dk), F32), pltpu.VMEM((tb, kw), F32)]
    if hgrn:
        scratch += [pltpu.VMEM((tb, kw), F32), pltpu.VMEM((tb, kw), F32)]
    return pl.pallas_call(
        functools.partial(_gla_prompt_kernel, hgrn=hgrn, heads=heads, dk=dk, dv=dv, tb=tb, q_scale=q_scale),
        grid=(bsz, nt),
        in_specs=in_specs,
        out_specs=[pl.BlockSpec((tb, vw), lambda b, t: (b * nt + t, 0)),
                   pl.BlockSpec((1, heads, dk, dv), lambda b, t: (b, 0, 0, 0))],
        out_shape=[jax.ShapeDtypeStruct((bsz * seq, vw), F32), jax.ShapeDtypeStruct((bsz, heads, dk, dv), F32)],
        scratch_shapes=scratch,
        compiler_params=_cparams(("arbitrary", "arbitrary")),
        name="hgrn_prompt" if hgrn else "gla_prompt",
    )(*operands)


def _gla_sample_kernel(*refs, hgrn, heads, dk, dv, t_new, q_scale):
    if hgrn:
        q_ref, fa_ref, v_ref, gate_ref, lb_ref, gain_ref, s0_ref, o_ref, s_ref = refs
    else:
        q_ref, k_ref, v_ref, gate_ref, g_ref, gain_ref, s0_ref, o_ref, s_ref = refs
    for h in range(heads):
        ksl = slice(h * dk, (h + 1) * dk)
        vsl = slice(h * dv, (h + 1) * dv)
        if hgrn:
            g, k = _hgrn_gate(fa_ref[:, ksl], _lower_bound(lb_ref[...], HGRN_LAYER)[:, ksl])
        else:
            g, k = g_ref[:, ksl], k_ref[:, ksl]
        q = q_ref[:, ksl] * q_scale
        v = v_ref[:, vsl]
        o_h = None
        for e in range(SUBLANES // t_new):
            o_e, st_new = _exact_group(q, k, v, g, s0_ref[e, h].T, e * t_new, (e + 1) * t_new)
            s_ref[e, h] = st_new.T
            o_h = o_e if o_h is None else o_h + o_e
        o_ref[:, vsl] = _head_norm_gate(o_h, gain_ref[:, vsl], gate_ref[:, vsl])


def _gla_sample(hgrn, arrays, col_blocks, small, gain, s0, *, t_new, heads, dk, dv, q_scale):
    m = arrays[0].shape[0]
    per = SUBLANES // t_new
    kw, vw = heads * dk, heads * dv
    widths = [kw, kw, vw, vw] + ([] if hgrn else [kw])
    in_specs = [pl.BlockSpec((SUBLANES, w), functools.partial(lambda i, cb: (i, cb), cb=cb))
                for w, cb in zip(widths, col_blocks)]
    operands = list(arrays)
    if hgrn:
        in_specs.append(pl.BlockSpec(small.shape, lambda i: (0, 0)))
        operands.append(small)
    in_specs.append(pl.BlockSpec((1, vw), lambda i: (0, 0)))
    operands.append(gain)
    in_specs.append(pl.BlockSpec((per, heads, dk, dv), lambda i: (i, 0, 0, 0)))
    operands.append(s0)
    return pl.pallas_call(
        functools.partial(_gla_sample_kernel, hgrn=hgrn, heads=heads, dk=dk, dv=dv, t_new=t_new, q_scale=q_scale),
        grid=(m // SUBLANES,),
        in_specs=in_specs,
        out_specs=[pl.BlockSpec((SUBLANES, vw), lambda i: (i, 0)),
                   pl.BlockSpec((per, heads, dk, dv), lambda i: (i, 0, 0, 0))],
        out_shape=[jax.ShapeDtypeStruct((m, vw), F32), jax.ShapeDtypeStruct(s0.shape, F32)],
        compiler_params=_cparams(("arbitrary",)),
        name="hgrn_sample" if hgrn else "gla_sample",
    )(*operands)


FOX_STRIP = 64


def _fox_prompt_kernel(qi_ref, ki_ref, q_ref, k_ref, v_ref, gate_ref, o_ref, m_ref, l_ref, acc_ref, p_ref, a_ref,
                       *, tq, heads, dh):
    p = pl.program_id(1)
    qi = qi_ref[p]
    ki = ki_ref[p]
    ncb = tq // LANES
    aw = 2 * dh

    @pl.when(ki == 0)
    def _():
        m_ref[...] = jnp.full_like(m_ref, -jnp.inf)
        l_ref[...] = jnp.zeros_like(l_ref)
        acc_ref[...] = jnp.zeros_like(acc_ref)

    def step(diag):
        if diag:
            r_i = lax.broadcasted_iota(jnp.int32, (FOX_STRIP, LANES), 0)
            c_i = lax.broadcasted_iota(jnp.int32, (FOX_STRIP, LANES), 1)
        for h in range(heads):
            sl = slice(h * dh, (h + 1) * dh)
            s = lax.dot_general(q_ref[:, h * aw:(h + 1) * aw], k_ref[:, h * aw:(h + 1) * aw], _NT,
                                preferred_element_type=F32)
            for r0 in range(0, tq, FOX_STRIP):
                rows = slice(r0, r0 + FOX_STRIP)
                live = [j for j in range(ncb) if not (diag and j * LANES > r0 + FOX_STRIP - 1)]
                blocks = []
                for j in live:
                    lg = s[rows, j * LANES:(j + 1) * LANES]
                    if diag and (j + 1) * LANES - 1 > r0:
                        lg = jnp.where(r_i + r0 >= c_i + j * LANES, lg, -jnp.inf)
                    blocks.append(lg)
                m_cur = blocks[0]
                for lg in blocks[1:]:
                    m_cur = jnp.maximum(m_cur, lg)
                m_prev = m_ref[h, rows]
                m_new = jnp.maximum(m_prev, jnp.max(m_cur, axis=-1, keepdims=True))
                alpha = jnp.exp2(m_prev - m_new)
                probs = [jnp.exp2(lg - m_new) for lg in blocks]
                row_sum = probs[0]
                for pj in probs[1:]:
                    row_sum = row_sum + pj
                l_ref[h, rows] = alpha * l_ref[h, rows] + jnp.sum(row_sum, axis=-1, keepdims=True)
                m_ref[h, rows] = m_new
                a_ref[rows] = alpha
                for j, pj in zip(live, probs):
                    p_ref[rows, j * LANES:(j + 1) * LANES] = pj.astype(BF16)
                for j in range(ncb):
                    if j not in live:
                        p_ref[rows, j * LANES:(j + 1) * LANES] = jnp.zeros((FOX_STRIP, LANES), BF16)
            acc_ref[:, sl] = a_ref[...] * acc_ref[:, sl] + jnp.dot(p_ref[...], v_ref[:, sl],
                                                                   preferred_element_type=F32)

    @pl.when(ki < qi)
    def _():
        step(False)

    @pl.when(ki == qi)
    def _():
        step(True)
        for h in range(heads):
            sl = slice(h * dh, (h + 1) * dh)
            o_ref[:, sl] = acc_ref[:, sl] / l_ref[h] * _silu(gate_ref[:, sl])


def _fox_prompt(qa, ka, vb, gate, *, bsz, seq, tq):
    nq = seq // tq
    pairs = [(qi, ki) for qi in range(nq) for ki in range(qi + 1)]
    qi_tab = jnp.asarray(np.array([p[0] for p in pairs], np.int32))
    ki_tab = jnp.asarray(np.array([p[1] for p in pairs], np.int32))
    w = B_WIDTH
    q_map = lambda b, p, qt, kt: (b * nq + qt[p], 0)
    k_map = lambda b, p, qt, kt: (b * nq + kt[p], 0)
    grid_spec = pltpu.PrefetchScalarGridSpec(
        num_scalar_prefetch=2,
        grid=(bsz, len(pairs)),
        in_specs=[pl.BlockSpec((tq, 2 * w), q_map), pl.BlockSpec((tq, 2 * w), k_map),
                  pl.BlockSpec((tq, w), k_map), pl.BlockSpec((tq, w), q_map)],
        out_specs=pl.BlockSpec((tq, w), q_map),
        scratch_shapes=[pltpu.VMEM((B_HEADS, tq, LANES), F32), pltpu.VMEM((B_HEADS, tq, LANES), F32),
                        pltpu.VMEM((tq, w), F32), pltpu.VMEM((tq, tq), BF16), pltpu.VMEM((tq, LANES), F32)],
    )
    return pl.pallas_call(
        functools.partial(_fox_prompt_kernel, tq=tq, heads=B_HEADS, dh=B_HEAD_DIM),
        grid_spec=grid_spec,
        out_shape=jax.ShapeDtypeStruct((bsz * seq, w), F32),
        compiler_params=_cparams(("arbitrary", "arbitrary")),
        name="fox_prompt",
    )(qi_tab, ki_tab, qa, ka, vb, gate)


def _page_suffix_kernel(lf_ref, later_ref, total_ref, out_ref):
    lf = lf_ref[...]
    pw = lf.shape[1]
    out_ref[:, 0:pw] = _x_dot01(lf, later_ref[...])
    out_ref[:, pw:2 * pw] = _x_dot01(lf, total_ref[...])


def _page_suffix(lf_pages, heads):
    n_pool, pw = lf_pages.shape
    tm = 512 if n_pool % 512 == 0 else n_pool
    j = np.arange(pw)
    in_head, in_tok = j[:, None] // (pw // heads), j[:, None] % (pw // heads)
    out_head, out_tok = j[None, :] % heads, j[None, :] // heads
    head_eq = in_head == out_head
    later = jnp.asarray(head_eq & (in_tok > out_tok), BF16)
    total = jnp.asarray(head_eq, BF16)
    row = lambda i: (i, 0)
    const = lambda i: (0, 0)
    return pl.pallas_call(
        _page_suffix_kernel,
        grid=(n_pool // tm,),
        in_specs=[pl.BlockSpec((tm, pw), row), pl.BlockSpec((pw, pw), const), pl.BlockSpec((pw, pw), const)],
        out_specs=pl.BlockSpec((tm, 2 * pw), row),
        out_shape=jax.ShapeDtypeStruct((n_pool, 2 * pw), F32),
        compiler_params=_cparams(("arbitrary",)),
        name="page_suffix",
    )(lf_pages, later, total)


def _fox_sample_kernel(pt_ref, k_hbm, v_hbm, sfx_hbm, q_ref, kn_ref, vn_ref, gate_ref, cn_col_ref, cn_row_ref, o_ref,
                       kbuf, vbuf, sbuf, sem, m_ref, l_ref, acc_ref, carry_ref, *, pages_per_step, heads, dh, t_new,
                       n_pages):
    pp = pages_per_step
    b = pl.program_id(0)
    j = pl.program_id(1)
    nj = pl.num_programs(1)
    nrow = t_new * heads
    pw = PAGE_SIZE * heads
    step = b * nj + j
    slot = lax.rem(step, 2)

    def page_copies(bb, jj, sl):
        copies = []
        for i in range(pp):
            page = pt_ref[bb, n_pages - 1 - (jj * pp + i)]
            for kind, (src, dst) in enumerate(((k_hbm, kbuf), (v_hbm, vbuf), (sfx_hbm, sbuf))):
                copies.append(pltpu.make_async_copy(src.at[page], dst.at[sl, i], sem.at[kind, sl]))
        return copies

    @pl.when(step == 0)
    def _():
        for cp in page_copies(b, j, slot):
            cp.start()

    @pl.when(step + 1 < pl.num_programs(0) * nj)
    def _():
        wrap = j + 1 == nj
        for cp in page_copies(jnp.where(wrap, b + 1, b), jnp.where(wrap, 0, j + 1), 1 - slot):
            cp.start()

    for cp in page_copies(b, j, slot):
        cp.wait()
    k_refs = [kbuf.at[slot, i] for i in range(pp)]
    v_refs = [vbuf.at[slot, i] for i in range(pp)]
    sfx_refs = [sbuf.at[slot, i] for i in range(pp)]

    @pl.when(j == 0)
    def _():
        m_ref[...] = jnp.full_like(m_ref, -jnp.inf)
        l_ref[...] = jnp.zeros_like(l_ref)
        acc_ref[...] = jnp.zeros_like(acc_ref)
        carry_ref[...] = jnp.zeros_like(carry_ref)

    q = q_ref[0]
    cn_col = cn_col_ref[0] * LOG2E
    row_head = lax.broadcasted_iota(jnp.int32, (nrow, pw), 0) % heads
    col_head = lax.broadcasted_iota(jnp.int32, (nrow, pw), 1) % heads
    same_head = row_head == col_head

    def online(logit_list, v_list):
        m_cur = logit_list[0]
        for lg in logit_list[1:]:
            m_cur = jnp.maximum(m_cur, lg)
        m_prev = m_ref[...]
        m_new = jnp.maximum(m_prev, jnp.max(m_cur, axis=-1, keepdims=True))
        alpha = jnp.exp2(m_prev - m_new)
        probs = [jnp.exp2(lg - m_new) for lg in logit_list]
        row_sum = probs[0]
        for pj in probs[1:]:
            row_sum = row_sum + pj
        l_ref[...] = alpha * l_ref[...] + jnp.sum(row_sum, axis=-1, keepdims=True)
        pv = None
        for pj, v in zip(probs, v_list):
            t = jnp.dot(pj.astype(BF16), v.astype(BF16), preferred_element_type=F32)
            pv = t if pv is None else pv + t
        acc_ref[...] = alpha * acc_ref[...] + pv
        m_ref[...] = m_new

    logit_list = []
    carry = carry_ref[...]
    for i in range(pp):
        suffix = (carry + sfx_refs[i][:, 0:pw]) * LOG2E
        carry = carry + sfx_refs[i][:, pw:2 * pw]
        s = lax.dot_general(q, k_refs[i][...].astype(BF16), _NT, preferred_element_type=F32)
        logit_list.append(jnp.where(same_head, s + cn_col + suffix, -jnp.inf))
    carry_ref[...] = carry
    online(logit_list, [v_refs[i][...] for i in range(pp)])

    @pl.when(j == pl.num_programs(1) - 1)
    def _():
        s = lax.dot_general(q, kn_ref[0].astype(BF16), _NT, preferred_element_type=F32)
        logits = s + cn_col - cn_row_ref[0] * LOG2E
        r = lax.broadcasted_iota(jnp.int32, (nrow, nrow), 0)
        c = lax.broadcasted_iota(jnp.int32, (nrow, nrow), 1)
        keep = jnp.logical_and((r % heads) == (c % heads), (r // heads) >= (c // heads))
        online([jnp.where(keep, logits, -jnp.inf)], [vn_ref[0]])
        o_ref[0] = acc_ref[...] / l_ref[...] * _silu(gate_ref[0])


def _fox_sample(page_table, k_pages, v_pages, lf_pages, q, k_new, v_new, gate, cn_col, cn_row, *, pages_per_step):
    n_pool = lf_pages.shape[0]
    sfx = _page_suffix(lf_pages, B_HEADS).reshape(n_pool, 1, -1)
    db, n_pages = page_table.shape
    pp = pages_per_step
    nrow = q.shape[1]
    pw = PAGE_SIZE * B_HEADS
    dh = B_HEAD_DIM
    per_b = lambda b, j, pt: (b, 0, 0)
    hbm = pl.BlockSpec(memory_space=pl.ANY)
    in_specs = ([hbm, hbm, hbm] + [pl.BlockSpec((1, nrow, dh), per_b)] * 4
                + [pl.BlockSpec((1, nrow, 1), per_b), pl.BlockSpec((1, 1, nrow), per_b)])
    n_slots = 2
    grid_spec = pltpu.PrefetchScalarGridSpec(
        num_scalar_prefetch=1,
        grid=(db, n_pages // pp),
        in_specs=in_specs,
        out_specs=pl.BlockSpec((1, nrow, dh), per_b),
        scratch_shapes=[pltpu.VMEM((n_slots, pp, pw, dh), F32), pltpu.VMEM((n_slots, pp, pw, dh), F32),
                        pltpu.VMEM((n_slots, pp, 1, 2 * pw), F32), pltpu.SemaphoreType.DMA((3, n_slots)),
                        pltpu.VMEM((nrow, 1), F32), pltpu.VMEM((nrow, 1), F32), pltpu.VMEM((nrow, dh), F32),
                        pltpu.VMEM((1, pw), F32)],
    )
    return pl.pallas_call(
        functools.partial(_fox_sample_kernel, pages_per_step=pp, heads=B_HEADS, dh=dh, t_new=nrow // B_HEADS,
                          n_pages=n_pages),
        grid_spec=grid_spec,
        out_shape=jax.ShapeDtypeStruct((db, nrow, dh), F32),
        compiler_params=_cparams(("arbitrary", "arbitrary")),
        name="fox_sample",
    )(page_table, k_pages, v_pages, sfx, q, k_new, v_new, gate, cn_col, cn_row)


def _out_proj_kernel(*refs, n_in, final):
    ins = refs[:n_in]
    w_ref, x_ref = refs[n_in], refs[n_in + 1]
    y = x_ref[...]
    k0 = 0
    for a_ref in ins:
        kw = a_ref.shape[1]
        y = y + jnp.dot(a_ref[...].astype(BF16), w_ref[k0:k0 + kw, :], preferred_element_type=F32)
        k0 += kw
    if final:
        g_ref, o_ref = refs[n_in + 2], refs[n_in + 3]
        o_ref[...] = _rmsnorm_rows(y, g_ref[...])
    else:
        refs[n_in + 2][...] = y


def _out_proj(ins, w, x, final_gain, *, tm):
    m = x.shape[0]
    const = lambda i: (0, 0)
    row = lambda i: (i, 0)
    in_specs = [pl.BlockSpec((tm, a.shape[1]), row) for a in ins]
    in_specs += [pl.BlockSpec(w.shape, const), pl.BlockSpec((tm, D_MODEL), row)]
    operands = list(ins) + [w, x]
    if final_gain is not None:
        in_specs.append(pl.BlockSpec((1, D_MODEL), const))
        operands.append(final_gain)
    return pl.pallas_call(
        functools.partial(_out_proj_kernel, n_in=len(ins), final=final_gain is not None),
        grid=(m // tm,),
        in_specs=in_specs,
        out_specs=pl.BlockSpec((tm, D_MODEL), row),
        out_shape=jax.ShapeDtypeStruct((m, D_MODEL), F32),
        compiler_params=_cparams(("arbitrary",)),
        name="out_proj_final" if final_gain is not None else "out_proj",
    )(*operands)


def _pad_cols(w, n):
    return jnp.pad(w, ((0, 0), (0, n - w.shape[1])))


def _prep_weights(weights):
    (norm_even, w_in_even, b_fox_f, lb_logits, hgrn_gain, w_out_even, norm_odd, w_in_odd, w_gla_gate, b_gla_gate,
     gla_gain, w_out_odd, final_norm) = weights
    n_even = 4 * A_WIDTH + 4 * B_WIDTH
    n_odd = 2 * C_KEY_WIDTH + 2 * C_VAL_WIDTH
    w_even_t = w_in_even[0].T.astype(BF16)
    w_odd_t = w_in_odd[0].T.astype(BF16)
    pad_rows = lambda a: jnp.pad(a, ((0, LANES - a.shape[0]), (0, 0)))
    return dict(
        norm_even=norm_even[0].reshape(1, D_MODEL),
        w_even=w_even_t[:n_even],
        w_fb=pad_rows(w_even_t[n_even:]),
        b_fox=_pad_cols(b_fox_f[0].reshape(1, B_HEADS), LANES),
        lb_logits=lb_logits,
        hgrn_gain=hgrn_gain[0].reshape(1, A_WIDTH),
        w_out_even=w_out_even[0].astype(BF16),
        norm_odd=norm_odd[0].reshape(1, D_MODEL),
        w_odd=w_odd_t[:n_odd],
        w_r=pad_rows(w_odd_t[n_odd:]),
        w_gate=jnp.pad(w_gla_gate[0], ((0, LANES - C_GATE_RANK), (0, 0))).astype(BF16),
        b_gate=b_gla_gate[0].reshape(1, C_KEY_WIDTH),
        gla_gain=gla_gain[0].reshape(1, C_VAL_WIDTH),
        w_out_odd=w_out_odd[0].astype(BF16),
        final_norm=final_norm.reshape(1, D_MODEL),
    )


def _trunk(x, w, *, bsz, seq, prompt, sample_ctx):
    m = bsz * seq
    tm = 512 if m % 512 == 0 else m
    x2 = x.reshape(m, D_MODEL)

    pa, gate_b, qa, ka, vb16, k_rows, v_rows, lf, c = _even_proj(x2, w["norm_even"], w["w_even"], w["w_fb"],
                                                                 w["b_fox"], seq=seq, tm=tm)
    hg = dict(heads=A_HEADS, dk=A_HEAD_DIM, dv=A_HEAD_DIM, q_scale=1.0)
    if prompt:
        o_a, s_a = _gla_prompt(True, [pa] * 4, [0, 1, 2, 3], w["lb_logits"], w["hgrn_gain"], bsz=bsz, seq=seq,
                               tb=512, **hg)
        o_b = _fox_prompt(qa, ka, vb16, gate_b, bsz=bsz, seq=seq, tq=min(seq, 1024))
    else:
        o_a, s_a = _gla_sample(True, [pa] * 4, [0, 1, 2, 3], w["lb_logits"], w["hgrn_gain"],
                               sample_ctx["state_hgrn"], t_new=seq, **hg)
        nrow = seq * B_HEADS
        rows = lambda a: a.reshape(bsz, nrow, B_HEAD_DIM)
        cn = c[:, :B_HEADS].reshape(bsz, nrow)
        q_rows = qa.reshape(bsz, nrow, 2 * B_HEAD_DIM)[:, :, :B_HEAD_DIM]
        o_b = _fox_sample(sample_ctx["page_table"], sample_ctx["k_pages"], sample_ctx["v_pages"],
                          sample_ctx["lf_pages"], q_rows, rows(k_rows), rows(v_rows), rows(gate_b),
                          cn.reshape(bsz, nrow, 1), cn.reshape(bsz, 1, nrow), pages_per_step=8)
        o_b = o_b.reshape(m, B_WIDTH)
    x1 = _out_proj([o_a, o_b], w["w_out_even"], x2, None, tm=tm)
    kb = k_rows.reshape(bsz, seq, B_HEADS, B_HEAD_DIM)
    vb = v_rows.reshape(bsz, seq, B_HEADS, B_HEAD_DIM)
    lfb = lf[:, :B_HEADS].reshape(bsz, seq, B_HEADS)

    proj1, lf1 = _odd_proj(x1, w["norm_odd"], w["w_odd"], w["w_r"], w["w_gate"], w["b_gate"], tm=tm)
    gl = dict(heads=C_HEADS, dk=C_KEY_DIM, dv=C_VAL_DIM, q_scale=C_KEY_DIM ** -0.5)
    arrays = [proj1, proj1, proj1, proj1, lf1]
    col_blocks = [0, 1, 1, 2, 0]
    if prompt:
        o_c, s_c = _gla_prompt(False, arrays, col_blocks, None, w["gla_gain"], bsz=bsz, seq=seq, tb=512, **gl)
    else:
        o_c, s_c = _gla_sample(False, arrays, col_blocks, None, w["gla_gain"], sample_ctx["state_gla"],
                               t_new=seq, **gl)
    y = _out_proj([o_c], w["w_out_odd"], x1, w["final_norm"], tm=tm)
    return y.reshape(bsz, seq, D_MODEL), kb, vb, lfb, s_a, s_c


def kernel(x_prompt, x_sample, cache_fox_k, cache_fox_v, cache_fox_logf, state_hgrn, state_gla, page_table,
           norm_even, w_in_even, b_fox_f, lb_logits, hgrn_gain, w_out_even, norm_odd, w_in_odd, w_gla_gate,
           b_gla_gate, gla_gain, w_out_odd, final_norm):
    weights = _prep_weights((norm_even, w_in_even, b_fox_f, lb_logits, hgrn_gain, w_out_even, norm_odd, w_in_odd,
                             w_gla_gate, b_gla_gate, gla_gain, w_out_odd, final_norm))
    bsz, seq, _ = x_prompt.shape
    y_p, kp, vp, lfp, hgrn_p, gla_p = _trunk(x_prompt, weights, bsz=bsz, seq=seq, prompt=True, sample_ctx=None)
    n_pp = seq // PAGE_SIZE
    fox_k_prompt = kp.reshape(1, bsz, n_pp, PAGE_SIZE, B_HEADS, B_HEAD_DIM)
    fox_v_prompt = vp.reshape(1, bsz, n_pp, PAGE_SIZE, B_HEADS, B_HEAD_DIM)
    fox_logf_prompt = lfp.reshape(1, bsz, n_pp, PAGE_SIZE, B_HEADS)

    db, t_new, _ = x_sample.shape
    n_pool = cache_fox_k.shape[1]
    pw = PAGE_SIZE * B_HEADS
    ctx = dict(
        page_table=page_table,
        k_pages=cache_fox_k[0].reshape(n_pool, pw, B_HEAD_DIM),
        v_pages=cache_fox_v[0].reshape(n_pool, pw, B_HEAD_DIM),
        lf_pages=cache_fox_logf[0].transpose(0, 2, 1).reshape(n_pool, pw),
        state_hgrn=state_hgrn[0], state_gla=state_gla[0])
    y_s, ks, vs, lfs, hgrn_s, gla_s = _trunk(x_sample, weights, bsz=db, seq=t_new, prompt=False, sample_ctx=ctx)
    return (y_p, y_s, fox_k_prompt, fox_v_prompt, fox_logf_prompt, hgrn_p[None], gla_p[None],
            ks[None], vs[None], lfs[None], hgrn_s[None], gla_s[None])
```

```python
import functools

import numpy as np
import jax
import jax.numpy as jnp
from jax import lax
from jax.experimental import pallas as pl
from jax.experimental.pallas import tpu as pltpu

F32 = jnp.float32
BF16 = jnp.bfloat16

D_MODEL = 1024
PAGE_SIZE = 128
A_HEADS = 4
A_HEAD_DIM = 128
A_WIDTH = 512
B_HEADS = 4
B_HEAD_DIM = 128
B_WIDTH = 512
C_HEADS = 4
C_KEY_WIDTH = 512
C_VAL_WIDTH = 1024
C_KEY_DIM = 128
C_VAL_DIM = 256
C_GATE_RANK = 16
GLA_GATE_NORMALIZER = 16.0
EPS = 1e-6
HGRN_LAYER = 0
LOG2E = 1.4426950408889634
AUG_PIECES = 3

LANES = 128
SUBLANES = 8
VMEM_LIMIT = 56 * 1024 * 1024
CHUNK = 64
MAX_CHUNK_LOG_DECAY = 60.0

_NT = (((1,), (1,)), ((), ()))
_TN = (((0,), (0,)), ((), ()))


def _cparams(sem):
    return pltpu.CompilerParams(dimension_semantics=sem, vmem_limit_bytes=VMEM_LIMIT)


def _sigmoid(x):
    return 1.0 / (1.0 + jnp.exp(-x))


def _log_sigmoid(x):
    return jnp.minimum(x, 0.0) - jnp.log1p(jnp.exp(-jnp.abs(x)))


def _silu(x):
    return x * _sigmoid(x)


def _rmsnorm_rows(x, g):
    return x * lax.rsqrt(jnp.mean(x * x, axis=-1, keepdims=True) + EPS) * g


def _split3(x):
    p1 = x.astype(BF16)
    r1 = x - p1.astype(F32)
    p2 = r1.astype(BF16)
    p3 = (r1 - p2.astype(F32)).astype(BF16)
    return p1, p2, p3


def _dot01(m01, x):
    acc = None
    for p in _split3(x):
        t = jnp.dot(m01, p, preferred_element_type=F32)
        acc = t if acc is None else acc + t
    return acc


def _x_dot01(x, m01):
    acc = None
    for p in _split3(x):
        t = jnp.dot(p, m01, preferred_element_type=F32)
        acc = t if acc is None else acc + t
    return acc


def _lower_tri(n, seq):
    r = lax.broadcasted_iota(jnp.int32, (n, n), 0)
    c = lax.broadcasted_iota(jnp.int32, (n, n), 1)
    keep = r >= c
    if seq < n:
        keep = jnp.logical_and(keep, (r // seq) == (c // seq))
    return jnp.where(keep, 1.0, 0.0).astype(BF16)


def _dot_wt(h, wt_ref, c0, n):
    return lax.dot_general(h, wt_ref[c0:c0 + n, :], _NT, preferred_element_type=F32)


def _proj_cols(h, wt_ref, out_ref):
    step = 512
    for c0 in range(0, wt_ref.shape[0], step):
        out_ref[:, c0:c0 + step] = _dot_wt(h, wt_ref, c0, step)


def _aug_selectors():
    sel_q = np.zeros((LANES, B_WIDTH), np.float32)
    sel_k = np.zeros((LANES, B_WIDTH), np.float32)
    for p in range(AUG_PIECES):
        for hd in range(B_HEADS):
            sel_q[p * B_HEADS + hd, hd * B_HEAD_DIM + p] = 1.0
            sel_k[p * B_HEADS + hd, hd * B_HEAD_DIM + AUG_PIECES + p] = -1.0
    return jnp.asarray(sel_q, BF16), jnp.asarray(sel_k, BF16)


def _even_proj_kernel(x_ref, g_ref, w_ref, wfb_ref, bfox_ref, selq_ref, selk_ref, pa_ref, gate_ref, qa_ref, ka_ref,
                      vb_ref, kout_ref, vout_ref, lf_ref, c_ref, carry_ref, *, tm, seq):
    i = pl.program_id(0)
    h = _rmsnorm_rows(x_ref[...], g_ref[...]).astype(BF16)
    na = 4 * A_WIDTH
    bw = B_WIDTH
    dh = B_HEAD_DIM
    for c0 in range(0, na, 512):
        pa_ref[:, c0:c0 + 512] = _dot_wt(h, w_ref, c0, 512)
    q = (_dot_wt(h, w_ref, na, bw) * (dh ** -0.5 * LOG2E)).astype(BF16)
    for hd in range(B_HEADS):
        qa_ref[:, 2 * hd * dh:(2 * hd + 1) * dh] = q[:, hd * dh:(hd + 1) * dh]
    for j, out_ref in ((1, kout_ref), (2, vout_ref)):
        kv = _dot_wt(h, w_ref, na + j * bw, bw)
        kv16 = kv.astype(BF16)
        if j == 1:
            for hd in range(B_HEADS):
                ka_ref[:, 2 * hd * dh:(2 * hd + 1) * dh] = kv16[:, hd * dh:(hd + 1) * dh]
        else:
            vb_ref[...] = kv16
        for hd in range(B_HEADS):
            out_ref[pl.ds(hd, tm, stride=B_HEADS), :] = kv[:, hd * dh:(hd + 1) * dh]
    gate_ref[...] = _dot_wt(h, w_ref, na + 3 * bw, bw)
    fb = _dot_wt(h, wfb_ref, 0, LANES) + bfox_ref[...]
    lane = lax.broadcasted_iota(jnp.int32, fb.shape, 1)
    lf = jnp.where(lane < B_HEADS, _log_sigmoid(fb), 0.0)
    lf_ref[...] = lf
    cs = _dot01(_lower_tri(tm, seq), lf)
    if seq > tm:
        @pl.when((i * tm) % seq == 0)
        def _():
            carry_ref[...] = jnp.zeros_like(carry_ref)
        cs = cs + carry_ref[0:1, :]
        carry_ref[0:1, :] = cs[tm - 1:tm, :]
    c_ref[...] = cs
    pos = lax.broadcasted_iota(jnp.int32, (1, bw), 1) % dh
    aug_q = jnp.where(jnp.logical_and(pos >= AUG_PIECES, pos < 2 * AUG_PIECES), 1.0, 0.0)
    aug_k = jnp.where(pos < AUG_PIECES, 1.0, 0.0)
    packed = None
    for p, piece in enumerate(_split3(cs * LOG2E)):
        shifted = piece.astype(F32) if p == 0 else pltpu.roll(piece.astype(F32), p * B_HEADS, 1)
        packed = shifted if packed is None else packed + shifted
    packed = packed.astype(BF16)
    aug_q = (aug_q + jnp.dot(packed, selq_ref[...], preferred_element_type=F32)).astype(BF16)
    aug_k = (aug_k + jnp.dot(packed, selk_ref[...], preferred_element_type=F32)).astype(BF16)
    for hd in range(B_HEADS):
        qa_ref[:, (2 * hd + 1) * dh:(2 * hd + 2) * dh] = aug_q[:, hd * dh:(hd + 1) * dh]
        ka_ref[:, (2 * hd + 1) * dh:(2 * hd + 2) * dh] = aug_k[:, hd * dh:(hd + 1) * dh]


def _even_proj(x, g, w_main, w_fb, b_fox, *, seq, tm):
    m = x.shape[0]
    const = lambda i: (0, 0)
    row = lambda i: (i, 0)
    sel_q, sel_k = _aug_selectors()
    return pl.pallas_call(
        functools.partial(_even_proj_kernel, tm=tm, seq=seq),
        grid=(m // tm,),
        in_specs=[pl.BlockSpec((tm, D_MODEL), row), pl.BlockSpec((1, D_MODEL), const),
                  pl.BlockSpec(w_main.shape, const), pl.BlockSpec(w_fb.shape, const),
                  pl.BlockSpec((1, LANES), const), pl.BlockSpec(sel_q.shape, const),
                  pl.BlockSpec(sel_k.shape, const)],
        out_specs=[pl.BlockSpec((tm, 4 * A_WIDTH), row), pl.BlockSpec((tm, B_WIDTH), row),
                   pl.BlockSpec((tm, 2 * B_WIDTH), row), pl.BlockSpec((tm, 2 * B_WIDTH), row),
                   pl.BlockSpec((tm, B_WIDTH), row), pl.BlockSpec((tm * B_HEADS, B_HEAD_DIM), row),
                   pl.BlockSpec((tm * B_HEADS, B_HEAD_DIM), row), pl.BlockSpec((tm, LANES), row),
                   pl.BlockSpec((tm, LANES), row)],
        out_shape=[jax.ShapeDtypeStruct((m, 4 * A_WIDTH), F32), jax.ShapeDtypeStruct((m, B_WIDTH), F32),
                   jax.ShapeDtypeStruct((m, 2 * B_WIDTH), BF16), jax.ShapeDtypeStruct((m, 2 * B_WIDTH), BF16),
                   jax.ShapeDtypeStruct((m, B_WIDTH), BF16),
                   jax.ShapeDtypeStruct((m * B_HEADS, B_HEAD_DIM), F32),
                   jax.ShapeDtypeStruct((m * B_HEADS, B_HEAD_DIM), F32),
                   jax.ShapeDtypeStruct((m, LANES), F32), jax.ShapeDtypeStruct((m, LANES), F32)],
        scratch_shapes=[pltpu.VMEM((SUBLANES, LANES), F32)],
        compiler_params=_cparams(("arbitrary",)),
        name="even_proj",
    )(x, g, w_main, w_fb, b_fox, sel_q, sel_k)


def _odd_proj_kernel(x_ref, g_ref, w_ref, wr_ref, wg_ref, bg_ref, proj_ref, lf_ref):
    h = _rmsnorm_rows(x_ref[...], g_ref[...]).astype(BF16)
    _proj_cols(h, w_ref, proj_ref)
    r = _dot_wt(h, wr_ref, 0, LANES)
    z = jnp.dot(r.astype(BF16), wg_ref[...], preferred_element_type=F32) + bg_ref[...]
    lf_ref[...] = _log_sigmoid(z) / GLA_GATE_NORMALIZER


def _odd_proj(x, g, w_main, w_r, w_gate, b_gate, *, tm):
    m = x.shape[0]
    n = w_main.shape[0]
    const = lambda i: (0, 0)
    row = lambda i: (i, 0)
    return pl.pallas_call(
        _odd_proj_kernel,
        grid=(m // tm,),
        in_specs=[pl.BlockSpec((tm, D_MODEL), row), pl.BlockSpec((1, D_MODEL), const),
                  pl.BlockSpec(w_main.shape, const), pl.BlockSpec(w_r.shape, const),
                  pl.BlockSpec((LANES, C_KEY_WIDTH), const), pl.BlockSpec((1, C_KEY_WIDTH), const)],
        out_specs=[pl.BlockSpec((tm, n), row), pl.BlockSpec((tm, C_KEY_WIDTH), row)],
        out_shape=[jax.ShapeDtypeStruct((m, n), F32), jax.ShapeDtypeStruct((m, C_KEY_WIDTH), F32)],
        compiler_params=_cparams(("arbitrary",)),
        name="odd_proj",
    )(x, g, w_main, w_r, w_gate, b_gate)


def _lower_bound(logits, layer):
    e = jnp.exp(logits - jnp.max(logits, axis=0, keepdims=True))
    return jnp.sum(e[:layer + 1, :], axis=0, keepdims=True) / jnp.sum(e, axis=0, keepdims=True)


def _hgrn_gate(fa, lb):
    f = lb + (1.0 - lb) * _sigmoid(fa)
    return jnp.log(f), 1.0 - f


def _exact_group(q, k, v, g, st, lo, hi):
    n = SUBLANES
    row = lax.broadcasted_iota(jnp.int32, (n, 1), 0)
    valid = jnp.logical_and(row >= lo, row < hi)
    q = jnp.where(valid, q, 0.0)
    k = jnp.where(valid, k, 0.0)
    g = jnp.where(valid, g, 0.0)
    b = g
    for sh in (1, 2, 4):
        b = b + jnp.where(row >= sh, pltpu.roll(b, sh, 0), 0.0)
    o = lax.dot_general((q * jnp.exp(b)).astype(BF16), st.astype(BF16), _NT, preferred_element_type=F32)
    for s in range(lo, hi):
        w = jnp.exp(jnp.minimum(b - b[s:s + 1, :], 0.0))
        a = jnp.sum(q * k[s:s + 1, :] * w, axis=-1, keepdims=True)
        o = o + jnp.where(row >= s, a, 0.0) * v[s:s + 1, :]
    b_last = b[n - 1:n, :]
    k_hat = k * jnp.exp(b_last - b)
    st_new = st * jnp.exp(b_last) + lax.dot_general(v.astype(BF16), k_hat.astype(BF16), _TN,
                                                    preferred_element_type=F32)
    return o, st_new


def _head_norm_gate(o, gain, gate):
    y = o * lax.rsqrt(jnp.mean(o * o, axis=-1, keepdims=True) + EPS) * gain
    return y * _silu(gate)


def _recurrence_block(q_ref, k_ref, v_ref, gate_ref, g_ref, gain_ref, o_ref, st_ref, b_ref, *, fa_ref, lb, heads, dk,
                      dv, tb, q_scale):
    n_chunks = tb // CHUNK

    tri = _lower_tri(CHUNK, CHUNK)
    b_min = None
    for c in range(n_chunks):
        rows = slice(c * CHUNK, (c + 1) * CHUNK)
        if fa_ref is not None:
            g, k = _hgrn_gate(fa_ref[rows, :], lb)
            g_ref[rows, :] = g
            k_ref[rows, :] = k
        else:
            g = g_ref[rows, :]
        b = _dot01(tri, g)
        b_ref[rows, :] = b
        b_last = b[CHUNK - 1:CHUNK, :]
        b_min = b_last if b_min is None else jnp.minimum(b_min, b_last)
    chunk_form_ok = jnp.min(b_min) >= -MAX_CHUNK_LOG_DECAY

    def finish(rows, h, o_h):
        vsl = slice(h * dv, (h + 1) * dv)
        o_ref[rows, vsl] = _head_norm_gate(o_h, gain_ref[:, vsl], gate_ref[rows, vsl]).astype(o_ref.dtype)

    @pl.when(chunk_form_ok)
    def _():
        r_i = lax.broadcasted_iota(jnp.int32, (CHUNK, CHUNK), 0)
        c_i = lax.broadcasted_iota(jnp.int32, (CHUNK, CHUNK), 1)
        causal = r_i >= c_i

        for c in range(n_chunks):
            rows = slice(c * CHUNK, (c + 1) * CHUNK)
            b = b_ref[rows, :]
            e_b = jnp.exp(b)
            b_last = b[CHUNK - 1:CHUNK, :]
            e_last = jnp.exp(b_last)
            q_t = q_ref[rows, :] * q_scale * e_b
            k_t = k_ref[rows, :] * jnp.exp(-b)
            k_hat = k_t * e_last
            for h in range(heads):
                ksl = slice(h * dk, (h + 1) * dk)
                vsl = slice(h * dv, (h + 1) * dv)
                qh = q_t[:, ksl].astype(BF16)
                vh = v_ref[rows, vsl].astype(BF16)
                a = lax.dot_general(qh, k_t[:, ksl].astype(BF16), _NT, preferred_element_type=F32)
                a = jnp.where(causal, a, 0.0).astype(BF16)
                st = st_ref[h]
                o_h = jnp.dot(a, vh, preferred_element_type=F32) + lax.dot_general(
                    qh, st.astype(BF16), _NT, preferred_element_type=F32)
                st_ref[h] = st * e_last[:, ksl] + lax.dot_general(
                    vh, k_hat[:, ksl].astype(BF16), _TN, preferred_element_type=F32)
                finish(rows, h, o_h)

    @pl.when(jnp.logical_not(chunk_form_ok))
    def _():
        pair = 2 * SUBLANES

        def group(i, carry):
            for h in range(heads):
                ksl = slice(h * dk, (h + 1) * dk)
                vsl = slice(h * dv, (h + 1) * dv)
                outs = []
                for half in range(2):
                    rows = pl.ds(pl.multiple_of(i * pair + half * SUBLANES, SUBLANES), SUBLANES)
                    o_h, st_new = _exact_group(q_ref[rows, ksl] * q_scale, k_ref[rows, ksl], v_ref[rows, vsl],
                                               g_ref[rows, ksl], st_ref[h], 0, SUBLANES)
                    st_ref[h] = st_new
                    outs.append(o_h)
                finish(pl.ds(pl.multiple_of(i * pair, pair), pair), h, jnp.concatenate(outs, axis=0))
            return carry

        lax.fori_loop(0, tb // pair, group, 0)


def _state_step_edges(st_ref, s_ref):
    t = pl.program_id(1)

    def first():
        @pl.when(t == 0)
        def _():
            st_ref[...] = jnp.zeros_like(st_ref)

    def last():
        @pl.when(t == pl.num_programs(1) - 1)
        def _():
            for h in range(st_ref.shape[0]):
                s_ref[0, h] = st_ref[h].T

    return first, last


def _hgrn_prompt_kernel(q_ref, fa_ref, v_ref, gate_ref, lb_ref, gain_ref, o_ref, s_ref, st_ref, b_ref, g_ref, k_ref,
                        **kw):
    first, last = _state_step_edges(st_ref, s_ref)
    first()
    _recurrence_block(q_ref, k_ref, v_ref, gate_ref, g_ref, gain_ref, o_ref, st_ref, b_ref, fa_ref=fa_ref,
                      lb=_lower_bound(lb_ref[...], HGRN_LAYER), **kw)
    last()


def _hgrn_prompt(pa, lb_logits, gain, *, bsz, seq, heads, dk, dv, tb, q_scale):
    nt = seq // tb
    kw, vw = heads * dk, heads * dv
    col = lambda cb: (lambda b, t: (b * nt + t, cb))
    const = lambda b, t: (0, 0)
    return pl.pallas_call(
        functools.partial(_hgrn_prompt_kernel, heads=heads, dk=dk, dv=dv, tb=tb, q_scale=q_scale),
        grid=(bsz, nt),
        in_specs=[pl.BlockSpec((tb, kw), col(0)), pl.BlockSpec((tb, kw), col(1)), pl.BlockSpec((tb, vw), col(2)),
                  pl.BlockSpec((tb, vw), col(3)), pl.BlockSpec(lb_logits.shape, const),
                  pl.BlockSpec((1, vw), const)],
        out_specs=[pl.BlockSpec((tb, vw), col(0)), pl.BlockSpec((1, heads, dk, dv), lambda b, t: (b, 0, 0, 0))],
        out_shape=[jax.ShapeDtypeStruct((bsz * seq, vw), BF16), jax.ShapeDtypeStruct((bsz, heads, dk, dv), F32)],
        scratch_shapes=[pltpu.VMEM((heads, dv, dk), F32), pltpu.VMEM((tb, kw), F32), pltpu.VMEM((tb, kw), F32),
                        pltpu.VMEM((tb, kw), F32)],
        compiler_params=_cparams(("arbitrary", "arbitrary")),
        name="hgrn_prompt",
    )(pa, pa, pa, pa, lb_logits, gain)


def _layer1_prompt_kernel(x_ref, oa_ref, ob_ref, woe_ref, gn_ref, w_ref, wr_ref, wg_ref, bg_ref, gain_ref, woo_ref,
                          fn_ref, y_ref, s_ref, st_ref, b_ref, proj_ref, lf_ref, o_ref, *, heads, dk, dv, tb,
                          q_scale):
    first, last = _state_step_edges(st_ref, s_ref)
    first()
    kw, vw = heads * dk, heads * dv
    aw = oa_ref.shape[1]
    x1 = (x_ref[...] + jnp.dot(oa_ref[...], woe_ref[0:aw, :], preferred_element_type=F32)
          + jnp.dot(ob_ref[...], woe_ref[aw:, :], preferred_element_type=F32))
    h = _rmsnorm_rows(x1, gn_ref[...]).astype(BF16)
    _proj_cols(h, w_ref, proj_ref)
    r = _dot_wt(h, wr_ref, 0, LANES)
    z = jnp.dot(r.astype(BF16), wg_ref[...], preferred_element_type=F32) + bg_ref[...]
    lf_ref[...] = _log_sigmoid(z) / GLA_GATE_NORMALIZER
    _recurrence_block(proj_ref.at[:, 0:kw], proj_ref.at[:, kw:2 * kw], proj_ref.at[:, 2 * kw:2 * kw + vw],
                      proj_ref.at[:, 2 * kw + vw:2 * kw + 2 * vw], lf_ref, gain_ref, o_ref, st_ref, b_ref,
                      fa_ref=None, lb=None, heads=heads, dk=dk, dv=dv, tb=tb, q_scale=q_scale)
    y = x1 + jnp.dot(o_ref[...], woo_ref[...], preferred_element_type=F32)
    y_ref[...] = _rmsnorm_rows(y, fn_ref[...])
    last()


def _layer1_prompt(x, o_a, o_b, w, *, bsz, seq, heads, dk, dv, tb, q_scale):
    nt = seq // tb
    kw, vw = heads * dk, heads * dv
    row = lambda b, t: (b * nt + t, 0)
    const = lambda b, t: (0, 0)
    full = lambda a: pl.BlockSpec(a.shape, const)
    weights = [w["w_out_even"], w["norm_odd"], w["w_odd"], w["w_r"], w["w_gate"], w["b_gate"], w["gla_gain"],
               w["w_out_odd"], w["final_norm"]]
    return pl.pallas_call(
        functools.partial(_layer1_prompt_kernel, heads=heads, dk=dk, dv=dv, tb=tb, q_scale=q_scale),
        grid=(bsz, nt),
        in_specs=[pl.BlockSpec((tb, D_MODEL), row), pl.BlockSpec((tb, o_a.shape[1]), row),
                  pl.BlockSpec((tb, o_b.shape[1]), row)] + [full(a) for a in weights],
        out_specs=[pl.BlockSpec((tb, D_MODEL), row), pl.BlockSpec((1, heads, dk, dv), lambda b, t: (b, 0, 0, 0))],
        out_shape=[jax.ShapeDtypeStruct((bsz * seq, D_MODEL), F32),
                   jax.ShapeDtypeStruct((bsz, heads, dk, dv), F32)],
        scratch_shapes=[pltpu.VMEM((heads, dv, dk), F32), pltpu.VMEM((tb, kw), F32),
                        pltpu.VMEM((tb, 2 * kw + 2 * vw), F32), pltpu.VMEM((tb, kw), F32),
                        pltpu.VMEM((tb, vw), BF16)],
        compiler_params=_cparams(("arbitrary", "arbitrary")),
        name="layer1_prompt",
    )(x, o_a, o_b, *weights)


def _gla_sample_kernel(*refs, hgrn, heads, dk, dv, t_new, q_scale):
    if hgrn:
        q_ref, fa_ref, v_ref, gate_ref, lb_ref, gain_ref, s0_ref, o_ref, s_ref = refs
    else:
        q_ref, k_ref, v_ref, gate_ref, g_ref, gain_ref, s0_ref, o_ref, s_ref = refs
    for h in range(heads):
        ksl = slice(h * dk, (h + 1) * dk)
        vsl = slice(h * dv, (h + 1) * dv)
        if hgrn:
            g, k = _hgrn_gate(fa_ref[:, ksl], _lower_bound(lb_ref[...], HGRN_LAYER)[:, ksl])
        else:
            g, k = g_ref[:, ksl], k_ref[:, ksl]
        q = q_ref[:, ksl] * q_scale
        v = v_ref[:, vsl]
        o_h = None
        for e in range(SUBLANES // t_new):
            o_e, st_new = _exact_group(q, k, v, g, s0_ref[e, h].T, e * t_new, (e + 1) * t_new)
            s_ref[e, h] = st_new.T
            o_h = o_e if o_h is None else o_h + o_e
        o_ref[:, vsl] = _head_norm_gate(o_h, gain_ref[:, vsl], gate_ref[:, vsl])


def _gla_sample(hgrn, arrays, col_blocks, small, gain, s0, *, t_new, heads, dk, dv, q_scale):
    m = arrays[0].shape[0]
    per = SUBLANES // t_new
    kw, vw = heads * dk, heads * dv
    widths = [kw, kw, vw, vw] + ([] if hgrn else [kw])
    in_specs = [pl.BlockSpec((SUBLANES, w), functools.partial(lambda i, cb: (i, cb), cb=cb))
                for w, cb in zip(widths, col_blocks)]
    operands = list(arrays)
    if hgrn:
        in_specs.append(pl.BlockSpec(small.shape, lambda i: (0, 0)))
        operands.append(small)
    in_specs.append(pl.BlockSpec((1, vw), lambda i: (0, 0)))
    operands.append(gain)
    in_specs.append(pl.BlockSpec((per, heads, dk, dv), lambda i: (i, 0, 0, 0)))
    operands.append(s0)
    return pl.pallas_call(
        functools.partial(_gla_sample_kernel, hgrn=hgrn, heads=heads, dk=dk, dv=dv, t_new=t_new, q_scale=q_scale),
        grid=(m // SUBLANES,),
        in_specs=in_specs,
        out_specs=[pl.BlockSpec((SUBLANES, vw), lambda i: (i, 0)),
                   pl.BlockSpec((per, heads, dk, dv), lambda i: (i, 0, 0, 0))],
        out_shape=[jax.ShapeDtypeStruct((m, vw), F32), jax.ShapeDtypeStruct(s0.shape, F32)],
        compiler_params=_cparams(("arbitrary",)),
        name="hgrn_sample" if hgrn else "gla_sample",
    )(*operands)


FOX_STRIP = 64


def _fox_prompt_kernel(qi_ref, ki_ref, q_ref, k_ref, v_ref, gate_ref, o_ref, m_ref, l_ref, acc_ref, p_ref, a_ref,
                       *, tq, heads, dh):
    p = pl.program_id(1)
    qi = qi_ref[p]
    ki = ki_ref[p]
    ncb = tq // LANES
    aw = 2 * dh

    @pl.when(ki == 0)
    def _():
        m_ref[...] = jnp.full_like(m_ref, -jnp.inf)
        l_ref[...] = jnp.zeros_like(l_ref)
        acc_ref[...] = jnp.zeros_like(acc_ref)

    def step(diag):
        if diag:
            r_i = lax.broadcasted_iota(jnp.int32, (FOX_STRIP, LANES), 0)
            c_i = lax.broadcasted_iota(jnp.int32, (FOX_STRIP, LANES), 1)
        for h in range(heads):
            sl = slice(h * dh, (h + 1) * dh)
            s = lax.dot_general(q_ref[:, h * aw:(h + 1) * aw], k_ref[:, h * aw:(h + 1) * aw], _NT,
                                preferred_element_type=F32)
            for r0 in range(0, tq, FOX_STRIP):
                rows = slice(r0, r0 + FOX_STRIP)
                live = [j for j in range(ncb) if not (diag and j * LANES > r0 + FOX_STRIP - 1)]
                blocks = []
                for j in live:
                    lg = s[rows, j * LANES:(j + 1) * LANES]
                    if diag and (j + 1) * LANES - 1 > r0:
                        lg = jnp.where(r_i + r0 >= c_i + j * LANES, lg, -jnp.inf)
                    blocks.append(lg)
                m_cur = blocks[0]
                for lg in blocks[1:]:
                    m_cur = jnp.maximum(m_cur, lg)
                m_prev = m_ref[h, rows]
                m_new = jnp.maximum(m_prev, jnp.max(m_cur, axis=-1, keepdims=True))
                alpha = jnp.exp2(m_prev - m_new)
                probs = [jnp.exp2(lg - m_new) for lg in blocks]
                row_sum = probs[0]
                for pj in probs[1:]:
                    row_sum = row_sum + pj
                l_ref[h, rows] = alpha * l_ref[h, rows] + jnp.sum(row_sum, axis=-1, keepdims=True)
                m_ref[h, rows] = m_new
                a_ref[rows] = alpha
                for j, pj in zip(live, probs):
                    p_ref[rows, j * LANES:(j + 1) * LANES] = pj.astype(BF16)
                for j in range(ncb):
                    if j not in live:
                        p_ref[rows, j * LANES:(j + 1) * LANES] = jnp.zeros((FOX_STRIP, LANES), BF16)
            acc_ref[:, sl] = a_ref[...] * acc_ref[:, sl] + jnp.dot(p_ref[...], v_ref[:, sl],
                                                                   preferred_element_type=F32)

    @pl.when(ki < qi)
    def _():
        step(False)

    @pl.when(ki == qi)
    def _():
        step(True)
        for h in range(heads):
            sl = slice(h * dh, (h + 1) * dh)
            o_ref[:, sl] = (acc_ref[:, sl] / l_ref[h] * _silu(gate_ref[:, sl])).astype(o_ref.dtype)


def _fox_prompt(qa, ka, vb, gate, *, bsz, seq, tq):
    nq = seq // tq
    pairs = [(qi, ki) for qi in range(nq) for ki in range(qi + 1)]
    qi_tab = jnp.asarray(np.array([p[0] for p in pairs], np.int32))
    ki_tab = jnp.asarray(np.array([p[1] for p in pairs], np.int32))
    w = B_WIDTH
    q_map = lambda b, p, qt, kt: (b * nq + qt[p], 0)
    k_map = lambda b, p, qt, kt: (b * nq + kt[p], 0)
    grid_spec = pltpu.PrefetchScalarGridSpec(
        num_scalar_prefetch=2,
        grid=(bsz, len(pairs)),
        in_specs=[pl.BlockSpec((tq, 2 * w), q_map), pl.BlockSpec((tq, 2 * w), k_map),
                  pl.BlockSpec((tq, w), k_map), pl.BlockSpec((tq, w), q_map)],
        out_specs=pl.BlockSpec((tq, w), q_map),
        scratch_shapes=[pltpu.VMEM((B_HEADS, tq, LANES), F32), pltpu.VMEM((B_HEADS, tq, LANES), F32),
                        pltpu.VMEM((tq, w), F32), pltpu.VMEM((tq, tq), BF16), pltpu.VMEM((tq, LANES), F32)],
    )
    return pl.pallas_call(
        functools.partial(_fox_prompt_kernel, tq=tq, heads=B_HEADS, dh=B_HEAD_DIM),
        grid_spec=grid_spec,
        out_shape=jax.ShapeDtypeStruct((bsz * seq, w), BF16),
        compiler_params=_cparams(("arbitrary", "arbitrary")),
        name="fox_prompt",
    )(qi_tab, ki_tab, qa, ka, vb, gate)


def _page_suffix_kernel(lf_ref, later_ref, total_ref, out_ref):
    lf = lf_ref[...]
    pw = lf.shape[1]
    out_ref[:, 0:pw] = _x_dot01(lf, later_ref[...])
    out_ref[:, pw:2 * pw] = _x_dot01(lf, total_ref[...])


def _page_suffix(lf_pages, heads):
    n_pool, pw = lf_pages.shape
    tm = 512 if n_pool % 512 == 0 else n_pool
    j = np.arange(pw)
    in_head, in_tok = j[:, None] // (pw // heads), j[:, None] % (pw // heads)
    out_head, out_tok = j[None, :] % heads, j[None, :] // heads
    head_eq = in_head == out_head
    later = jnp.asarray(head_eq & (in_tok > out_tok), BF16)
    total = jnp.asarray(head_eq, BF16)
    row = lambda i: (i, 0)
    const = lambda i: (0, 0)
    return pl.pallas_call(
        _page_suffix_kernel,
        grid=(n_pool // tm,),
        in_specs=[pl.BlockSpec((tm, pw), row), pl.BlockSpec((pw, pw), const), pl.BlockSpec((pw, pw), const)],
        out_specs=pl.BlockSpec((tm, 2 * pw), row),
        out_shape=jax.ShapeDtypeStruct((n_pool, 2 * pw), F32),
        compiler_params=_cparams(("arbitrary",)),
        name="page_suffix",
    )(lf_pages, later, total)


def _fox_sample_kernel(pt_ref, k_hbm, v_hbm, sfx_hbm, q_ref, kn_ref, vn_ref, gate_ref, cn_col_ref, cn_row_ref, o_ref,
                       kbuf, vbuf, sbuf, sem, m_ref, l_ref, acc_ref, carry_ref, *, pages_per_step, heads, dh, t_new,
                       n_pages):
    pp = pages_per_step
    b = pl.program_id(0)
    j = pl.program_id(1)
    nj = pl.num_programs(1)
    nrow = t_new * heads
    pw = PAGE_SIZE * heads
    step = b * nj + j
    slot = lax.rem(step, 2)

    def page_copies(bb, jj, sl):
        copies = []
        for i in range(pp):
            page = pt_ref[bb, n_pages - 1 - (jj * pp + i)]
            for kind, (src, dst) in enumerate(((k_hbm, kbuf), (v_hbm, vbuf), (sfx_hbm, sbuf))):
                copies.append(pltpu.make_async_copy(src.at[page], dst.at[sl, i], sem.at[kind, sl]))
        return copies

    @pl.when(step == 0)
    def _():
        for cp in page_copies(b, j, slot):
            cp.start()

    @pl.when(step + 1 < pl.num_programs(0) * nj)
    def _():
        wrap = j + 1 == nj
        for cp in page_copies(jnp.where(wrap, b + 1, b), jnp.where(wrap, 0, j + 1), 1 - slot):
            cp.start()

    for cp in page_copies(b, j, slot):
        cp.wait()
    k_refs = [kbuf.at[slot, i] for i in range(pp)]
    v_refs = [vbuf.at[slot, i] for i in range(pp)]
    sfx_refs = [sbuf.at[slot, i] for i in range(pp)]

    @pl.when(j == 0)
    def _():
        m_ref[...] = jnp.full_like(m_ref, -jnp.inf)
        l_ref[...] = jnp.zeros_like(l_ref)
        acc_ref[...] = jnp.zeros_like(acc_ref)
        carry_ref[...] = jnp.zeros_like(carry_ref)

    q = q_ref[0]
    cn_col = cn_col_ref[0] * LOG2E
    row_head = lax.broadcasted_iota(jnp.int32, (nrow, pw), 0) % heads
    col_head = lax.broadcasted_iota(jnp.int32, (nrow, pw), 1) % heads
    same_head = row_head == col_head

    def online(logit_list, v_list):
        m_cur = logit_list[0]
        for lg in logit_list[1:]:
            m_cur = jnp.maximum(m_cur, lg)
        m_prev = m_ref[...]
        m_new = jnp.maximum(m_prev, jnp.max(m_cur, axis=-1, keepdims=True))
        alpha = jnp.exp2(m_prev - m_new)
        probs = [jnp.exp2(lg - m_new) for lg in logit_list]
        row_sum = probs[0]
        for pj in probs[1:]:
            row_sum = row_sum + pj
        l_ref[...] = alpha * l_ref[...] + jnp.sum(row_sum, axis=-1, keepdims=True)
        pv = None
        for pj, v in zip(probs, v_list):
            t = jnp.dot(pj.astype(BF16), v.astype(BF16), preferred_element_type=F32)
            pv = t if pv is None else pv + t
        acc_ref[...] = alpha * acc_ref[...] + pv
        m_ref[...] = m_new

    logit_list = []
    carry = carry_ref[...]
    for i in range(pp):
        suffix = (carry + sfx_refs[i][:, 0:pw]) * LOG2E
        carry = carry + sfx_refs[i][:, pw:2 * pw]
        s = lax.dot_general(q, k_refs[i][...].astype(BF16), _NT, preferred_element_type=F32)
        logit_list.append(jnp.where(same_head, s + cn_col + suffix, -jnp.inf))
    carry_ref[...] = carry
    online(logit_list, [v_refs[i][...] for i in range(pp)])

    @pl.when(j == pl.num_programs(1) - 1)
    def _():
        s = lax.dot_general(q, kn_ref[0].astype(BF16), _NT, preferred_element_type=F32)
        logits = s + cn_col - cn_row_ref[0] * LOG2E
        r = lax.broadcasted_iota(jnp.int32, (nrow, nrow), 0)
        c = lax.broadcasted_iota(jnp.int32, (nrow, nrow), 1)
        keep = jnp.logical_and((r % heads) == (c % heads), (r // heads) >= (c // heads))
        online([jnp.where(keep, logits, -jnp.inf)], [vn_ref[0]])
        o_ref[0] = acc_ref[...] / l_ref[...] * _silu(gate_ref[0])


def _fox_sample(page_table, k_pages, v_pages, lf_pages, q, k_new, v_new, gate, cn_col, cn_row, *, pages_per_step):
    n_pool = lf_pages.shape[0]
    sfx = _page_suffix(lf_pages, B_HEADS).reshape(n_pool, 1, -1)
    db, n_pages = page_table.shape
    pp = pages_per_step
    nrow = q.shape[1]
    pw = PAGE_SIZE * B_HEADS
    dh = B_HEAD_DIM
    per_b = lambda b, j, pt: (b, 0, 0)
    hbm = pl.BlockSpec(memory_space=pl.ANY)
    in_specs = ([hbm, hbm, hbm] + [pl.BlockSpec((1, nrow, dh), per_b)] * 4
                + [pl.BlockSpec((1, nrow, 1), per_b), pl.BlockSpec((1, 1, nrow), per_b)])
    n_slots = 2
    grid_spec = pltpu.PrefetchScalarGridSpec(
        num_scalar_prefetch=1,
        grid=(db, n_pages // pp),
        in_specs=in_specs,
        out_specs=pl.BlockSpec((1, nrow, dh), per_b),
        scratch_shapes=[pltpu.VMEM((n_slots, pp, pw, dh), F32), pltpu.VMEM((n_slots, pp, pw, dh), F32),
                        pltpu.VMEM((n_slots, pp, 1, 2 * pw), F32), pltpu.SemaphoreType.DMA((3, n_slots)),
                        pltpu.VMEM((nrow, 1), F32), pltpu.VMEM((nrow, 1), F32), pltpu.VMEM((nrow, dh), F32),
                        pltpu.VMEM((1, pw), F32)],
    )
    return pl.pallas_call(
        functools.partial(_fox_sample_kernel, pages_per_step=pp, heads=B_HEADS, dh=dh, t_new=nrow // B_HEADS,
                          n_pages=n_pages),
        grid_spec=grid_spec,
        out_shape=jax.ShapeDtypeStruct((db, nrow, dh), F32),
        compiler_params=_cparams(("arbitrary", "arbitrary")),
        name="fox_sample",
    )(page_table, k_pages, v_pages, sfx, q, k_new, v_new, gate, cn_col, cn_row)


def _out_proj_kernel(*refs, n_in, final):
    ins = refs[:n_in]
    w_ref, x_ref = refs[n_in], refs[n_in + 1]
    y = x_ref[...]
    k0 = 0
    for a_ref in ins:
        kw = a_ref.shape[1]
        y = y + jnp.dot(a_ref[...].astype(BF16), w_ref[k0:k0 + kw, :], preferred_element_type=F32)
        k0 += kw
    if final:
        g_ref, o_ref = refs[n_in + 2], refs[n_in + 3]
        o_ref[...] = _rmsnorm_rows(y, g_ref[...])
    else:
        refs[n_in + 2][...] = y


def _out_proj(ins, w, x, final_gain, *, tm):
    m = x.shape[0]
    const = lambda i: (0, 0)
    row = lambda i: (i, 0)
    in_specs = [pl.BlockSpec((tm, a.shape[1]), row) for a in ins]
    in_specs += [pl.BlockSpec(w.shape, const), pl.BlockSpec((tm, D_MODEL), row)]
    operands = list(ins) + [w, x]
    if final_gain is not None:
        in_specs.append(pl.BlockSpec((1, D_MODEL), const))
        operands.append(final_gain)
    return pl.pallas_call(
        functools.partial(_out_proj_kernel, n_in=len(ins), final=final_gain is not None),
        grid=(m // tm,),
        in_specs=in_specs,
        out_specs=pl.BlockSpec((tm, D_MODEL), row),
        out_shape=jax.ShapeDtypeStruct((m, D_MODEL), F32),
        compiler_params=_cparams(("arbitrary",)),
        name="out_proj_final" if final_gain is not None else "out_proj",
    )(*operands)


def _pad_cols(w, n):
    return jnp.pad(w, ((0, 0), (0, n - w.shape[1])))


def _prep_weights(weights):
    (norm_even, w_in_even, b_fox_f, lb_logits, hgrn_gain, w_out_even, norm_odd, w_in_odd, w_gla_gate, b_gla_gate,
     gla_gain, w_out_odd, final_norm) = weights
    n_even = 4 * A_WIDTH + 4 * B_WIDTH
    n_odd = 2 * C_KEY_WIDTH + 2 * C_VAL_WIDTH
    w_even_t = w_in_even[0].T.astype(BF16)
    w_odd_t = w_in_odd[0].T.astype(BF16)
    pad_rows = lambda a: jnp.pad(a, ((0, LANES - a.shape[0]), (0, 0)))
    return dict(
        norm_even=norm_even[0].reshape(1, D_MODEL),
        w_even=w_even_t[:n_even],
        w_fb=pad_rows(w_even_t[n_even:]),
        b_fox=_pad_cols(b_fox_f[0].reshape(1, B_HEADS), LANES),
        lb_logits=lb_logits,
        hgrn_gain=hgrn_gain[0].reshape(1, A_WIDTH),
        w_out_even=w_out_even[0].astype(BF16),
        norm_odd=norm_odd[0].reshape(1, D_MODEL),
        w_odd=w_odd_t[:n_odd],
        w_r=pad_rows(w_odd_t[n_odd:]),
        w_gate=jnp.pad(w_gla_gate[0], ((0, LANES - C_GATE_RANK), (0, 0))).astype(BF16),
        b_gate=b_gla_gate[0].reshape(1, C_KEY_WIDTH),
        gla_gain=gla_gain[0].reshape(1, C_VAL_WIDTH),
        w_out_odd=w_out_odd[0].astype(BF16),
        final_norm=final_norm.reshape(1, D_MODEL),
    )


def _trunk(x, w, *, bsz, seq, prompt, sample_ctx):
    m = bsz * seq
    tm = 512 if m % 512 == 0 else m
    x2 = x.reshape(m, D_MODEL)
    hg = dict(heads=A_HEADS, dk=A_HEAD_DIM, dv=A_HEAD_DIM, q_scale=1.0)
    gl = dict(heads=C_HEADS, dk=C_KEY_DIM, dv=C_VAL_DIM, q_scale=C_KEY_DIM ** -0.5)

    pa, gate_b, qa, ka, vb16, k_rows, v_rows, lf, c = _even_proj(x2, w["norm_even"], w["w_even"], w["w_fb"],
                                                                 w["b_fox"], seq=seq, tm=tm)
    kb = k_rows.reshape(bsz, seq, B_HEADS, B_HEAD_DIM)
    vb = v_rows.reshape(bsz, seq, B_HEADS, B_HEAD_DIM)
    lfb = lf[:, :B_HEADS].reshape(bsz, seq, B_HEADS)

    if prompt:
        o_a, s_a = _hgrn_prompt(pa, w["lb_logits"], w["hgrn_gain"], bsz=bsz, seq=seq, tb=512, **hg)
        o_b = _fox_prompt(qa, ka, vb16, gate_b, bsz=bsz, seq=seq, tq=min(seq, 1024))
        y, s_c = _layer1_prompt(x2, o_a, o_b, w, bsz=bsz, seq=seq, tb=512, **gl)
        return y.reshape(bsz, seq, D_MODEL), kb, vb, lfb, s_a, s_c

    o_a, s_a = _gla_sample(True, [pa] * 4, [0, 1, 2, 3], w["lb_logits"], w["hgrn_gain"], sample_ctx["state_hgrn"],
                           t_new=seq, **hg)
    nrow = seq * B_HEADS
    rows = lambda a: a.reshape(bsz, nrow, B_HEAD_DIM)
    cn = c[:, :B_HEADS].reshape(bsz, nrow)
    q_rows = qa.reshape(bsz, nrow, 2 * B_HEAD_DIM)[:, :, :B_HEAD_DIM]
    o_b = _fox_sample(sample_ctx["page_table"], sample_ctx["k_pages"], sample_ctx["v_pages"],
                      sample_ctx["lf_pages"], q_rows, rows(k_rows), rows(v_rows), rows(gate_b),
                      cn.reshape(bsz, nrow, 1), cn.reshape(bsz, 1, nrow), pages_per_step=8)
    x1 = _out_proj([o_a, o_b.reshape(m, B_WIDTH)], w["w_out_even"], x2, None, tm=tm)
    proj1, lf1 = _odd_proj(x1, w["norm_odd"], w["w_odd"], w["w_r"], w["w_gate"], w["b_gate"], tm=tm)
    o_c, s_c = _gla_sample(False, [proj1, proj1, proj1, proj1, lf1], [0, 1, 1, 2, 0], None, w["gla_gain"],
                           sample_ctx["state_gla"], t_new=seq, **gl)
    y = _out_proj([o_c], w["w_out_odd"], x1, w["final_norm"], tm=tm)
    return y.reshape(bsz, seq, D_MODEL), kb, vb, lfb, s_a, s_c


def kernel(x_prompt, x_sample, cache_fox_k, cache_fox_v, cache_fox_logf, state_hgrn, state_gla, page_table,
           norm_even, w_in_even, b_fox_f, lb_logits, hgrn_gain, w_out_even, norm_odd, w_in_odd, w_gla_gate,
           b_gla_gate, gla_gain, w_out_odd, final_norm):
    weights = _prep_weights((norm_even, w_in_even, b_fox_f, lb_logits, hgrn_gain, w_out_even, norm_odd, w_in_odd,
                             w_gla_gate, b_gla_gate, gla_gain, w_out_odd, final_norm))
    bsz, seq, _ = x_prompt.shape
    y_p, kp, vp, lfp, hgrn_p, gla_p = _trunk(x_prompt, weights, bsz=bsz, seq=seq, prompt=True, sample_ctx=None)
    n_pp = seq // PAGE_SIZE
    fox_k_prompt = kp.reshape(1, bsz, n_pp, PAGE_SIZE, B_HEADS, B_HEAD_DIM)
    fox_v_prompt = vp.reshape(1, bsz, n_pp, PAGE_SIZE, B_HEADS, B_HEAD_DIM)
    fox_logf_prompt = lfp.reshape(1, bsz, n_pp, PAGE_SIZE, B_HEADS)

    db, t_new, _ = x_sample.shape
    n_pool = cache_fox_k.shape[1]
    pw = PAGE_SIZE * B_HEADS
    ctx = dict(
        page_table=page_table,
        k_pages=cache_fox_k[0].reshape(n_pool, pw, B_HEAD_DIM),
        v_pages=cache_fox_v[0].reshape(n_pool, pw, B_HEAD_DIM),
        lf_pages=cache_fox_logf[0].transpose(0, 2, 1).reshape(n_pool, pw),
        state_hgrn=state_hgrn[0], state_gla=state_gla[0])
    y_s, ks, vs, lfs, hgrn_s, gla_s = _trunk(x_sample, weights, bsz=db, seq=t_new, prompt=False, sample_ctx=ctx)
    return (y_p, y_s, fox_k_prompt, fox_v_prompt, fox_logf_prompt, hgrn_p[None], gla_p[None],
            ks[None], vs[None], lfs[None], hgrn_s[None], gla_s[None])
```

```python
import functools

import numpy as np
import jax
import jax.numpy as jnp
from jax import lax
from jax.experimental import pallas as pl
from jax.experimental.pallas import tpu as pltpu

F32 = jnp.float32
BF16 = jnp.bfloat16

D_MODEL = 1024
PAGE_SIZE = 128
A_HEADS = 4
A_HEAD_DIM = 128
A_WIDTH = 512
B_HEADS = 4
B_HEAD_DIM = 128
B_WIDTH = 512
C_HEADS = 4
C_KEY_WIDTH = 512
C_VAL_WIDTH = 1024
C_KEY_DIM = 128
C_VAL_DIM = 256
C_GATE_RANK = 16
GLA_GATE_NORMALIZER = 16.0
EPS = 1e-6
HGRN_LAYER = 0
LOG2E = 1.4426950408889634
AUG_PIECES = 3

LANES = 128
SUBLANES = 8
VMEM_LIMIT = 56 * 1024 * 1024
CHUNK = 64
MAX_CHUNK_LOG_DECAY = 60.0

_NT = (((1,), (1,)), ((), ()))
_TN = (((0,), (0,)), ((), ()))


def _cparams(sem):
    return pltpu.CompilerParams(dimension_semantics=sem, vmem_limit_bytes=VMEM_LIMIT)


def _sigmoid(x):
    return 1.0 / (1.0 + jnp.exp(-x))


def _log_sigmoid(x):
    return jnp.minimum(x, 0.0) - jnp.log1p(jnp.exp(-jnp.abs(x)))


def _silu(x):
    return x * _sigmoid(x)


def _rmsnorm_rows(x, g):
    return x * lax.rsqrt(jnp.mean(x * x, axis=-1, keepdims=True) + EPS) * g


def _split3(x):
    p1 = x.astype(BF16)
    r1 = x - p1.astype(F32)
    p2 = r1.astype(BF16)
    p3 = (r1 - p2.astype(F32)).astype(BF16)
    return p1, p2, p3


def _dot01(m01, x):
    acc = None
    for p in _split3(x):
        t = jnp.dot(m01, p, preferred_element_type=F32)
        acc = t if acc is None else acc + t
    return acc


def _x_dot01(x, m01):
    acc = None
    for p in _split3(x):
        t = jnp.dot(p, m01, preferred_element_type=F32)
        acc = t if acc is None else acc + t
    return acc


def _lower_tri(n, seq):
    r = lax.broadcasted_iota(jnp.int32, (n, n), 0)
    c = lax.broadcasted_iota(jnp.int32, (n, n), 1)
    keep = r >= c
    if seq < n:
        keep = jnp.logical_and(keep, (r // seq) == (c // seq))
    return jnp.where(keep, 1.0, 0.0).astype(BF16)


def _dot_wt(h, wt_ref, c0, n):
    return lax.dot_general(h, wt_ref[c0:c0 + n, :], _NT, preferred_element_type=F32)


def _proj_cols(h, wt_ref, out_ref):
    step = 512
    for c0 in range(0, wt_ref.shape[0], step):
        out_ref[:, c0:c0 + step] = _dot_wt(h, wt_ref, c0, step)


def _aug_selectors():
    sel_q = np.zeros((LANES, B_WIDTH), np.float32)
    sel_k = np.zeros((LANES, B_WIDTH), np.float32)
    for p in range(AUG_PIECES):
        for hd in range(B_HEADS):
            sel_q[p * B_HEADS + hd, hd * B_HEAD_DIM + p] = 1.0
            sel_k[p * B_HEADS + hd, hd * B_HEAD_DIM + AUG_PIECES + p] = -1.0
    return jnp.asarray(sel_q, BF16), jnp.asarray(sel_k, BF16)


def _even_proj_kernel(x_ref, g_ref, w_ref, wfb_ref, bfox_ref, selq_ref, selk_ref, pa_ref, gate_ref, qa_ref, ka_ref,
                      vb_ref, kout_ref, vout_ref, lf_ref, c_ref, carry_ref, *, tm, seq):
    i = pl.program_id(0)
    h = _rmsnorm_rows(x_ref[...], g_ref[...]).astype(BF16)
    na = 4 * A_WIDTH
    bw = B_WIDTH
    dh = B_HEAD_DIM
    for c0 in range(0, na, 512):
        pa_ref[:, c0:c0 + 512] = _dot_wt(h, w_ref, c0, 512)
    q = (_dot_wt(h, w_ref, na, bw) * (dh ** -0.5 * LOG2E)).astype(BF16)
    for hd in range(B_HEADS):
        qa_ref[:, 2 * hd * dh:(2 * hd + 1) * dh] = q[:, hd * dh:(hd + 1) * dh]
    for j, out_ref in ((1, kout_ref), (2, vout_ref)):
        kv = _dot_wt(h, w_ref, na + j * bw, bw)
        kv16 = kv.astype(BF16)
        if j == 1:
            for hd in range(B_HEADS):
                ka_ref[:, 2 * hd * dh:(2 * hd + 1) * dh] = kv16[:, hd * dh:(hd + 1) * dh]
        else:
            vb_ref[...] = kv16
        for hd in range(B_HEADS):
            out_ref[pl.ds(hd, tm, stride=B_HEADS), :] = kv[:, hd * dh:(hd + 1) * dh]
    gate_ref[...] = _dot_wt(h, w_ref, na + 3 * bw, bw)
    fb = _dot_wt(h, wfb_ref, 0, LANES) + bfox_ref[...]
    lane = lax.broadcasted_iota(jnp.int32, fb.shape, 1)
    lf = jnp.where(lane < B_HEADS, _log_sigmoid(fb), 0.0)
    lf_ref[...] = lf
    cs = _dot01(_lower_tri(tm, seq), lf)
    if seq > tm:
        @pl.when((i * tm) % seq == 0)
        def _():
            carry_ref[...] = jnp.zeros_like(carry_ref)
        cs = cs + carry_ref[0:1, :]
        carry_ref[0:1, :] = cs[tm - 1:tm, :]
    c_ref[...] = cs
    pos = lax.broadcasted_iota(jnp.int32, (1, bw), 1) % dh
    aug_q = jnp.where(jnp.logical_and(pos >= AUG_PIECES, pos < 2 * AUG_PIECES), 1.0, 0.0)
    aug_k = jnp.where(pos < AUG_PIECES, 1.0, 0.0)
    packed = None
    for p, piece in enumerate(_split3(cs * LOG2E)):
        shifted = piece.astype(F32) if p == 0 else pltpu.roll(piece.astype(F32), p * B_HEADS, 1)
        packed = shifted if packed is None else packed + shifted
    packed = packed.astype(BF16)
    aug_q = (aug_q + jnp.dot(packed, selq_ref[...], preferred_element_type=F32)).astype(BF16)
    aug_k = (aug_k + jnp.dot(packed, selk_ref[...], preferred_element_type=F32)).astype(BF16)
    for hd in range(B_HEADS):
        qa_ref[:, (2 * hd + 1) * dh:(2 * hd + 2) * dh] = aug_q[:, hd * dh:(hd + 1) * dh]
        ka_ref[:, (2 * hd + 1) * dh:(2 * hd + 2) * dh] = aug_k[:, hd * dh:(hd + 1) * dh]


def _even_proj(x, g, w_main, w_fb, b_fox, *, seq, tm):
    m = x.shape[0]
    const = lambda i: (0, 0)
    row = lambda i: (i, 0)
    sel_q, sel_k = _aug_selectors()
    return pl.pallas_call(
        functools.partial(_even_proj_kernel, tm=tm, seq=seq),
        grid=(m // tm,),
        in_specs=[pl.BlockSpec((tm, D_MODEL), row), pl.BlockSpec((1, D_MODEL), const),
                  pl.BlockSpec(w_main.shape, const), pl.BlockSpec(w_fb.shape, const),
                  pl.BlockSpec((1, LANES), const), pl.BlockSpec(sel_q.shape, const),
                  pl.BlockSpec(sel_k.shape, const)],
        out_specs=[pl.BlockSpec((tm, 4 * A_WIDTH), row), pl.BlockSpec((tm, B_WIDTH), row),
                   pl.BlockSpec((tm, 2 * B_WIDTH), row), pl.BlockSpec((tm, 2 * B_WIDTH), row),
                   pl.BlockSpec((tm, B_WIDTH), row), pl.BlockSpec((tm * B_HEADS, B_HEAD_DIM), row),
                   pl.BlockSpec((tm * B_HEADS, B_HEAD_DIM), row), pl.BlockSpec((tm, LANES), row),
                   pl.BlockSpec((tm, LANES), row)],
        out_shape=[jax.ShapeDtypeStruct((m, 4 * A_WIDTH), F32), jax.ShapeDtypeStruct((m, B_WIDTH), F32),
                   jax.ShapeDtypeStruct((m, 2 * B_WIDTH), BF16), jax.ShapeDtypeStruct((m, 2 * B_WIDTH), BF16),
                   jax.ShapeDtypeStruct((m, B_WIDTH), BF16),
                   jax.ShapeDtypeStruct((m * B_HEADS, B_HEAD_DIM), F32),
                   jax.ShapeDtypeStruct((m * B_HEADS, B_HEAD_DIM), F32),
                   jax.ShapeDtypeStruct((m, LANES), F32), jax.ShapeDtypeStruct((m, LANES), F32)],
        scratch_shapes=[pltpu.VMEM((SUBLANES, LANES), F32)],
        compiler_params=_cparams(("arbitrary",)),
        name="even_proj",
    )(x, g, w_main, w_fb, b_fox, sel_q, sel_k)


def _odd_proj_kernel(x_ref, g_ref, w_ref, wr_ref, wg_ref, bg_ref, proj_ref, lf_ref):
    h = _rmsnorm_rows(x_ref[...], g_ref[...]).astype(BF16)
    _proj_cols(h, w_ref, proj_ref)
    r = _dot_wt(h, wr_ref, 0, LANES)
    z = jnp.dot(r.astype(BF16), wg_ref[...], preferred_element_type=F32) + bg_ref[...]
    lf_ref[...] = _log_sigmoid(z) / GLA_GATE_NORMALIZER


def _odd_proj(x, g, w_main, w_r, w_gate, b_gate, *, tm):
    m = x.shape[0]
    n = w_main.shape[0]
    const = lambda i: (0, 0)
    row = lambda i: (i, 0)
    return pl.pallas_call(
        _odd_proj_kernel,
        grid=(m // tm,),
        in_specs=[pl.BlockSpec((tm, D_MODEL), row), pl.BlockSpec((1, D_MODEL), const),
                  pl.BlockSpec(w_main.shape, const), pl.BlockSpec(w_r.shape, const),
                  pl.BlockSpec((LANES, C_KEY_WIDTH), const), pl.BlockSpec((1, C_KEY_WIDTH), const)],
        out_specs=[pl.BlockSpec((tm, n), row), pl.BlockSpec((tm, C_KEY_WIDTH), row)],
        out_shape=[jax.ShapeDtypeStruct((m, n), F32), jax.ShapeDtypeStruct((m, C_KEY_WIDTH), F32)],
        compiler_params=_cparams(("arbitrary",)),
        name="odd_proj",
    )(x, g, w_main, w_r, w_gate, b_gate)


def _lower_bound(logits, layer):
    e = jnp.exp(logits - jnp.max(logits, axis=0, keepdims=True))
    return jnp.sum(e[:layer + 1, :], axis=0, keepdims=True) / jnp.sum(e, axis=0, keepdims=True)


def _hgrn_gate(fa, lb):
    f = lb + (1.0 - lb) * _sigmoid(fa)
    return jnp.log(f), 1.0 - f


def _exact_group(q, k, v, g, st, lo, hi):
    n = SUBLANES
    row = lax.broadcasted_iota(jnp.int32, (n, 1), 0)
    valid = jnp.logical_and(row >= lo, row < hi)
    q = jnp.where(valid, q, 0.0)
    k = jnp.where(valid, k, 0.0)
    g = jnp.where(valid, g, 0.0)
    b = g
    for sh in (1, 2, 4):
        b = b + jnp.where(row >= sh, pltpu.roll(b, sh, 0), 0.0)
    o = lax.dot_general((q * jnp.exp(b)).astype(BF16), st.astype(BF16), _NT, preferred_element_type=F32)
    for s in range(lo, hi):
        w = jnp.exp(jnp.minimum(b - b[s:s + 1, :], 0.0))
        a = jnp.sum(q * k[s:s + 1, :] * w, axis=-1, keepdims=True)
        o = o + jnp.where(row >= s, a, 0.0) * v[s:s + 1, :]
    b_last = b[n - 1:n, :]
    k_hat = k * jnp.exp(b_last - b)
    st_new = st * jnp.exp(b_last) + lax.dot_general(v.astype(BF16), k_hat.astype(BF16), _TN,
                                                    preferred_element_type=F32)
    return o, st_new


def _head_norm_gate(o, gain, gate):
    y = o * lax.rsqrt(jnp.mean(o * o, axis=-1, keepdims=True) + EPS) * gain
    return y * _silu(gate)


def _recurrence_block(q_ref, k_ref, v_ref, gate_ref, g_ref, gain_ref, o_ref, st_ref, b_ref, *, fa_ref, lb, heads, dk,
                      dv, tb, q_scale):
    n_chunks = tb // CHUNK

    tri = _lower_tri(CHUNK, CHUNK)
    b_min = None
    for c in range(n_chunks):
        rows = slice(c * CHUNK, (c + 1) * CHUNK)
        if fa_ref is not None:
            g, k = _hgrn_gate(fa_ref[rows, :], lb)
            g_ref[rows, :] = g
            k_ref[rows, :] = k
        else:
            g = g_ref[rows, :]
        b = _dot01(tri, g)
        b_ref[rows, :] = b
        b_last = b[CHUNK - 1:CHUNK, :]
        b_min = b_last if b_min is None else jnp.minimum(b_min, b_last)
    chunk_form_ok = jnp.min(b_min) >= -MAX_CHUNK_LOG_DECAY

    def finish(rows, h, o_h):
        vsl = slice(h * dv, (h + 1) * dv)
        o_ref[rows, vsl] = _head_norm_gate(o_h, gain_ref[:, vsl], gate_ref[rows, vsl]).astype(o_ref.dtype)

    @pl.when(chunk_form_ok)
    def _():
        r_i = lax.broadcasted_iota(jnp.int32, (CHUNK, CHUNK), 0)
        c_i = lax.broadcasted_iota(jnp.int32, (CHUNK, CHUNK), 1)
        causal = r_i >= c_i

        for c in range(n_chunks):
            rows = slice(c * CHUNK, (c + 1) * CHUNK)
            b = b_ref[rows, :]
            e_b = jnp.exp(b)
            b_last = b[CHUNK - 1:CHUNK, :]
            e_last = jnp.exp(b_last)
            q_t = q_ref[rows, :] * q_scale * e_b
            k_t = k_ref[rows, :] * jnp.exp(-b)
            k_hat = k_t * e_last
            for h in range(heads):
                ksl = slice(h * dk, (h + 1) * dk)
                vsl = slice(h * dv, (h + 1) * dv)
                qh = q_t[:, ksl].astype(BF16)
                vh = v_ref[rows, vsl].astype(BF16)
                a = lax.dot_general(qh, k_t[:, ksl].astype(BF16), _NT, preferred_element_type=F32)
                a = jnp.where(causal, a, 0.0).astype(BF16)
                st = st_ref[h]
                o_h = jnp.dot(a, vh, preferred_element_type=F32) + lax.dot_general(
                    qh, st.astype(BF16), _NT, preferred_element_type=F32)
                st_ref[h] = st * e_last[:, ksl] + lax.dot_general(
                    vh, k_hat[:, ksl].astype(BF16), _TN, preferred_element_type=F32)
                finish(rows, h, o_h)

    @pl.when(jnp.logical_not(chunk_form_ok))
    def _():
        pair = 2 * SUBLANES

        def group(i, carry):
            for h in range(heads):
                ksl = slice(h * dk, (h + 1) * dk)
                vsl = slice(h * dv, (h + 1) * dv)
                outs = []
                for half in range(2):
                    rows = pl.ds(pl.multiple_of(i * pair + half * SUBLANES, SUBLANES), SUBLANES)
                    o_h, st_new = _exact_group(q_ref[rows, ksl] * q_scale, k_ref[rows, ksl], v_ref[rows, vsl],
                                               g_ref[rows, ksl], st_ref[h], 0, SUBLANES)
                    st_ref[h] = st_new
                    outs.append(o_h)
                finish(pl.ds(pl.multiple_of(i * pair, pair), pair), h, jnp.concatenate(outs, axis=0))
            return carry

        lax.fori_loop(0, tb // pair, group, 0)


def _state_step_edges(st_ref, s_ref):
    t = pl.program_id(1)

    def first():
        @pl.when(t == 0)
        def _():
            st_ref[...] = jnp.zeros_like(st_ref)

    def last():
        @pl.when(t == pl.num_programs(1) - 1)
        def _():
            for h in range(st_ref.shape[0]):
                s_ref[0, h] = st_ref[h].T

    return first, last


def _hgrn_prompt_kernel(q_ref, fa_ref, v_ref, gate_ref, lb_ref, gain_ref, o_ref, s_ref, st_ref, b_ref, g_ref, k_ref,
                        **kw):
    first, last = _state_step_edges(st_ref, s_ref)
    first()
    _recurrence_block(q_ref, k_ref, v_ref, gate_ref, g_ref, gain_ref, o_ref, st_ref, b_ref, fa_ref=fa_ref,
                      lb=_lower_bound(lb_ref[...], HGRN_LAYER), **kw)
    last()


def _hgrn_prompt(pa, lb_logits, gain, *, bsz, seq, heads, dk, dv, tb, q_scale):
    nt = seq // tb
    kw, vw = heads * dk, heads * dv
    col = lambda cb: (lambda b, t: (b * nt + t, cb))
    const = lambda b, t: (0, 0)
    return pl.pallas_call(
        functools.partial(_hgrn_prompt_kernel, heads=heads, dk=dk, dv=dv, tb=tb, q_scale=q_scale),
        grid=(bsz, nt),
        in_specs=[pl.BlockSpec((tb, kw), col(0)), pl.BlockSpec((tb, kw), col(1)), pl.BlockSpec((tb, vw), col(2)),
                  pl.BlockSpec((tb, vw), col(3)), pl.BlockSpec(lb_logits.shape, const),
                  pl.BlockSpec((1, vw), const)],
        out_specs=[pl.BlockSpec((tb, vw), col(0)), pl.BlockSpec((1, heads, dk, dv), lambda b, t: (b, 0, 0, 0))],
        out_shape=[jax.ShapeDtypeStruct((bsz * seq, vw), BF16), jax.ShapeDtypeStruct((bsz, heads, dk, dv), F32)],
        scratch_shapes=[pltpu.VMEM((heads, dv, dk), F32), pltpu.VMEM((tb, kw), F32), pltpu.VMEM((tb, kw), F32),
                        pltpu.VMEM((tb, kw), F32)],
        compiler_params=_cparams(("arbitrary", "arbitrary")),
        name="hgrn_prompt",
    )(pa, pa, pa, pa, lb_logits, gain)


def _layer1_prompt_kernel(x_ref, oa_ref, ob_ref, woe_ref, gn_ref, w_ref, wr_ref, wg_ref, bg_ref, gain_ref, woo_ref,
                          fn_ref, y_ref, s_ref, st_ref, b_ref, proj_ref, lf_ref, o_ref, *, heads, dk, dv, tb,
                          q_scale):
    first, last = _state_step_edges(st_ref, s_ref)
    first()
    kw, vw = heads * dk, heads * dv
    aw = oa_ref.shape[1]
    x1 = (x_ref[...] + jnp.dot(oa_ref[...], woe_ref[0:aw, :], preferred_element_type=F32)
          + jnp.dot(ob_ref[...], woe_ref[aw:, :], preferred_element_type=F32))
    h = _rmsnorm_rows(x1, gn_ref[...]).astype(BF16)
    _proj_cols(h, w_ref, proj_ref)
    r = _dot_wt(h, wr_ref, 0, LANES)
    z = jnp.dot(r.astype(BF16), wg_ref[...], preferred_element_type=F32) + bg_ref[...]
    lf_ref[...] = _log_sigmoid(z) / GLA_GATE_NORMALIZER
    _recurrence_block(proj_ref.at[:, 0:kw], proj_ref.at[:, kw:2 * kw], proj_ref.at[:, 2 * kw:2 * kw + vw],
                      proj_ref.at[:, 2 * kw + vw:2 * kw + 2 * vw], lf_ref, gain_ref, o_ref, st_ref, b_ref,
                      fa_ref=None, lb=None, heads=heads, dk=dk, dv=dv, tb=tb, q_scale=q_scale)
    y = x1 + jnp.dot(o_ref[...], woo_ref[...], preferred_element_type=F32)
    y_ref[...] = _rmsnorm_rows(y, fn_ref[...])
    last()


def _layer1_prompt(x, o_a, o_b, w, *, bsz, seq, heads, dk, dv, tb, q_scale):
    nt = seq // tb
    kw, vw = heads * dk, heads * dv
    row = lambda b, t: (b * nt + t, 0)
    const = lambda b, t: (0, 0)
    full = lambda a: pl.BlockSpec(a.shape, const)
    weights = [w["w_out_even"], w["norm_odd"], w["w_odd"], w["w_r"], w["w_gate"], w["b_gate"], w["gla_gain"],
               w["w_out_odd"], w["final_norm"]]
    return pl.pallas_call(
        functools.partial(_layer1_prompt_kernel, heads=heads, dk=dk, dv=dv, tb=tb, q_scale=q_scale),
        grid=(bsz, nt),
        in_specs=[pl.BlockSpec((tb, D_MODEL), row), pl.BlockSpec((tb, o_a.shape[1]), row),
                  pl.BlockSpec((tb, o_b.shape[1]), row)] + [full(a) for a in weights],
        out_specs=[pl.BlockSpec((tb, D_MODEL), row), pl.BlockSpec((1, heads, dk, dv), lambda b, t: (b, 0, 0, 0))],
        out_shape=[jax.ShapeDtypeStruct((bsz * seq, D_MODEL), F32),
                   jax.ShapeDtypeStruct((bsz, heads, dk, dv), F32)],
        scratch_shapes=[pltpu.VMEM((heads, dv, dk), F32), pltpu.VMEM((tb, kw), F32),
                        pltpu.VMEM((tb, 2 * kw + 2 * vw), F32), pltpu.VMEM((tb, kw), F32),
                        pltpu.VMEM((tb, vw), BF16)],
        compiler_params=_cparams(("arbitrary", "arbitrary")),
        name="layer1_prompt",
    )(x, o_a, o_b, *weights)


def _gla_sample_kernel(*refs, hgrn, heads, dk, dv, t_new, q_scale):
    if hgrn:
        q_ref, fa_ref, v_ref, gate_ref, lb_ref, gain_ref, s0_ref, o_ref, s_ref = refs
    else:
        q_ref, k_ref, v_ref, gate_ref, g_ref, gain_ref, s0_ref, o_ref, s_ref = refs
    for h in range(heads):
        ksl = slice(h * dk, (h + 1) * dk)
        vsl = slice(h * dv, (h + 1) * dv)
        if hgrn:
            g, k = _hgrn_gate(fa_ref[:, ksl], _lower_bound(lb_ref[...], HGRN_LAYER)[:, ksl])
        else:
            g, k = g_ref[:, ksl], k_ref[:, ksl]
        q = q_ref[:, ksl] * q_scale
        v = v_ref[:, vsl]
        o_h = None
        for e in range(SUBLANES // t_new):
            o_e, st_new = _exact_group(q, k, v, g, s0_ref[e, h].T, e * t_new, (e + 1) * t_new)
            s_ref[e, h] = st_new.T
            o_h = o_e if o_h is None else o_h + o_e
        o_ref[:, vsl] = _head_norm_gate(o_h, gain_ref[:, vsl], gate_ref[:, vsl])


def _gla_sample(hgrn, arrays, col_blocks, small, gain, s0, *, t_new, heads, dk, dv, q_scale):
    m = arrays[0].shape[0]
    per = SUBLANES // t_new
    kw, vw = heads * dk, heads * dv
    widths = [kw, kw, vw, vw] + ([] if hgrn else [kw])
    in_specs = [pl.BlockSpec((SUBLANES, w), functools.partial(lambda i, cb: (i, cb), cb=cb))
                for w, cb in zip(widths, col_blocks)]
    operands = list(arrays)
    if hgrn:
        in_specs.append(pl.BlockSpec(small.shape, lambda i: (0, 0)))
        operands.append(small)
    in_specs.append(pl.BlockSpec((1, vw), lambda i: (0, 0)))
    operands.append(gain)
    in_specs.append(pl.BlockSpec((per, heads, dk, dv), lambda i: (i, 0, 0, 0)))
    operands.append(s0)
    return pl.pallas_call(
        functools.partial(_gla_sample_kernel, hgrn=hgrn, heads=heads, dk=dk, dv=dv, t_new=t_new, q_scale=q_scale),
        grid=(m // SUBLANES,),
        in_specs=in_specs,
        out_specs=[pl.BlockSpec((SUBLANES, vw), lambda i: (i, 0)),
                   pl.BlockSpec((per, heads, dk, dv), lambda i: (i, 0, 0, 0))],
        out_shape=[jax.ShapeDtypeStruct((m, vw), F32), jax.ShapeDtypeStruct(s0.shape, F32)],
        compiler_params=_cparams(("arbitrary",)),
        name="hgrn_sample" if hgrn else "gla_sample",
    )(*operands)


FOX_STRIP = 64


def _fox_prompt_step(qi, ki, before_head, q_ref, k_ref, v_ref, gate_ref, o_ref, m_ref, l_ref, acc_ref, p_ref, a_ref,
                     *, tq, heads, dh):
    ncb = tq // LANES
    aw = 2 * dh

    @pl.when(ki == 0)
    def _():
        m_ref[...] = jnp.full_like(m_ref, -jnp.inf)
        l_ref[...] = jnp.zeros_like(l_ref)
        acc_ref[...] = jnp.zeros_like(acc_ref)

    def step(diag):
        if diag:
            r_i = lax.broadcasted_iota(jnp.int32, (FOX_STRIP, LANES), 0)
            c_i = lax.broadcasted_iota(jnp.int32, (FOX_STRIP, LANES), 1)
        for h in range(heads):
            before_head(h)
            sl = slice(h * dh, (h + 1) * dh)
            s = lax.dot_general(q_ref[:, h * aw:(h + 1) * aw], k_ref[:, h * aw:(h + 1) * aw], _NT,
                                preferred_element_type=F32)
            for r0 in range(0, tq, FOX_STRIP):
                rows = slice(r0, r0 + FOX_STRIP)
                live = [j for j in range(ncb) if not (diag and j * LANES > r0 + FOX_STRIP - 1)]
                blocks = []
                for j in live:
                    lg = s[rows, j * LANES:(j + 1) * LANES]
                    if diag and (j + 1) * LANES - 1 > r0:
                        lg = jnp.where(r_i + r0 >= c_i + j * LANES, lg, -jnp.inf)
                    blocks.append(lg)
                m_cur = blocks[0]
                for lg in blocks[1:]:
                    m_cur = jnp.maximum(m_cur, lg)
                m_prev = m_ref[h, rows]
                m_new = jnp.maximum(m_prev, jnp.max(m_cur, axis=-1, keepdims=True))
                alpha = jnp.exp2(m_prev - m_new)
                probs = [jnp.exp2(lg - m_new) for lg in blocks]
                row_sum = probs[0]
                for pj in probs[1:]:
                    row_sum = row_sum + pj
                l_ref[h, rows] = alpha * l_ref[h, rows] + jnp.sum(row_sum, axis=-1, keepdims=True)
                m_ref[h, rows] = m_new
                a_ref[rows] = alpha
                for j, pj in zip(live, probs):
                    p_ref[rows, j * LANES:(j + 1) * LANES] = pj.astype(BF16)
                for j in range(ncb):
                    if j not in live:
                        p_ref[rows, j * LANES:(j + 1) * LANES] = jnp.zeros((FOX_STRIP, LANES), BF16)
            acc_ref[:, sl] = a_ref[...] * acc_ref[:, sl] + jnp.dot(p_ref[...], v_ref[:, sl],
                                                                   preferred_element_type=F32)

    @pl.when(ki < qi)
    def _():
        step(False)

    @pl.when(ki == qi)
    def _():
        step(True)
        for h in range(heads):
            sl = slice(h * dh, (h + 1) * dh)
            o_ref[:, sl] = (acc_ref[:, sl] / l_ref[h] * _silu(gate_ref[:, sl])).astype(o_ref.dtype)


N_PROMPT_REFS = 4
N_SAMPLE_REFS = 9


def _fox_kernel(qi_ref, ki_ref, pt_ref, *refs, tq, heads, dh, sample_kw):
    prompt_in = refs[:N_PROMPT_REFS]
    sample_in = refs[N_PROMPT_REFS:N_PROMPT_REFS + N_SAMPLE_REFS]
    o_ref, os_ref = refs[N_PROMPT_REFS + N_SAMPLE_REFS:N_PROMPT_REFS + N_SAMPLE_REFS + 2]
    scratch = refs[N_PROMPT_REFS + N_SAMPLE_REFS + 2:]
    prompt_scratch, sample_scratch = scratch[:5], scratch[5:]
    p = pl.program_id(1)
    step = pl.program_id(0) * pl.num_programs(1) + p

    def before_head(h):
        u = step * heads + h

        @pl.when(u < sample_kw["n_units"])
        def _():
            _fox_sample_unit(u, pt_ref, *sample_in, os_ref, *sample_scratch, heads=heads, dh=dh, **sample_kw)

    _fox_prompt_step(qi_ref[p], ki_ref[p], before_head, *prompt_in, o_ref, *prompt_scratch, tq=tq, heads=heads, dh=dh)


def _fox_attention(qa, ka, vb, gate, page_table, k_pages, v_pages, lf_pages, q_s, k_new, v_new, gate_s, cn_col,
                   cn_row, *, bsz, seq, tq, pages_per_unit):
    nq = seq // tq
    pairs = [(qi, ki) for qi in range(nq) for ki in range(qi + 1)]
    qi_tab = jnp.asarray(np.array([p[0] for p in pairs], np.int32))
    ki_tab = jnp.asarray(np.array([p[1] for p in pairs], np.int32))
    w = B_WIDTH
    dh = B_HEAD_DIM
    n_pool = lf_pages.shape[0]
    sfx = _page_suffix(lf_pages, B_HEADS).reshape(n_pool, 1, -1)
    db, n_pages = page_table.shape
    pp = pages_per_unit
    n_units = db * (n_pages // pp)
    assert bsz * len(pairs) * B_HEADS >= n_units, "not enough prompt steps to carry the sample page units"
    nrow = q_s.shape[1]
    pw = PAGE_SIZE * B_HEADS
    q_map = lambda b, p, qt, kt, pt: (b * nq + qt[p], 0)
    k_map = lambda b, p, qt, kt, pt: (b * nq + kt[p], 0)
    whole = lambda a: pl.BlockSpec(a.shape, lambda b, p, qt, kt, pt: (0,) * a.ndim)
    hbm = pl.BlockSpec(memory_space=pl.ANY)
    n_slots = 2
    grid_spec = pltpu.PrefetchScalarGridSpec(
        num_scalar_prefetch=3,
        grid=(bsz, len(pairs)),
        in_specs=[pl.BlockSpec((tq, 2 * w), q_map), pl.BlockSpec((tq, 2 * w), k_map),
                  pl.BlockSpec((tq, w), k_map), pl.BlockSpec((tq, w), q_map),
                  hbm, hbm, hbm, whole(q_s), whole(k_new), whole(v_new), whole(gate_s), whole(cn_col),
                  whole(cn_row)],
        out_specs=[pl.BlockSpec((tq, w), q_map),
                   pl.BlockSpec((db, nrow, dh), lambda b, p, qt, kt, pt: (0, 0, 0))],
        scratch_shapes=[pltpu.VMEM((B_HEADS, tq, LANES), F32), pltpu.VMEM((B_HEADS, tq, LANES), F32),
                        pltpu.VMEM((tq, w), F32), pltpu.VMEM((tq, tq), BF16), pltpu.VMEM((tq, LANES), F32),
                        pltpu.VMEM((n_slots, pp, pw, dh), F32), pltpu.VMEM((n_slots, pp, pw, dh), F32),
                        pltpu.VMEM((n_slots, pp, 1, 2 * pw), F32), pltpu.SemaphoreType.DMA((3, n_slots)),
                        pltpu.VMEM((nrow, 1), F32), pltpu.VMEM((nrow, 1), F32), pltpu.VMEM((nrow, dh), F32),
                        pltpu.VMEM((1, pw), F32)],
    )
    sample_kw = dict(pp=pp, t_new=nrow // B_HEADS, n_pages=n_pages, n_units=n_units)
    return pl.pallas_call(
        functools.partial(_fox_kernel, tq=tq, heads=B_HEADS, dh=dh, sample_kw=sample_kw),
        grid_spec=grid_spec,
        out_shape=[jax.ShapeDtypeStruct((bsz * seq, w), BF16), jax.ShapeDtypeStruct((db, nrow, dh), F32)],
        compiler_params=_cparams(("arbitrary", "arbitrary")),
        name="fox_attention",
    )(qi_tab, ki_tab, page_table, qa, ka, vb, gate, k_pages, v_pages, sfx, q_s, k_new, v_new, gate_s, cn_col,
      cn_row)


def _page_suffix_kernel(lf_ref, later_ref, total_ref, out_ref):
    lf = lf_ref[...]
    pw = lf.shape[1]
    out_ref[:, 0:pw] = _x_dot01(lf, later_ref[...])
    out_ref[:, pw:2 * pw] = _x_dot01(lf, total_ref[...])


def _page_suffix(lf_pages, heads):
    n_pool, pw = lf_pages.shape
    tm = 512 if n_pool % 512 == 0 else n_pool
    j = np.arange(pw)
    in_head, in_tok = j[:, None] // (pw // heads), j[:, None] % (pw // heads)
    out_head, out_tok = j[None, :] % heads, j[None, :] // heads
    head_eq = in_head == out_head
    later = jnp.asarray(head_eq & (in_tok > out_tok), BF16)
    total = jnp.asarray(head_eq, BF16)
    row = lambda i: (i, 0)
    const = lambda i: (0, 0)
    return pl.pallas_call(
        _page_suffix_kernel,
        grid=(n_pool // tm,),
        in_specs=[pl.BlockSpec((tm, pw), row), pl.BlockSpec((pw, pw), const), pl.BlockSpec((pw, pw), const)],
        out_specs=pl.BlockSpec((tm, 2 * pw), row),
        out_shape=jax.ShapeDtypeStruct((n_pool, 2 * pw), F32),
        compiler_params=_cparams(("arbitrary",)),
        name="page_suffix",
    )(lf_pages, later, total)


def _fox_sample_unit(u, pt_ref, k_hbm, v_hbm, sfx_hbm, q_ref, kn_ref, vn_ref, gate_ref, cn_col_ref, cn_row_ref, o_ref,
                     kbuf, vbuf, sbuf, sem, m_ref, l_ref, acc_ref, carry_ref, *, pp, heads, dh, t_new, n_pages,
                     n_units):
    nj = n_pages // pp
    b = lax.div(u, nj)
    j = lax.rem(u, nj)
    nrow = t_new * heads
    pw = PAGE_SIZE * heads
    slot = lax.rem(u, 2)

    def page_copies(uu, sl):
        bb = lax.div(uu, nj)
        jj = lax.rem(uu, nj)
        copies = []
        for i in range(pp):
            page = pt_ref[bb, n_pages - 1 - (jj * pp + i)]
            for kind, (src, dst) in enumerate(((k_hbm, kbuf), (v_hbm, vbuf), (sfx_hbm, sbuf))):
                copies.append(pltpu.make_async_copy(src.at[page], dst.at[sl, i], sem.at[kind, sl]))
        return copies

    @pl.when(u == 0)
    def _():
        for cp in page_copies(u, slot):
            cp.start()

    @pl.when(u + 1 < n_units)
    def _():
        for cp in page_copies(u + 1, 1 - slot):
            cp.start()

    for cp in page_copies(u, slot):
        cp.wait()
    k_refs = [kbuf.at[slot, i] for i in range(pp)]
    v_refs = [vbuf.at[slot, i] for i in range(pp)]
    sfx_refs = [sbuf.at[slot, i] for i in range(pp)]

    @pl.when(j == 0)
    def _():
        m_ref[...] = jnp.full_like(m_ref, -jnp.inf)
        l_ref[...] = jnp.zeros_like(l_ref)
        acc_ref[...] = jnp.zeros_like(acc_ref)
        carry_ref[...] = jnp.zeros_like(carry_ref)

    q = q_ref[b]
    cn_col = cn_col_ref[b] * LOG2E
    row_head = lax.broadcasted_iota(jnp.int32, (nrow, pw), 0) % heads
    col_head = lax.broadcasted_iota(jnp.int32, (nrow, pw), 1) % heads
    same_head = row_head == col_head

    def online(logit_list, v_list):
        m_cur = logit_list[0]
        for lg in logit_list[1:]:
            m_cur = jnp.maximum(m_cur, lg)
        m_prev = m_ref[...]
        m_new = jnp.maximum(m_prev, jnp.max(m_cur, axis=-1, keepdims=True))
        alpha = jnp.exp2(m_prev - m_new)
        probs = [jnp.exp2(lg - m_new) for lg in logit_list]
        row_sum = probs[0]
        for pj in probs[1:]:
            row_sum = row_sum + pj
        l_ref[...] = alpha * l_ref[...] + jnp.sum(row_sum, axis=-1, keepdims=True)
        pv = None
        for pj, v in zip(probs, v_list):
            t = jnp.dot(pj.astype(BF16), v.astype(BF16), preferred_element_type=F32)
            pv = t if pv is None else pv + t
        acc_ref[...] = alpha * acc_ref[...] + pv
        m_ref[...] = m_new

    logit_list = []
    carry = carry_ref[...]
    for i in range(pp):
        suffix = (carry + sfx_refs[i][:, 0:pw]) * LOG2E
        carry = carry + sfx_refs[i][:, pw:2 * pw]
        s = lax.dot_general(q, k_refs[i][...].astype(BF16), _NT, preferred_element_type=F32)
        logit_list.append(jnp.where(same_head, s + cn_col + suffix, -jnp.inf))
    carry_ref[...] = carry
    online(logit_list, [v_refs[i][...] for i in range(pp)])

    @pl.when(j == nj - 1)
    def _():
        s = lax.dot_general(q, kn_ref[b].astype(BF16), _NT, preferred_element_type=F32)
        logits = s + cn_col - cn_row_ref[b] * LOG2E
        r = lax.broadcasted_iota(jnp.int32, (nrow, nrow), 0)
        c = lax.broadcasted_iota(jnp.int32, (nrow, nrow), 1)
        keep = jnp.logical_and((r % heads) == (c % heads), (r // heads) >= (c // heads))
        online([jnp.where(keep, logits, -jnp.inf)], [vn_ref[b]])
        o_ref[b] = acc_ref[...] / l_ref[...] * _silu(gate_ref[b])


def _out_proj_kernel(*refs, n_in, final):
    ins = refs[:n_in]
    w_ref, x_ref = refs[n_in], refs[n_in + 1]
    y = x_ref[...]
    k0 = 0
    for a_ref in ins:
        kw = a_ref.shape[1]
        y = y + jnp.dot(a_ref[...].astype(BF16), w_ref[k0:k0 + kw, :], preferred_element_type=F32)
        k0 += kw
    if final:
        g_ref, o_ref = refs[n_in + 2], refs[n_in + 3]
        o_ref[...] = _rmsnorm_rows(y, g_ref[...])
    else:
        refs[n_in + 2][...] = y


def _out_proj(ins, w, x, final_gain, *, tm):
    m = x.shape[0]
    const = lambda i: (0, 0)
    row = lambda i: (i, 0)
    in_specs = [pl.BlockSpec((tm, a.shape[1]), row) for a in ins]
    in_specs += [pl.BlockSpec(w.shape, const), pl.BlockSpec((tm, D_MODEL), row)]
    operands = list(ins) + [w, x]
    if final_gain is not None:
        in_specs.append(pl.BlockSpec((1, D_MODEL), const))
        operands.append(final_gain)
    return pl.pallas_call(
        functools.partial(_out_proj_kernel, n_in=len(ins), final=final_gain is not None),
        grid=(m // tm,),
        in_specs=in_specs,
        out_specs=pl.BlockSpec((tm, D_MODEL), row),
        out_shape=jax.ShapeDtypeStruct((m, D_MODEL), F32),
        compiler_params=_cparams(("arbitrary",)),
        name="out_proj_final" if final_gain is not None else "out_proj",
    )(*operands)


def _pad_cols(w, n):
    return jnp.pad(w, ((0, 0), (0, n - w.shape[1])))


def _prep_weights(weights):
    (norm_even, w_in_even, b_fox_f, lb_logits, hgrn_gain, w_out_even, norm_odd, w_in_odd, w_gla_gate, b_gla_gate,
     gla_gain, w_out_odd, final_norm) = weights
    n_even = 4 * A_WIDTH + 4 * B_WIDTH
    n_odd = 2 * C_KEY_WIDTH + 2 * C_VAL_WIDTH
    w_even_t = w_in_even[0].T.astype(BF16)
    w_odd_t = w_in_odd[0].T.astype(BF16)
    pad_rows = lambda a: jnp.pad(a, ((0, LANES - a.shape[0]), (0, 0)))
    return dict(
        norm_even=norm_even[0].reshape(1, D_MODEL),
        w_even=w_even_t[:n_even],
        w_fb=pad_rows(w_even_t[n_even:]),
        b_fox=_pad_cols(b_fox_f[0].reshape(1, B_HEADS), LANES),
        lb_logits=lb_logits,
        hgrn_gain=hgrn_gain[0].reshape(1, A_WIDTH),
        w_out_even=w_out_even[0].astype(BF16),
        norm_odd=norm_odd[0].reshape(1, D_MODEL),
        w_odd=w_odd_t[:n_odd],
        w_r=pad_rows(w_odd_t[n_odd:]),
        w_gate=jnp.pad(w_gla_gate[0], ((0, LANES - C_GATE_RANK), (0, 0))).astype(BF16),
        b_gate=b_gla_gate[0].reshape(1, C_KEY_WIDTH),
        gla_gain=gla_gain[0].reshape(1, C_VAL_WIDTH),
        w_out_odd=w_out_odd[0].astype(BF16),
        final_norm=final_norm.reshape(1, D_MODEL),
    )


def _forward(x_prompt, x_sample, ctx, w):
    bsz, seq, _ = x_prompt.shape
    db, t_new, _ = x_sample.shape
    mp, ms = bsz * seq, db * t_new
    xp = x_prompt.reshape(mp, D_MODEL)
    xs = x_sample.reshape(ms, D_MODEL)
    hg = dict(heads=A_HEADS, dk=A_HEAD_DIM, dv=A_HEAD_DIM, q_scale=1.0)
    gl = dict(heads=C_HEADS, dk=C_KEY_DIM, dv=C_VAL_DIM, q_scale=C_KEY_DIM ** -0.5)

    even = lambda x2, s, tm: _even_proj(x2, w["norm_even"], w["w_even"], w["w_fb"], w["b_fox"], seq=s, tm=tm)
    pa_p, gate_p, qa_p, ka_p, vb_p, krows_p, vrows_p, lf_p, _ = even(xp, seq, 512)
    pa_s, gate_s, qa_s, _, _, krows_s, vrows_s, lf_s, c_s = even(xs, t_new, ms)

    oa_p, sa_p = _hgrn_prompt(pa_p, w["lb_logits"], w["hgrn_gain"], bsz=bsz, seq=seq, tb=512, **hg)
    oa_s, sa_s = _gla_sample(True, [pa_s] * 4, [0, 1, 2, 3], w["lb_logits"], w["hgrn_gain"], ctx["state_hgrn"],
                             t_new=t_new, **hg)
    nrow = t_new * B_HEADS
    rows = lambda a: a.reshape(db, nrow, B_HEAD_DIM)
    cn = c_s[:, :B_HEADS].reshape(db, nrow)
    q_rows = qa_s.reshape(db, nrow, 2 * B_HEAD_DIM)[:, :, :B_HEAD_DIM]
    ob_p, ob_s = _fox_attention(qa_p, ka_p, vb_p, gate_p, ctx["page_table"], ctx["k_pages"], ctx["v_pages"],
                                ctx["lf_pages"], q_rows, rows(krows_s), rows(vrows_s), rows(gate_s),
                                cn.reshape(db, nrow, 1), cn.reshape(db, 1, nrow), bsz=bsz, seq=seq,
                                tq=min(seq, 1024), pages_per_unit=8)

    y_p, sc_p = _layer1_prompt(xp, oa_p, ob_p, w, bsz=bsz, seq=seq, tb=512, **gl)
    x1_s = _out_proj([oa_s, ob_s.reshape(ms, B_WIDTH)], w["w_out_even"], xs, None, tm=ms)
    proj1, lf1 = _odd_proj(x1_s, w["norm_odd"], w["w_odd"], w["w_r"], w["w_gate"], w["b_gate"], tm=ms)
    oc_s, sc_s = _gla_sample(False, [proj1, proj1, proj1, proj1, lf1], [0, 1, 1, 2, 0], None, w["gla_gain"],
                             ctx["state_gla"], t_new=t_new, **gl)
    y_s = _out_proj([oc_s], w["w_out_odd"], x1_s, w["final_norm"], tm=ms)

    heads4 = lambda a, n, s: a.reshape(n, s, B_HEADS, B_HEAD_DIM)
    logf = lambda a, n, s: a[:, :B_HEADS].reshape(n, s, B_HEADS)
    prompt = (y_p.reshape(bsz, seq, D_MODEL), heads4(krows_p, bsz, seq), heads4(vrows_p, bsz, seq),
              logf(lf_p, bsz, seq), sa_p, sc_p)
    sample = (y_s.reshape(db, t_new, D_MODEL), heads4(krows_s, db, t_new), heads4(vrows_s, db, t_new),
              logf(lf_s, db, t_new), sa_s, sc_s)
    return prompt, sample


def kernel(x_prompt, x_sample, cache_fox_k, cache_fox_v, cache_fox_logf, state_hgrn, state_gla, page_table,
           norm_even, w_in_even, b_fox_f, lb_logits, hgrn_gain, w_out_even, norm_odd, w_in_odd, w_gla_gate,
           b_gla_gate, gla_gain, w_out_odd, final_norm):
    weights = _prep_weights((norm_even, w_in_even, b_fox_f, lb_logits, hgrn_gain, w_out_even, norm_odd, w_in_odd,
                             w_gla_gate, b_gla_gate, gla_gain, w_out_odd, final_norm))
    bsz, seq, _ = x_prompt.shape
    n_pool = cache_fox_k.shape[1]
    pw = PAGE_SIZE * B_HEADS
    ctx = dict(
        page_table=page_table,
        k_pages=cache_fox_k[0].reshape(n_pool, pw, B_HEAD_DIM),
        v_pages=cache_fox_v[0].reshape(n_pool, pw, B_HEAD_DIM),
        lf_pages=cache_fox_logf[0].transpose(0, 2, 1).reshape(n_pool, pw),
        state_hgrn=state_hgrn[0], state_gla=state_gla[0])
    (y_p, kp, vp, lfp, hgrn_p, gla_p), (y_s, ks, vs, lfs, hgrn_s, gla_s) = _forward(x_prompt, x_sample, ctx, weights)
    n_pp = seq // PAGE_SIZE
    fox_k_prompt = kp.reshape(1, bsz, n_pp, PAGE_SIZE, B_HEADS, B_HEAD_DIM)
    fox_v_prompt = vp.reshape(1, bsz, n_pp, PAGE_SIZE, B_HEADS, B_HEAD_DIM)
    fox_logf_prompt = lfp.reshape(1, bsz, n_pp, PAGE_SIZE, B_HEADS)
    return (y_p, y_s, fox_k_prompt, fox_v_prompt, fox_logf_prompt, hgrn_p[None], gla_p[None],
            ks[None], vs[None], lfs[None], hgrn_s[None], gla_s[None])
```

```python
import functools

import numpy as np
import jax
import jax.numpy as jnp
from jax import lax
from jax.experimental import pallas as pl
from jax.experimental.pallas import tpu as pltpu

F32 = jnp.float32
BF16 = jnp.bfloat16

D_MODEL = 1024
PAGE_SIZE = 128
A_HEADS = 4
A_HEAD_DIM = 128
A_WIDTH = 512
B_HEADS = 4
B_HEAD_DIM = 128
B_WIDTH = 512
C_HEADS = 4
C_KEY_WIDTH = 512
C_VAL_WIDTH = 1024
C_KEY_DIM = 128
C_VAL_DIM = 256
C_GATE_RANK = 16
GLA_GATE_NORMALIZER = 16.0
EPS = 1e-6
HGRN_LAYER = 0
LOG2E = 1.4426950408889634
AUG_PIECES = 3

LANES = 128
SUBLANES = 8
VMEM_LIMIT = 56 * 1024 * 1024
CHUNK = 64
MAX_CHUNK_LOG_DECAY = 60.0

_NT = (((1,), (1,)), ((), ()))
_TN = (((0,), (0,)), ((), ()))


def _cparams(sem):
    return pltpu.CompilerParams(dimension_semantics=sem, vmem_limit_bytes=VMEM_LIMIT)


def _sigmoid(x):
    return 1.0 / (1.0 + jnp.exp(-x))


def _log_sigmoid(x):
    return jnp.minimum(x, 0.0) - jnp.log1p(jnp.exp(-jnp.abs(x)))


def _silu(x):
    return x * _sigmoid(x)


def _rmsnorm_rows(x, g):
    return x * lax.rsqrt(jnp.mean(x * x, axis=-1, keepdims=True) + EPS) * g


def _split3(x):
    p1 = x.astype(BF16)
    r1 = x - p1.astype(F32)
    p2 = r1.astype(BF16)
    p3 = (r1 - p2.astype(F32)).astype(BF16)
    return p1, p2, p3


def _dot01(m01, x):
    acc = None
    for p in _split3(x):
        t = jnp.dot(m01, p, preferred_element_type=F32)
        acc = t if acc is None else acc + t
    return acc


def _x_dot01(x, m01):
    acc = None
    for p in _split3(x):
        t = jnp.dot(p, m01, preferred_element_type=F32)
        acc = t if acc is None else acc + t
    return acc


def _lower_tri(n, seq):
    r = lax.broadcasted_iota(jnp.int32, (n, n), 0)
    c = lax.broadcasted_iota(jnp.int32, (n, n), 1)
    keep = r >= c
    if seq < n:
        keep = jnp.logical_and(keep, (r // seq) == (c // seq))
    return jnp.where(keep, 1.0, 0.0).astype(BF16)


def _dot_wt(h, wt_ref, c0, n):
    return lax.dot_general(h, wt_ref[c0:c0 + n, :], _NT, preferred_element_type=F32)


def _proj_cols(h, wt_ref, out_ref):
    step = 512
    for c0 in range(0, wt_ref.shape[0], step):
        out_ref[:, c0:c0 + step] = _dot_wt(h, wt_ref, c0, step)


def _aug_selectors():
    sel_q = np.zeros((LANES, B_WIDTH), np.float32)
    sel_k = np.zeros((LANES, B_WIDTH), np.float32)
    for p in range(AUG_PIECES):
        for hd in range(B_HEADS):
            sel_q[p * B_HEADS + hd, hd * B_HEAD_DIM + p] = 1.0
            sel_k[p * B_HEADS + hd, hd * B_HEAD_DIM + AUG_PIECES + p] = -1.0
    return jnp.asarray(sel_q, BF16), jnp.asarray(sel_k, BF16)


def _even_proj_kernel(x_ref, g_ref, w_ref, wfb_ref, bfox_ref, selq_ref, selk_ref, pa_ref, gate_ref, qa_ref, ka_ref,
                      vb_ref, kout_ref, vout_ref, lf_ref, c_ref, carry_ref, *, tm, seq):
    i = pl.program_id(0)
    h = _rmsnorm_rows(x_ref[...], g_ref[...]).astype(BF16)
    na = 4 * A_WIDTH
    bw = B_WIDTH
    dh = B_HEAD_DIM
    for c0 in range(0, na, 512):
        pa_ref[:, c0:c0 + 512] = _dot_wt(h, w_ref, c0, 512)
    q = (_dot_wt(h, w_ref, na, bw) * (dh ** -0.5 * LOG2E)).astype(BF16)
    for hd in range(B_HEADS):
        qa_ref[:, 2 * hd * dh:(2 * hd + 1) * dh] = q[:, hd * dh:(hd + 1) * dh]
    for j, out_ref in ((1, kout_ref), (2, vout_ref)):
        kv = _dot_wt(h, w_ref, na + j * bw, bw)
        kv16 = kv.astype(BF16)
        if j == 1:
            for hd in range(B_HEADS):
                ka_ref[:, 2 * hd * dh:(2 * hd + 1) * dh] = kv16[:, hd * dh:(hd + 1) * dh]
        else:
            vb_ref[...] = kv16
        for hd in range(B_HEADS):
            out_ref[pl.ds(hd, tm, stride=B_HEADS), :] = kv[:, hd * dh:(hd + 1) * dh]
    gate_ref[...] = _dot_wt(h, w_ref, na + 3 * bw, bw)
    fb = _dot_wt(h, wfb_ref, 0, LANES) + bfox_ref[...]
    lane = lax.broadcasted_iota(jnp.int32, fb.shape, 1)
    lf = jnp.where(lane < B_HEADS, _log_sigmoid(fb), 0.0)
    lf_ref[...] = lf
    cs = _dot01(_lower_tri(tm, seq), lf)
    if seq > tm:
        @pl.when((i * tm) % seq == 0)
        def _():
            carry_ref[...] = jnp.zeros_like(carry_ref)
        cs = cs + carry_ref[0:1, :]
        carry_ref[0:1, :] = cs[tm - 1:tm, :]
    c_ref[...] = cs
    pos = lax.broadcasted_iota(jnp.int32, (1, bw), 1) % dh
    aug_q = jnp.where(jnp.logical_and(pos >= AUG_PIECES, pos < 2 * AUG_PIECES), 1.0, 0.0)
    aug_k = jnp.where(pos < AUG_PIECES, 1.0, 0.0)
    packed = None
    for p, piece in enumerate(_split3(cs * LOG2E)):
        shifted = piece.astype(F32) if p == 0 else pltpu.roll(piece.astype(F32), p * B_HEADS, 1)
        packed = shifted if packed is None else packed + shifted
    packed = packed.astype(BF16)
    aug_q = (aug_q + jnp.dot(packed, selq_ref[...], preferred_element_type=F32)).astype(BF16)
    aug_k = (aug_k + jnp.dot(packed, selk_ref[...], preferred_element_type=F32)).astype(BF16)
    for hd in range(B_HEADS):
        qa_ref[:, (2 * hd + 1) * dh:(2 * hd + 2) * dh] = aug_q[:, hd * dh:(hd + 1) * dh]
        ka_ref[:, (2 * hd + 1) * dh:(2 * hd + 2) * dh] = aug_k[:, hd * dh:(hd + 1) * dh]


def _even_proj(x, g, w_main, w_fb, b_fox, *, seq, tm):
    m = x.shape[0]
    const = lambda i: (0, 0)
    row = lambda i: (i, 0)
    sel_q, sel_k = _aug_selectors()
    return pl.pallas_call(
        functools.partial(_even_proj_kernel, tm=tm, seq=seq),
        grid=(m // tm,),
        in_specs=[pl.BlockSpec((tm, D_MODEL), row), pl.BlockSpec((1, D_MODEL), const),
                  pl.BlockSpec(w_main.shape, const), pl.BlockSpec(w_fb.shape, const),
                  pl.BlockSpec((1, LANES), const), pl.BlockSpec(sel_q.shape, const),
                  pl.BlockSpec(sel_k.shape, const)],
        out_specs=[pl.BlockSpec((tm, 4 * A_WIDTH), row), pl.BlockSpec((tm, B_WIDTH), row),
                   pl.BlockSpec((tm, 2 * B_WIDTH), row), pl.BlockSpec((tm, 2 * B_WIDTH), row),
                   pl.BlockSpec((tm, B_WIDTH), row), pl.BlockSpec((tm * B_HEADS, B_HEAD_DIM), row),
                   pl.BlockSpec((tm * B_HEADS, B_HEAD_DIM), row), pl.BlockSpec((tm, LANES), row),
                   pl.BlockSpec((tm, LANES), row)],
        out_shape=[jax.ShapeDtypeStruct((m, 4 * A_WIDTH), F32), jax.ShapeDtypeStruct((m, B_WIDTH), F32),
                   jax.ShapeDtypeStruct((m, 2 * B_WIDTH), BF16), jax.ShapeDtypeStruct((m, 2 * B_WIDTH), BF16),
                   jax.ShapeDtypeStruct((m, B_WIDTH), BF16),
                   jax.ShapeDtypeStruct((m * B_HEADS, B_HEAD_DIM), F32),
                   jax.ShapeDtypeStruct((m * B_HEADS, B_HEAD_DIM), F32),
                   jax.ShapeDtypeStruct((m, LANES), F32), jax.ShapeDtypeStruct((m, LANES), F32)],
        scratch_shapes=[pltpu.VMEM((SUBLANES, LANES), F32)],
        compiler_params=_cparams(("arbitrary",)),
        name="even_proj",
    )(x, g, w_main, w_fb, b_fox, sel_q, sel_k)


def _odd_proj_kernel(x_ref, g_ref, w_ref, wr_ref, wg_ref, bg_ref, proj_ref, lf_ref):
    h = _rmsnorm_rows(x_ref[...], g_ref[...]).astype(BF16)
    _proj_cols(h, w_ref, proj_ref)
    r = _dot_wt(h, wr_ref, 0, LANES)
    z = jnp.dot(r.astype(BF16), wg_ref[...], preferred_element_type=F32) + bg_ref[...]
    lf_ref[...] = _log_sigmoid(z) / GLA_GATE_NORMALIZER


def _odd_proj(x, g, w_main, w_r, w_gate, b_gate, *, tm):
    m = x.shape[0]
    n = w_main.shape[0]
    const = lambda i: (0, 0)
    row = lambda i: (i, 0)
    return pl.pallas_call(
        _odd_proj_kernel,
        grid=(m // tm,),
        in_specs=[pl.BlockSpec((tm, D_MODEL), row), pl.BlockSpec((1, D_MODEL), const),
                  pl.BlockSpec(w_main.shape, const), pl.BlockSpec(w_r.shape, const),
                  pl.BlockSpec((LANES, C_KEY_WIDTH), const), pl.BlockSpec((1, C_KEY_WIDTH), const)],
        out_specs=[pl.BlockSpec((tm, n), row), pl.BlockSpec((tm, C_KEY_WIDTH), row)],
        out_shape=[jax.ShapeDtypeStruct((m, n), F32), jax.ShapeDtypeStruct((m, C_KEY_WIDTH), F32)],
        compiler_params=_cparams(("arbitrary",)),
        name="odd_proj",
    )(x, g, w_main, w_r, w_gate, b_gate)


def _lower_bound(logits, layer):
    e = jnp.exp(logits - jnp.max(logits, axis=0, keepdims=True))
    return jnp.sum(e[:layer + 1, :], axis=0, keepdims=True) / jnp.sum(e, axis=0, keepdims=True)


def _hgrn_gate(fa, lb):
    f = lb + (1.0 - lb) * _sigmoid(fa)
    return jnp.log(f), 1.0 - f


def _exact_group(q, k, v, g, st, lo, hi):
    n = SUBLANES
    row = lax.broadcasted_iota(jnp.int32, (n, 1), 0)
    valid = jnp.logical_and(row >= lo, row < hi)
    q = jnp.where(valid, q, 0.0)
    k = jnp.where(valid, k, 0.0)
    g = jnp.where(valid, g, 0.0)
    b = g
    for sh in (1, 2, 4):
        b = b + jnp.where(row >= sh, pltpu.roll(b, sh, 0), 0.0)
    o = lax.dot_general((q * jnp.exp(b)).astype(BF16), st.astype(BF16), _NT, preferred_element_type=F32)
    for s in range(lo, hi):
        w = jnp.exp(jnp.minimum(b - b[s:s + 1, :], 0.0))
        a = jnp.sum(q * k[s:s + 1, :] * w, axis=-1, keepdims=True)
        o = o + jnp.where(row >= s, a, 0.0) * v[s:s + 1, :]
    b_last = b[n - 1:n, :]
    k_hat = k * jnp.exp(b_last - b)
    st_new = st * jnp.exp(b_last) + lax.dot_general(v.astype(BF16), k_hat.astype(BF16), _TN,
                                                    preferred_element_type=F32)
    return o, st_new


def _head_norm_gate(o, gain, gate):
    y = o * lax.rsqrt(jnp.mean(o * o, axis=-1, keepdims=True) + EPS) * gain
    return y * _silu(gate)


def _recurrence_block(q_ref, k_ref, v_ref, gate_ref, g_ref, gain_ref, o_ref, st_ref, b_ref, *, fa_ref, lb, heads, dk,
                      dv, tb, q_scale):
    n_chunks = tb // CHUNK

    tri = _lower_tri(CHUNK, CHUNK)
    b_min = None
    for c in range(n_chunks):
        rows = slice(c * CHUNK, (c + 1) * CHUNK)
        if fa_ref is not None:
            g, k = _hgrn_gate(fa_ref[rows, :], lb)
            g_ref[rows, :] = g
            k_ref[rows, :] = k
        else:
            g = g_ref[rows, :]
        b = _dot01(tri, g)
        b_ref[rows, :] = b
        b_last = b[CHUNK - 1:CHUNK, :]
        b_min = b_last if b_min is None else jnp.minimum(b_min, b_last)
    chunk_form_ok = jnp.min(b_min) >= -MAX_CHUNK_LOG_DECAY

    def finish(rows, h, o_h):
        vsl = slice(h * dv, (h + 1) * dv)
        o_ref[rows, vsl] = _head_norm_gate(o_h, gain_ref[:, vsl], gate_ref[rows, vsl]).astype(o_ref.dtype)

    @pl.when(chunk_form_ok)
    def _():
        r_i = lax.broadcasted_iota(jnp.int32, (CHUNK, CHUNK), 0)
        c_i = lax.broadcasted_iota(jnp.int32, (CHUNK, CHUNK), 1)
        causal = r_i >= c_i

        for c in range(n_chunks):
            rows = slice(c * CHUNK, (c + 1) * CHUNK)
            b = b_ref[rows, :]
            e_b = jnp.exp(b)
            b_last = b[CHUNK - 1:CHUNK, :]
            e_last = jnp.exp(b_last)
            q_t = q_ref[rows, :] * q_scale * e_b
            k_t = k_ref[rows, :] * jnp.exp(-b)
            k_hat = k_t * e_last
            for h in range(heads):
                ksl = slice(h * dk, (h + 1) * dk)
                vsl = slice(h * dv, (h + 1) * dv)
                qh = q_t[:, ksl].astype(BF16)
                vh = v_ref[rows, vsl].astype(BF16)
                a = lax.dot_general(qh, k_t[:, ksl].astype(BF16), _NT, preferred_element_type=F32)
                a = jnp.where(causal, a, 0.0).astype(BF16)
                st = st_ref[h]
                o_h = jnp.dot(a, vh, preferred_element_type=F32) + lax.dot_general(
                    qh, st.astype(BF16), _NT, preferred_element_type=F32)
                st_ref[h] = st * e_last[:, ksl] + lax.dot_general(
                    vh, k_hat[:, ksl].astype(BF16), _TN, preferred_element_type=F32)
                finish(rows, h, o_h)

    @pl.when(jnp.logical_not(chunk_form_ok))
    def _():
        pair = 2 * SUBLANES

        def group(i, carry):
            for h in range(heads):
                ksl = slice(h * dk, (h + 1) * dk)
                vsl = slice(h * dv, (h + 1) * dv)
                outs = []
                for half in range(2):
                    rows = pl.ds(pl.multiple_of(i * pair + half * SUBLANES, SUBLANES), SUBLANES)
                    o_h, st_new = _exact_group(q_ref[rows, ksl] * q_scale, k_ref[rows, ksl], v_ref[rows, vsl],
                                               g_ref[rows, ksl], st_ref[h], 0, SUBLANES)
                    st_ref[h] = st_new
                    outs.append(o_h)
                finish(pl.ds(pl.multiple_of(i * pair, pair), pair), h, jnp.concatenate(outs, axis=0))
            return carry

        lax.fori_loop(0, tb // pair, group, 0)


def _state_step_edges(st_ref, s_ref):
    t = pl.program_id(1)

    def first():
        @pl.when(t == 0)
        def _():
            st_ref[...] = jnp.zeros_like(st_ref)

    def last():
        @pl.when(t == pl.num_programs(1) - 1)
        def _():
            for h in range(st_ref.shape[0]):
                s_ref[0, h] = st_ref[h].T

    return first, last


def _hgrn_prompt_kernel(q_ref, fa_ref, v_ref, gate_ref, lb_ref, gain_ref, o_ref, s_ref, st_ref, b_ref, g_ref, k_ref,
                        **kw):
    first, last = _state_step_edges(st_ref, s_ref)
    first()
    _recurrence_block(q_ref, k_ref, v_ref, gate_ref, g_ref, gain_ref, o_ref, st_ref, b_ref, fa_ref=fa_ref,
                      lb=_lower_bound(lb_ref[...], HGRN_LAYER), **kw)
    last()


def _hgrn_prompt(pa, lb_logits, gain, *, bsz, seq, heads, dk, dv, tb, q_scale):
    nt = seq // tb
    kw, vw = heads * dk, heads * dv
    col = lambda cb: (lambda b, t: (b * nt + t, cb))
    const = lambda b, t: (0, 0)
    return pl.pallas_call(
        functools.partial(_hgrn_prompt_kernel, heads=heads, dk=dk, dv=dv, tb=tb, q_scale=q_scale),
        grid=(bsz, nt),
        in_specs=[pl.BlockSpec((tb, kw), col(0)), pl.BlockSpec((tb, kw), col(1)), pl.BlockSpec((tb, vw), col(2)),
                  pl.BlockSpec((tb, vw), col(3)), pl.BlockSpec(lb_logits.shape, const),
                  pl.BlockSpec((1, vw), const)],
        out_specs=[pl.BlockSpec((tb, vw), col(0)), pl.BlockSpec((1, heads, dk, dv), lambda b, t: (b, 0, 0, 0))],
        out_shape=[jax.ShapeDtypeStruct((bsz * seq, vw), BF16), jax.ShapeDtypeStruct((bsz, heads, dk, dv), F32)],
        scratch_shapes=[pltpu.VMEM((heads, dv, dk), F32), pltpu.VMEM((tb, kw), F32), pltpu.VMEM((tb, kw), F32),
                        pltpu.VMEM((tb, kw), F32)],
        compiler_params=_cparams(("arbitrary", "arbitrary")),
        name="hgrn_prompt",
    )(pa, pa, pa, pa, lb_logits, gain)


def _layer1_prompt_kernel(x_ref, oa_ref, ob_ref, woe_ref, gn_ref, w_ref, wr_ref, wg_ref, bg_ref, gain_ref, woo_ref,
                          fn_ref, y_ref, s_ref, st_ref, b_ref, proj_ref, lf_ref, o_ref, *, heads, dk, dv, tb,
                          q_scale):
    first, last = _state_step_edges(st_ref, s_ref)
    first()
    kw, vw = heads * dk, heads * dv
    aw = oa_ref.shape[1]
    x1 = (x_ref[...] + jnp.dot(oa_ref[...], woe_ref[0:aw, :], preferred_element_type=F32)
          + jnp.dot(ob_ref[...], woe_ref[aw:, :], preferred_element_type=F32))
    h = _rmsnorm_rows(x1, gn_ref[...]).astype(BF16)
    _proj_cols(h, w_ref, proj_ref)
    r = _dot_wt(h, wr_ref, 0, LANES)
    z = jnp.dot(r.astype(BF16), wg_ref[...], preferred_element_type=F32) + bg_ref[...]
    lf_ref[...] = _log_sigmoid(z) / GLA_GATE_NORMALIZER
    _recurrence_block(proj_ref.at[:, 0:kw], proj_ref.at[:, kw:2 * kw], proj_ref.at[:, 2 * kw:2 * kw + vw],
                      proj_ref.at[:, 2 * kw + vw:2 * kw + 2 * vw], lf_ref, gain_ref, o_ref, st_ref, b_ref,
                      fa_ref=None, lb=None, heads=heads, dk=dk, dv=dv, tb=tb, q_scale=q_scale)
    y = x1 + jnp.dot(o_ref[...], woo_ref[...], preferred_element_type=F32)
    y_ref[...] = _rmsnorm_rows(y, fn_ref[...])
    last()


def _layer1_prompt(x, o_a, o_b, w, *, bsz, seq, heads, dk, dv, tb, q_scale):
    nt = seq // tb
    kw, vw = heads * dk, heads * dv
    row = lambda b, t: (b * nt + t, 0)
    const = lambda b, t: (0, 0)
    full = lambda a: pl.BlockSpec(a.shape, const)
    weights = [w["w_out_even"], w["norm_odd"], w["w_odd"], w["w_r"], w["w_gate"], w["b_gate"], w["gla_gain"],
               w["w_out_odd"], w["final_norm"]]
    return pl.pallas_call(
        functools.partial(_layer1_prompt_kernel, heads=heads, dk=dk, dv=dv, tb=tb, q_scale=q_scale),
        grid=(bsz, nt),
        in_specs=[pl.BlockSpec((tb, D_MODEL), row), pl.BlockSpec((tb, o_a.shape[1]), row),
                  pl.BlockSpec((tb, o_b.shape[1]), row)] + [full(a) for a in weights],
        out_specs=[pl.BlockSpec((tb, D_MODEL), row), pl.BlockSpec((1, heads, dk, dv), lambda b, t: (b, 0, 0, 0))],
        out_shape=[jax.ShapeDtypeStruct((bsz * seq, D_MODEL), F32),
                   jax.ShapeDtypeStruct((bsz, heads, dk, dv), F32)],
        scratch_shapes=[pltpu.VMEM((heads, dv, dk), F32), pltpu.VMEM((tb, kw), F32),
                        pltpu.VMEM((tb, 2 * kw + 2 * vw), F32), pltpu.VMEM((tb, kw), F32),
                        pltpu.VMEM((tb, vw), BF16)],
        compiler_params=_cparams(("arbitrary", "arbitrary")),
        name="layer1_prompt",
    )(x, o_a, o_b, *weights)


def _gla_sample_kernel(*refs, hgrn, heads, dk, dv, t_new, q_scale):
    if hgrn:
        q_ref, fa_ref, v_ref, gate_ref, lb_ref, gain_ref, s0_ref, o_ref, s_ref = refs
    else:
        q_ref, k_ref, v_ref, gate_ref, g_ref, gain_ref, s0_ref, o_ref, s_ref = refs
    for h in range(heads):
        ksl = slice(h * dk, (h + 1) * dk)
        vsl = slice(h * dv, (h + 1) * dv)
        if hgrn:
            g, k = _hgrn_gate(fa_ref[:, ksl], _lower_bound(lb_ref[...], HGRN_LAYER)[:, ksl])
        else:
            g, k = g_ref[:, ksl], k_ref[:, ksl]
        q = q_ref[:, ksl] * q_scale
        v = v_ref[:, vsl]
        o_h = None
        for e in range(SUBLANES // t_new):
            o_e, st_new = _exact_group(q, k, v, g, s0_ref[e, h].T, e * t_new, (e + 1) * t_new)
            s_ref[e, h] = st_new.T
            o_h = o_e if o_h is None else o_h + o_e
        o_ref[:, vsl] = _head_norm_gate(o_h, gain_ref[:, vsl], gate_ref[:, vsl])


def _gla_sample(hgrn, arrays, col_blocks, small, gain, s0, *, t_new, heads, dk, dv, q_scale):
    m = arrays[0].shape[0]
    per = SUBLANES // t_new
    kw, vw = heads * dk, heads * dv
    widths = [kw, kw, vw, vw] + ([] if hgrn else [kw])
    in_specs = [pl.BlockSpec((SUBLANES, w), functools.partial(lambda i, cb: (i, cb), cb=cb))
                for w, cb in zip(widths, col_blocks)]
    operands = list(arrays)
    if hgrn:
        in_specs.append(pl.BlockSpec(small.shape, lambda i: (0, 0)))
        operands.append(small)
    in_specs.append(pl.BlockSpec((1, vw), lambda i: (0, 0)))
    operands.append(gain)
    in_specs.append(pl.BlockSpec((per, heads, dk, dv), lambda i: (i, 0, 0, 0)))
    operands.append(s0)
    return pl.pallas_call(
        functools.partial(_gla_sample_kernel, hgrn=hgrn, heads=heads, dk=dk, dv=dv, t_new=t_new, q_scale=q_scale),
        grid=(m // SUBLANES,),
        in_specs=in_specs,
        out_specs=[pl.BlockSpec((SUBLANES, vw), lambda i: (i, 0)),
                   pl.BlockSpec((per, heads, dk, dv), lambda i: (i, 0, 0, 0))],
        out_shape=[jax.ShapeDtypeStruct((m, vw), F32), jax.ShapeDtypeStruct(s0.shape, F32)],
        compiler_params=_cparams(("arbitrary",)),
        name="hgrn_sample" if hgrn else "gla_sample",
    )(*operands)


FOX_STRIP = 64


def _fox_prompt_step(qi, ki, before_head, q_ref, k_ref, v_ref, gate_ref, o_ref, m_ref, l_ref, acc_ref, p_ref, a_ref,
                     *, tq, heads, dh):
    ncb = tq // LANES
    aw = 2 * dh

    @pl.when(ki == 0)
    def _():
        m_ref[...] = jnp.full_like(m_ref, -jnp.inf)
        l_ref[...] = jnp.zeros_like(l_ref)
        acc_ref[...] = jnp.zeros_like(acc_ref)

    def step(diag):
        if diag:
            r_i = lax.broadcasted_iota(jnp.int32, (FOX_STRIP, LANES), 0)
            c_i = lax.broadcasted_iota(jnp.int32, (FOX_STRIP, LANES), 1)
        for h in range(heads):
            after_head = before_head(h)
            sl = slice(h * dh, (h + 1) * dh)
            s = lax.dot_general(q_ref[:, h * aw:(h + 1) * aw], k_ref[:, h * aw:(h + 1) * aw], _NT,
                                preferred_element_type=F32)
            for r0 in range(0, tq, FOX_STRIP):
                rows = slice(r0, r0 + FOX_STRIP)
                live = [j for j in range(ncb) if not (diag and j * LANES > r0 + FOX_STRIP - 1)]
                blocks = []
                for j in live:
                    lg = s[rows, j * LANES:(j + 1) * LANES]
                    if diag and (j + 1) * LANES - 1 > r0:
                        lg = jnp.where(r_i + r0 >= c_i + j * LANES, lg, -jnp.inf)
                    blocks.append(lg)
                m_cur = blocks[0]
                for lg in blocks[1:]:
                    m_cur = jnp.maximum(m_cur, lg)
                m_prev = m_ref[h, rows]
                m_new = jnp.maximum(m_prev, jnp.max(m_cur, axis=-1, keepdims=True))
                alpha = jnp.exp2(m_prev - m_new)
                probs = [jnp.exp2(lg - m_new) for lg in blocks]
                row_sum = probs[0]
                for pj in probs[1:]:
                    row_sum = row_sum + pj
                l_ref[h, rows] = alpha * l_ref[h, rows] + jnp.sum(row_sum, axis=-1, keepdims=True)
                m_ref[h, rows] = m_new
                a_ref[rows] = alpha
                for j, pj in zip(live, probs):
                    p_ref[rows, j * LANES:(j + 1) * LANES] = pj.astype(BF16)
                for j in range(ncb):
                    if j not in live:
                        p_ref[rows, j * LANES:(j + 1) * LANES] = jnp.zeros((FOX_STRIP, LANES), BF16)
            acc_ref[:, sl] = a_ref[...] * acc_ref[:, sl] + jnp.dot(p_ref[...], v_ref[:, sl],
                                                                   preferred_element_type=F32)
            after_head()

    @pl.when(ki < qi)
    def _():
        step(False)

    @pl.when(ki == qi)
    def _():
        step(True)
        for h in range(heads):
            sl = slice(h * dh, (h + 1) * dh)
            o_ref[:, sl] = (acc_ref[:, sl] / l_ref[h] * _silu(gate_ref[:, sl])).astype(o_ref.dtype)


N_PROMPT_REFS = 4
N_SAMPLE_REFS = 9


def _fox_kernel(qi_ref, ki_ref, pt_ref, *refs, tq, heads, dh, sample_kw):
    prompt_in = refs[:N_PROMPT_REFS]
    sample_in = refs[N_PROMPT_REFS:N_PROMPT_REFS + N_SAMPLE_REFS]
    o_ref, os_ref = refs[N_PROMPT_REFS + N_SAMPLE_REFS:N_PROMPT_REFS + N_SAMPLE_REFS + 2]
    scratch = refs[N_PROMPT_REFS + N_SAMPLE_REFS + 2:]
    prompt_scratch, sample_scratch = scratch[:5], scratch[5:]
    p = pl.program_id(1)
    step = pl.program_id(0) * pl.num_programs(1) + p

    def before_head(h):
        return _fox_sample_unit(step * heads + h, pt_ref, *sample_in, os_ref, *sample_scratch, heads=heads, dh=dh,
                                **sample_kw)

    _fox_prompt_step(qi_ref[p], ki_ref[p], before_head, *prompt_in, o_ref, *prompt_scratch, tq=tq, heads=heads, dh=dh)


def _fox_attention(qa, ka, vb, gate, page_table, k_pages, v_pages, lf_pages, q_s, k_new, v_new, gate_s, cn_col,
                   cn_row, *, bsz, seq, tq, pages_per_unit):
    nq = seq // tq
    pairs = [(qi, ki) for qi in range(nq) for ki in range(qi + 1)]
    qi_tab = jnp.asarray(np.array([p[0] for p in pairs], np.int32))
    ki_tab = jnp.asarray(np.array([p[1] for p in pairs], np.int32))
    w = B_WIDTH
    dh = B_HEAD_DIM
    n_pool = lf_pages.shape[0]
    sfx = _page_suffix(lf_pages, B_HEADS).reshape(n_pool, 1, -1)
    db, n_pages = page_table.shape
    pp = pages_per_unit
    n_units = db * (n_pages // pp)
    assert bsz * len(pairs) * B_HEADS >= n_units, "not enough prompt steps to carry the sample page units"
    nrow = q_s.shape[1]
    pw = PAGE_SIZE * B_HEADS
    q_map = lambda b, p, qt, kt, pt: (b * nq + qt[p], 0)
    k_map = lambda b, p, qt, kt, pt: (b * nq + kt[p], 0)
    whole = lambda a: pl.BlockSpec(a.shape, lambda b, p, qt, kt, pt: (0,) * a.ndim)
    hbm = pl.BlockSpec(memory_space=pl.ANY)
    n_slots = 2
    grid_spec = pltpu.PrefetchScalarGridSpec(
        num_scalar_prefetch=3,
        grid=(bsz, len(pairs)),
        in_specs=[pl.BlockSpec((tq, 2 * w), q_map), pl.BlockSpec((tq, 2 * w), k_map),
                  pl.BlockSpec((tq, w), k_map), pl.BlockSpec((tq, w), q_map),
                  hbm, hbm, hbm, whole(q_s), whole(k_new), whole(v_new), whole(gate_s), whole(cn_col),
                  whole(cn_row)],
        out_specs=[pl.BlockSpec((tq, w), q_map),
                   pl.BlockSpec((db, nrow, dh), lambda b, p, qt, kt, pt: (0, 0, 0))],
        scratch_shapes=[pltpu.VMEM((B_HEADS, tq, LANES), F32), pltpu.VMEM((B_HEADS, tq, LANES), F32),
                        pltpu.VMEM((tq, w), F32), pltpu.VMEM((tq, tq), BF16), pltpu.VMEM((tq, LANES), F32),
                        pltpu.VMEM((n_slots, pp, pw, dh), F32), pltpu.VMEM((n_slots, pp, pw, dh), F32),
                        pltpu.VMEM((n_slots, pp, 1, 2 * pw), F32), pltpu.SemaphoreType.DMA((3, n_slots)),
                        pltpu.VMEM((nrow, 1), F32), pltpu.VMEM((nrow, 1), F32), pltpu.VMEM((nrow, dh), F32),
                        pltpu.VMEM((nrow, PAGE_SIZE), F32)],
    )
    sample_kw = dict(pp=pp, t_new=nrow // B_HEADS, n_pages=n_pages, n_units=n_units)
    return pl.pallas_call(
        functools.partial(_fox_kernel, tq=tq, heads=B_HEADS, dh=dh, sample_kw=sample_kw),
        grid_spec=grid_spec,
        out_shape=[jax.ShapeDtypeStruct((bsz * seq, w), BF16), jax.ShapeDtypeStruct((db, nrow, dh), F32)],
        compiler_params=_cparams(("arbitrary", "arbitrary")),
        name="fox_attention",
    )(qi_tab, ki_tab, page_table, qa, ka, vb, gate, k_pages, v_pages, sfx, q_s, k_new, v_new, gate_s, cn_col,
      cn_row)


def _page_suffix_kernel(lf_ref, later_ref, total_ref, out_ref):
    lf = lf_ref[...]
    pw = lf.shape[1]
    out_ref[:, 0:pw] = _x_dot01(lf, later_ref[...])
    out_ref[:, pw:2 * pw] = _x_dot01(lf, total_ref[...])


def _page_suffix(lf_pages, heads):
    n_pool, pw = lf_pages.shape
    tm = 512 if n_pool % 512 == 0 else n_pool
    j = np.arange(pw)
    in_head, in_tok = j[:, None] // (pw // heads), j[:, None] % (pw // heads)
    out_head, out_tok = j[None, :] // (pw // heads), j[None, :] % (pw // heads)
    head_eq = in_head == out_head
    later = jnp.asarray(head_eq & (in_tok > out_tok), BF16)
    total = jnp.asarray(head_eq, BF16)
    row = lambda i: (i, 0)
    const = lambda i: (0, 0)
    return pl.pallas_call(
        _page_suffix_kernel,
        grid=(n_pool // tm,),
        in_specs=[pl.BlockSpec((tm, pw), row), pl.BlockSpec((pw, pw), const), pl.BlockSpec((pw, pw), const)],
        out_specs=pl.BlockSpec((tm, 2 * pw), row),
        out_shape=jax.ShapeDtypeStruct((n_pool, 2 * pw), F32),
        compiler_params=_cparams(("arbitrary",)),
        name="page_suffix",
    )(lf_pages, later, total)


def _fox_sample_unit(u, pt_ref, k_hbm, v_hbm, sfx_hbm, q_ref, kn_ref, vn_ref, gate_ref, cn_col_ref, cn_row_ref, o_ref,
                     kbuf, vbuf, sbuf, sem, m_ref, l_ref, acc_ref, carry_ref, *, pp, heads, dh, t_new, n_pages,
                     n_units):
    nj = n_pages // pp
    valid = u < n_units
    uc = jnp.minimum(u, n_units - 1)
    b = lax.div(uc, nj)
    j = lax.rem(uc, nj)
    nrow = t_new * heads
    pw = PAGE_SIZE * heads
    slot = lax.rem(uc, 2)

    def page_copies(uu, sl):
        bb = lax.div(uu, nj)
        jj = lax.rem(uu, nj)
        copies = []
        for i in range(pp):
            page = pt_ref[bb, n_pages - 1 - (jj * pp + i)]
            for kind, (src, dst) in enumerate(((k_hbm, kbuf), (v_hbm, vbuf), (sfx_hbm, sbuf))):
                copies.append(pltpu.make_async_copy(src.at[page], dst.at[sl, i], sem.at[kind, sl]))
        return copies

    @pl.when(u == 0)
    def _():
        for cp in page_copies(u, slot):
            cp.start()

    @pl.when(u + 1 < n_units)
    def _():
        for cp in page_copies(u + 1, 1 - slot):
            cp.start()

    @pl.when(valid)
    def _():
        for cp in page_copies(u, slot):
            cp.wait()

    k_refs = [kbuf.at[slot, i] for i in range(pp)]
    v_refs = [vbuf.at[slot, i] for i in range(pp)]
    sfx_refs = [sbuf.at[slot, i] for i in range(pp)]

    @pl.when(j == 0)
    def _():
        m_ref[...] = jnp.full_like(m_ref, -jnp.inf)
        l_ref[...] = jnp.zeros_like(l_ref)
        acc_ref[...] = jnp.zeros_like(acc_ref)
        carry_ref[...] = jnp.zeros_like(carry_ref)

    q = q_ref[b]
    cn_col = cn_col_ref[b] * LOG2E
    row_head = lax.broadcasted_iota(jnp.int32, (nrow, dh), 0) % heads
    q_wide = jnp.concatenate([jnp.where(row_head == h, q, jnp.zeros_like(q)) for h in range(heads)], axis=1)

    def page_wide(ref):
        return jnp.concatenate([ref[pl.ds(h, PAGE_SIZE, stride=heads), :] for h in range(heads)],
                               axis=1).astype(BF16)

    def per_row_head(row):
        by_head = jnp.concatenate([row[:, h * PAGE_SIZE:(h + 1) * PAGE_SIZE] for h in range(heads)], axis=0)
        return jnp.concatenate([by_head] * t_new, axis=0)

    def own_head(wide):
        out = None
        for h in range(heads):
            part = jnp.where(row_head == h, wide[:, h * dh:(h + 1) * dh], 0.0)
            out = part if out is None else out + part
        return out

    def online(logit_list, weighted_values):
        m_cur = logit_list[0]
        for lg in logit_list[1:]:
            m_cur = jnp.maximum(m_cur, lg)
        m_prev = m_ref[...]
        m_new = jnp.maximum(m_prev, jnp.max(m_cur, axis=-1, keepdims=True))
        alpha = jnp.exp2(m_prev - m_new)
        probs = [jnp.exp2(lg - m_new) for lg in logit_list]
        row_sum = probs[0]
        for pj in probs[1:]:
            row_sum = row_sum + pj
        l_ref[...] = alpha * l_ref[...] + jnp.sum(row_sum, axis=-1, keepdims=True)
        acc_ref[...] = alpha * acc_ref[...] + weighted_values([pj.astype(BF16) for pj in probs])
        m_ref[...] = m_new

    logit_list = []
    carry = carry_ref[...]
    for i in range(pp):
        suffix = (carry + per_row_head(sfx_refs[i][:, 0:pw])) * LOG2E
        carry = carry + per_row_head(sfx_refs[i][:, pw:2 * pw])
        s = lax.dot_general(q_wide, page_wide(k_refs[i]), _NT, preferred_element_type=F32)
        logit_list.append(s + cn_col + suffix)
    carry_ref[...] = carry

    def page_values(probs):
        wide = None
        for pj, v_ref in zip(probs, v_refs):
            t = jnp.dot(pj, page_wide(v_ref), preferred_element_type=F32)
            wide = t if wide is None else wide + t
        return own_head(wide)

    online(logit_list, page_values)

    def finish():
        @pl.when(jnp.logical_and(valid, j == nj - 1))
        def _():
            s = lax.dot_general(q, kn_ref[b].astype(BF16), _NT, preferred_element_type=F32)
            logits = s + cn_col - cn_row_ref[b] * LOG2E
            r = lax.broadcasted_iota(jnp.int32, (nrow, nrow), 0)
            c = lax.broadcasted_iota(jnp.int32, (nrow, nrow), 1)
            keep = jnp.logical_and((r % heads) == (c % heads), (r // heads) >= (c // heads))
            online([jnp.where(keep, logits, -jnp.inf)],
                   lambda probs: jnp.dot(probs[0], vn_ref[b].astype(BF16), preferred_element_type=F32))
            o_ref[b] = acc_ref[...] / l_ref[...] * _silu(gate_ref[b])

    return finish


def _out_proj_kernel(*refs, n_in, final):
    ins = refs[:n_in]
    w_ref, x_ref = refs[n_in], refs[n_in + 1]
    y = x_ref[...]
    k0 = 0
    for a_ref in ins:
        kw = a_ref.shape[1]
        y = y + jnp.dot(a_ref[...].astype(BF16), w_ref[k0:k0 + kw, :], preferred_element_type=F32)
        k0 += kw
    if final:
        g_ref, o_ref = refs[n_in + 2], refs[n_in + 3]
        o_ref[...] = _rmsnorm_rows(y, g_ref[...])
    else:
        refs[n_in + 2][...] = y


def _out_proj(ins, w, x, final_gain, *, tm):
    m = x.shape[0]
    const = lambda i: (0, 0)
    row = lambda i: (i, 0)
    in_specs = [pl.BlockSpec((tm, a.shape[1]), row) for a in ins]
    in_specs += [pl.BlockSpec(w.shape, const), pl.BlockSpec((tm, D_MODEL), row)]
    operands = list(ins) + [w, x]
    if final_gain is not None:
        in_specs.append(pl.BlockSpec((1, D_MODEL), const))
        operands.append(final_gain)
    return pl.pallas_call(
        functools.partial(_out_proj_kernel, n_in=len(ins), final=final_gain is not None),
        grid=(m // tm,),
        in_specs=in_specs,
        out_specs=pl.BlockSpec((tm, D_MODEL), row),
        out_shape=jax.ShapeDtypeStruct((m, D_MODEL), F32),
        compiler_params=_cparams(("arbitrary",)),
        name="out_proj_final" if final_gain is not None else "out_proj",
    )(*operands)


def _pad_cols(w, n):
    return jnp.pad(w, ((0, 0), (0, n - w.shape[1])))


def _prep_weights(weights):
    (norm_even, w_in_even, b_fox_f, lb_logits, hgrn_gain, w_out_even, norm_odd, w_in_odd, w_gla_gate, b_gla_gate,
     gla_gain, w_out_odd, final_norm) = weights
    n_even = 4 * A_WIDTH + 4 * B_WIDTH
    n_odd = 2 * C_KEY_WIDTH + 2 * C_VAL_WIDTH
    w_even_t = w_in_even[0].T.astype(BF16)
    w_odd_t = w_in_odd[0].T.astype(BF16)
    pad_rows = lambda a: jnp.pad(a, ((0, LANES - a.shape[0]), (0, 0)))
    return dict(
        norm_even=norm_even[0].reshape(1, D_MODEL),
        w_even=w_even_t[:n_even],
        w_fb=pad_rows(w_even_t[n_even:]),
        b_fox=_pad_cols(b_fox_f[0].reshape(1, B_HEADS), LANES),
        lb_logits=lb_logits,
        hgrn_gain=hgrn_gain[0].reshape(1, A_WIDTH),
        w_out_even=w_out_even[0].astype(BF16),
        norm_odd=norm_odd[0].reshape(1, D_MODEL),
        w_odd=w_odd_t[:n_odd],
        w_r=pad_rows(w_odd_t[n_odd:]),
        w_gate=jnp.pad(w_gla_gate[0], ((0, LANES - C_GATE_RANK), (0, 0))).astype(BF16),
        b_gate=b_gla_gate[0].reshape(1, C_KEY_WIDTH),
        gla_gain=gla_gain[0].reshape(1, C_VAL_WIDTH),
        w_out_odd=w_out_odd[0].astype(BF16),
        final_norm=final_norm.reshape(1, D_MODEL),
    )


def _forward(x_prompt, x_sample, ctx, w):
    bsz, seq, _ = x_prompt.shape
    db, t_new, _ = x_sample.shape
    mp, ms = bsz * seq, db * t_new
    xp = x_prompt.reshape(mp, D_MODEL)
    xs = x_sample.reshape(ms, D_MODEL)
    hg = dict(heads=A_HEADS, dk=A_HEAD_DIM, dv=A_HEAD_DIM, q_scale=1.0)
    gl = dict(heads=C_HEADS, dk=C_KEY_DIM, dv=C_VAL_DIM, q_scale=C_KEY_DIM ** -0.5)

    even = lambda x2, s, tm: _even_proj(x2, w["norm_even"], w["w_even"], w["w_fb"], w["b_fox"], seq=s, tm=tm)
    pa_p, gate_p, qa_p, ka_p, vb_p, krows_p, vrows_p, lf_p, _ = even(xp, seq, 512)
    pa_s, gate_s, qa_s, _, _, krows_s, vrows_s, lf_s, c_s = even(xs, t_new, ms)

    oa_p, sa_p = _hgrn_prompt(pa_p, w["lb_logits"], w["hgrn_gain"], bsz=bsz, seq=seq, tb=512, **hg)
    oa_s, sa_s = _gla_sample(True, [pa_s] * 4, [0, 1, 2, 3], w["lb_logits"], w["hgrn_gain"], ctx["state_hgrn"],
                             t_new=t_new, **hg)
    nrow = t_new * B_HEADS
    rows = lambda a: a.reshape(db, nrow, B_HEAD_DIM)
    cn = c_s[:, :B_HEADS].reshape(db, nrow)
    q_rows = qa_s.reshape(db, nrow, 2 * B_HEAD_DIM)[:, :, :B_HEAD_DIM]
    ob_p, ob_s = _fox_attention(qa_p, ka_p, vb_p, gate_p, ctx["page_table"], ctx["k_pages"], ctx["v_pages"],
                                ctx["lf_pages"], q_rows, rows(krows_s), rows(vrows_s), rows(gate_s),
                                cn.reshape(db, nrow, 1), cn.reshape(db, 1, nrow), bsz=bsz, seq=seq,
                                tq=min(seq, 1024), pages_per_unit=8)

    y_p, sc_p = _layer1_prompt(xp, oa_p, ob_p, w, bsz=bsz, seq=seq, tb=512, **gl)
    x1_s = _out_proj([oa_s, ob_s.reshape(ms, B_WIDTH)], w["w_out_even"], xs, None, tm=ms)
    proj1, lf1 = _odd_proj(x1_s, w["norm_odd"], w["w_odd"], w["w_r"], w["w_gate"], w["b_gate"], tm=ms)
    oc_s, sc_s = _gla_sample(False, [proj1, proj1, proj1, proj1, lf1], [0, 1, 1, 2, 0], None, w["gla_gain"],
                             ctx["state_gla"], t_new=t_new, **gl)
    y_s = _out_proj([oc_s], w["w_out_odd"], x1_s, w["final_norm"], tm=ms)

    heads4 = lambda a, n, s: a.reshape(n, s, B_HEADS, B_HEAD_DIM)
    logf = lambda a, n, s: a[:, :B_HEADS].reshape(n, s, B_HEADS)
    prompt = (y_p.reshape(bsz, seq, D_MODEL), heads4(krows_p, bsz, seq), heads4(vrows_p, bsz, seq),
              logf(lf_p, bsz, seq), sa_p, sc_p)
    sample = (y_s.reshape(db, t_new, D_MODEL), heads4(krows_s, db, t_new), heads4(vrows_s, db, t_new),
              logf(lf_s, db, t_new), sa_s, sc_s)
    return prompt, sample


def kernel(x_prompt, x_sample, cache_fox_k, cache_fox_v, cache_fox_logf, state_hgrn, state_gla, page_table,
           norm_even, w_in_even, b_fox_f, lb_logits, hgrn_gain, w_out_even, norm_odd, w_in_odd, w_gla_gate,
           b_gla_gate, gla_gain, w_out_odd, final_norm):
    weights = _prep_weights((norm_even, w_in_even, b_fox_f, lb_logits, hgrn_gain, w_out_even, norm_odd, w_in_odd,
                             w_gla_gate, b_gla_gate, gla_gain, w_out_odd, final_norm))
    bsz, seq, _ = x_prompt.shape
    n_pool = cache_fox_k.shape[1]
    pw = PAGE_SIZE * B_HEADS
    ctx = dict(
        page_table=page_table,
        k_pages=cache_fox_k[0].reshape(n_pool, pw, B_HEAD_DIM),
        v_pages=cache_fox_v[0].reshape(n_pool, pw, B_HEAD_DIM),
        lf_pages=cache_fox_logf[0].transpose(0, 2, 1).reshape(n_pool, pw),
        state_hgrn=state_hgrn[0], state_gla=state_gla[0])
    (y_p, kp, vp, lfp, hgrn_p, gla_p), (y_s, ks, vs, lfs, hgrn_s, gla_s) = _forward(x_prompt, x_sample, ctx, weights)
    n_pp = seq // PAGE_SIZE
    fox_k_prompt = kp.reshape(1, bsz, n_pp, PAGE_SIZE, B_HEADS, B_HEAD_DIM)
    fox_v_prompt = vp.reshape(1, bsz, n_pp, PAGE_SIZE, B_HEADS, B_HEAD_DIM)
    fox_logf_prompt = lfp.reshape(1, bsz, n_pp, PAGE_SIZE, B_HEADS)
    return (y_p, y_s, fox_k_prompt, fox_v_prompt, fox_logf_prompt, hgrn_p[None], gla_p[None],
            ks[None], vs[None], lfs[None], hgrn_s[None], gla_s[None])
```

```python
import functools

import numpy as np
import jax
import jax.numpy as jnp
from jax import lax
from jax.experimental import pallas as pl
from jax.experimental.pallas import tpu as pltpu

F32 = jnp.float32
BF16 = jnp.bfloat16

D_MODEL = 1024
PAGE_SIZE = 128
A_HEADS = 4
A_HEAD_DIM = 128
A_WIDTH = 512
B_HEADS = 4
B_HEAD_DIM = 128
B_WIDTH = 512
C_HEADS = 4
C_KEY_WIDTH = 512
C_VAL_WIDTH = 1024
C_KEY_DIM = 128
C_VAL_DIM = 256
C_GATE_RANK = 16
GLA_GATE_NORMALIZER = 16.0
EPS = 1e-6
HGRN_LAYER = 0
LOG2E = 1.4426950408889634
AUG_PIECES = 3

LANES = 128
SUBLANES = 8
VMEM_LIMIT = 56 * 1024 * 1024
CHUNK = 64
MAX_CHUNK_LOG_DECAY = 60.0

_NT = (((1,), (1,)), ((), ()))
_TN = (((0,), (0,)), ((), ()))


def _cparams(sem):
    return pltpu.CompilerParams(dimension_semantics=sem, vmem_limit_bytes=VMEM_LIMIT)


def _sigmoid(x):
    return 1.0 / (1.0 + jnp.exp(-x))


def _log_sigmoid(x):
    return jnp.minimum(x, 0.0) - jnp.log1p(jnp.exp(-jnp.abs(x)))


def _silu(x):
    return x * _sigmoid(x)


def _rmsnorm_rows(x, g):
    return x * lax.rsqrt(jnp.mean(x * x, axis=-1, keepdims=True) + EPS) * g


def _split3(x):
    p1 = x.astype(BF16)
    r1 = x - p1.astype(F32)
    p2 = r1.astype(BF16)
    p3 = (r1 - p2.astype(F32)).astype(BF16)
    return p1, p2, p3


def _dot01(m01, x):
    acc = None
    for p in _split3(x):
        t = jnp.dot(m01, p, preferred_element_type=F32)
        acc = t if acc is None else acc + t
    return acc


def _x_dot01(x, m01):
    acc = None
    for p in _split3(x):
        t = jnp.dot(p, m01, preferred_element_type=F32)
        acc = t if acc is None else acc + t
    return acc


def _lower_tri(n, seq):
    r = lax.broadcasted_iota(jnp.int32, (n, n), 0)
    c = lax.broadcasted_iota(jnp.int32, (n, n), 1)
    keep = r >= c
    if seq < n:
        keep = jnp.logical_and(keep, (r // seq) == (c // seq))
    return jnp.where(keep, 1.0, 0.0).astype(BF16)


def _dot_wt(h, wt_ref, c0, n):
    return lax.dot_general(h, wt_ref[c0:c0 + n, :], _NT, preferred_element_type=F32)


def _proj_cols(h, wt_ref, out_ref):
    step = 512
    for c0 in range(0, wt_ref.shape[0], step):
        out_ref[:, c0:c0 + step] = _dot_wt(h, wt_ref, c0, step)


def _aug_selectors():
    sel_q = np.zeros((LANES, B_WIDTH), np.float32)
    sel_k = np.zeros((LANES, B_WIDTH), np.float32)
    for p in range(AUG_PIECES):
        for hd in range(B_HEADS):
            sel_q[p * B_HEADS + hd, hd * B_HEAD_DIM + p] = 1.0
            sel_k[p * B_HEADS + hd, hd * B_HEAD_DIM + AUG_PIECES + p] = -1.0
    return jnp.asarray(sel_q, BF16), jnp.asarray(sel_k, BF16)


def _even_proj_kernel(x_ref, g_ref, w_ref, wfb_ref, bfox_ref, selq_ref, selk_ref, pa_ref, gate_ref, qa_ref, ka_ref,
                      vb_ref, kout_ref, vout_ref, lf_ref, c_ref, carry_ref, *, tm, seq):
    i = pl.program_id(0)
    h = _rmsnorm_rows(x_ref[...], g_ref[...]).astype(BF16)
    na = 4 * A_WIDTH
    bw = B_WIDTH
    dh = B_HEAD_DIM
    for c0 in range(0, na, 512):
        pa_ref[:, c0:c0 + 512] = _dot_wt(h, w_ref, c0, 512)
    q = (_dot_wt(h, w_ref, na, bw) * (dh ** -0.5 * LOG2E)).astype(BF16)
    for hd in range(B_HEADS):
        qa_ref[:, 2 * hd * dh:(2 * hd + 1) * dh] = q[:, hd * dh:(hd + 1) * dh]
    for j, out_ref in ((1, kout_ref), (2, vout_ref)):
        kv = _dot_wt(h, w_ref, na + j * bw, bw)
        kv16 = kv.astype(BF16)
        if j == 1:
            for hd in range(B_HEADS):
                ka_ref[:, 2 * hd * dh:(2 * hd + 1) * dh] = kv16[:, hd * dh:(hd + 1) * dh]
        else:
            vb_ref[...] = kv16
        for hd in range(B_HEADS):
            out_ref[pl.ds(hd, tm, stride=B_HEADS), :] = kv[:, hd * dh:(hd + 1) * dh]
    gate_ref[...] = _dot_wt(h, w_ref, na + 3 * bw, bw)
    fb = _dot_wt(h, wfb_ref, 0, LANES) + bfox_ref[...]
    lane = lax.broadcasted_iota(jnp.int32, fb.shape, 1)
    lf = jnp.where(lane < B_HEADS, _log_sigmoid(fb), 0.0)
    lf_ref[...] = lf
    cs = _dot01(_lower_tri(tm, seq), lf)
    if seq > tm:
        @pl.when((i * tm) % seq == 0)
        def _():
            carry_ref[...] = jnp.zeros_like(carry_ref)
        cs = cs + carry_ref[0:1, :]
        carry_ref[0:1, :] = cs[tm - 1:tm, :]
    c_ref[...] = cs
    pos = lax.broadcasted_iota(jnp.int32, (1, bw), 1) % dh
    aug_q = jnp.where(jnp.logical_and(pos >= AUG_PIECES, pos < 2 * AUG_PIECES), 1.0, 0.0)
    aug_k = jnp.where(pos < AUG_PIECES, 1.0, 0.0)
    packed = None
    for p, piece in enumerate(_split3(cs * LOG2E)):
        shifted = piece.astype(F32) if p == 0 else pltpu.roll(piece.astype(F32), p * B_HEADS, 1)
        packed = shifted if packed is None else packed + shifted
    packed = packed.astype(BF16)
    aug_q = (aug_q + jnp.dot(packed, selq_ref[...], preferred_element_type=F32)).astype(BF16)
    aug_k = (aug_k + jnp.dot(packed, selk_ref[...], preferred_element_type=F32)).astype(BF16)
    for hd in range(B_HEADS):
        qa_ref[:, (2 * hd + 1) * dh:(2 * hd + 2) * dh] = aug_q[:, hd * dh:(hd + 1) * dh]
        ka_ref[:, (2 * hd + 1) * dh:(2 * hd + 2) * dh] = aug_k[:, hd * dh:(hd + 1) * dh]


def _even_proj(x, g, w_main, w_fb, b_fox, *, seq, tm):
    m = x.shape[0]
    const = lambda i: (0, 0)
    row = lambda i: (i, 0)
    sel_q, sel_k = _aug_selectors()
    return pl.pallas_call(
        functools.partial(_even_proj_kernel, tm=tm, seq=seq),
        grid=(m // tm,),
        in_specs=[pl.BlockSpec((tm, D_MODEL), row), pl.BlockSpec((1, D_MODEL), const),
                  pl.BlockSpec(w_main.shape, const), pl.BlockSpec(w_fb.shape, const),
                  pl.BlockSpec((1, LANES), const), pl.BlockSpec(sel_q.shape, const),
                  pl.BlockSpec(sel_k.shape, const)],
        out_specs=[pl.BlockSpec((tm, 4 * A_WIDTH), row), pl.BlockSpec((tm, B_WIDTH), row),
                   pl.BlockSpec((tm, 2 * B_WIDTH), row), pl.BlockSpec((tm, 2 * B_WIDTH), row),
                   pl.BlockSpec((tm, B_WIDTH), row), pl.BlockSpec((tm * B_HEADS, B_HEAD_DIM), row),
                   pl.BlockSpec((tm * B_HEADS, B_HEAD_DIM), row), pl.BlockSpec((tm, LANES), row),
                   pl.BlockSpec((tm, LANES), row)],
        out_shape=[jax.ShapeDtypeStruct((m, 4 * A_WIDTH), F32), jax.ShapeDtypeStruct((m, B_WIDTH), F32),
                   jax.ShapeDtypeStruct((m, 2 * B_WIDTH), BF16), jax.ShapeDtypeStruct((m, 2 * B_WIDTH), BF16),
                   jax.ShapeDtypeStruct((m, B_WIDTH), BF16),
                   jax.ShapeDtypeStruct((m * B_HEADS, B_HEAD_DIM), F32),
                   jax.ShapeDtypeStruct((m * B_HEADS, B_HEAD_DIM), F32),
                   jax.ShapeDtypeStruct((m, LANES), F32), jax.ShapeDtypeStruct((m, LANES), F32)],
        scratch_shapes=[pltpu.VMEM((SUBLANES, LANES), F32)],
        compiler_params=_cparams(("arbitrary",)),
        name="even_proj",
    )(x, g, w_main, w_fb, b_fox, sel_q, sel_k)


def _odd_proj_kernel(x_ref, g_ref, w_ref, wr_ref, wg_ref, bg_ref, proj_ref, lf_ref):
    h = _rmsnorm_rows(x_ref[...], g_ref[...]).astype(BF16)
    _proj_cols(h, w_ref, proj_ref)
    r = _dot_wt(h, wr_ref, 0, LANES)
    z = jnp.dot(r.astype(BF16), wg_ref[...], preferred_element_type=F32) + bg_ref[...]
    lf_ref[...] = _log_sigmoid(z) / GLA_GATE_NORMALIZER


def _odd_proj(x, g, w_main, w_r, w_gate, b_gate, *, tm):
    m = x.shape[0]
    n = w_main.shape[0]
    const = lambda i: (0, 0)
    row = lambda i: (i, 0)
    return pl.pallas_call(
        _odd_proj_kernel,
        grid=(m // tm,),
        in_specs=[pl.BlockSpec((tm, D_MODEL), row), pl.BlockSpec((1, D_MODEL), const),
                  pl.BlockSpec(w_main.shape, const), pl.BlockSpec(w_r.shape, const),
                  pl.BlockSpec((LANES, C_KEY_WIDTH), const), pl.BlockSpec((1, C_KEY_WIDTH), const)],
        out_specs=[pl.BlockSpec((tm, n), row), pl.BlockSpec((tm, C_KEY_WIDTH), row)],
        out_shape=[jax.ShapeDtypeStruct((m, n), F32), jax.ShapeDtypeStruct((m, C_KEY_WIDTH), F32)],
        compiler_params=_cparams(("arbitrary",)),
        name="odd_proj",
    )(x, g, w_main, w_r, w_gate, b_gate)


def _lower_bound(logits, layer):
    e = jnp.exp(logits - jnp.max(logits, axis=0, keepdims=True))
    return jnp.sum(e[:layer + 1, :], axis=0, keepdims=True) / jnp.sum(e, axis=0, keepdims=True)


def _hgrn_gate(fa, lb):
    f = lb + (1.0 - lb) * _sigmoid(fa)
    return jnp.log(f), 1.0 - f


def _exact_group(q, k, v, g, st, lo, hi):
    n = SUBLANES
    row = lax.broadcasted_iota(jnp.int32, (n, 1), 0)
    valid = jnp.logical_and(row >= lo, row < hi)
    q = jnp.where(valid, q, 0.0)
    k = jnp.where(valid, k, 0.0)
    g = jnp.where(valid, g, 0.0)
    b = g
    for sh in (1, 2, 4):
        b = b + jnp.where(row >= sh, pltpu.roll(b, sh, 0), 0.0)
    o = lax.dot_general((q * jnp.exp(b)).astype(BF16), st.astype(BF16), _NT, preferred_element_type=F32)
    for s in range(lo, hi):
        w = jnp.exp(jnp.minimum(b - b[s:s + 1, :], 0.0))
        a = jnp.sum(q * k[s:s + 1, :] * w, axis=-1, keepdims=True)
        o = o + jnp.where(row >= s, a, 0.0) * v[s:s + 1, :]
    b_last = b[n - 1:n, :]
    k_hat = k * jnp.exp(b_last - b)
    st_new = st * jnp.exp(b_last) + lax.dot_general(v.astype(BF16), k_hat.astype(BF16), _TN,
                                                    preferred_element_type=F32)
    return o, st_new


def _head_norm_gate(o, gain, gate):
    y = o * lax.rsqrt(jnp.mean(o * o, axis=-1, keepdims=True) + EPS) * gain
    return y * _silu(gate)


def _recurrence_block(q_ref, k_ref, v_ref, gate_ref, g_ref, gain_ref, o_ref, st_ref, b_ref, *, fa_ref, lb, heads, dk,
                      dv, tb, q_scale):
    n_chunks = tb // CHUNK

    tri = _lower_tri(CHUNK, CHUNK)
    b_min = None
    for c in range(n_chunks):
        rows = slice(c * CHUNK, (c + 1) * CHUNK)
        if fa_ref is not None:
            g, k = _hgrn_gate(fa_ref[rows, :], lb)
            g_ref[rows, :] = g
            k_ref[rows, :] = k
        else:
            g = g_ref[rows, :]
        b = _dot01(tri, g)
        b_ref[rows, :] = b
        b_last = b[CHUNK - 1:CHUNK, :]
        b_min = b_last if b_min is None else jnp.minimum(b_min, b_last)
    chunk_form_ok = jnp.min(b_min) >= -MAX_CHUNK_LOG_DECAY

    def finish(rows, h, o_h):
        vsl = slice(h * dv, (h + 1) * dv)
        o_ref[rows, vsl] = _head_norm_gate(o_h, gain_ref[:, vsl], gate_ref[rows, vsl]).astype(o_ref.dtype)

    @pl.when(chunk_form_ok)
    def _():
        r_i = lax.broadcasted_iota(jnp.int32, (CHUNK, CHUNK), 0)
        c_i = lax.broadcasted_iota(jnp.int32, (CHUNK, CHUNK), 1)
        causal = r_i >= c_i

        for c in range(n_chunks):
            rows = slice(c * CHUNK, (c + 1) * CHUNK)
            b = b_ref[rows, :]
            e_b = jnp.exp(b)
            b_last = b[CHUNK - 1:CHUNK, :]
            e_last = jnp.exp(b_last)
            q_t = q_ref[rows, :] * q_scale * e_b
            k_t = k_ref[rows, :] * jnp.exp(-b)
            k_hat = k_t * e_last
            for h in range(heads):
                ksl = slice(h * dk, (h + 1) * dk)
                vsl = slice(h * dv, (h + 1) * dv)
                qh = q_t[:, ksl].astype(BF16)
                vh = v_ref[rows, vsl].astype(BF16)
                a = lax.dot_general(qh, k_t[:, ksl].astype(BF16), _NT, preferred_element_type=F32)
                a = jnp.where(causal, a, 0.0).astype(BF16)
                st = st_ref[h]
                o_h = jnp.dot(a, vh, preferred_element_type=F32) + lax.dot_general(
                    qh, st.astype(BF16), _NT, preferred_element_type=F32)
                st_ref[h] = st * e_last[:, ksl] + lax.dot_general(
                    vh, k_hat[:, ksl].astype(BF16), _TN, preferred_element_type=F32)
                finish(rows, h, o_h)

    @pl.when(jnp.logical_not(chunk_form_ok))
    def _():
        pair = 2 * SUBLANES

        def group(i, carry):
            for h in range(heads):
                ksl = slice(h * dk, (h + 1) * dk)
                vsl = slice(h * dv, (h + 1) * dv)
                outs = []
                for half in range(2):
                    rows = pl.ds(pl.multiple_of(i * pair + half * SUBLANES, SUBLANES), SUBLANES)
                    o_h, st_new = _exact_group(q_ref[rows, ksl] * q_scale, k_ref[rows, ksl], v_ref[rows, vsl],
                                               g_ref[rows, ksl], st_ref[h], 0, SUBLANES)
                    st_ref[h] = st_new
                    outs.append(o_h)
                finish(pl.ds(pl.multiple_of(i * pair, pair), pair), h, jnp.concatenate(outs, axis=0))
            return carry

        lax.fori_loop(0, tb // pair, group, 0)


def _state_step_edges(st_ref, s_ref):
    t = pl.program_id(1)

    def first():
        @pl.when(t == 0)
        def _():
            st_ref[...] = jnp.zeros_like(st_ref)

    def last():
        @pl.when(t == pl.num_programs(1) - 1)
        def _():
            for h in range(st_ref.shape[0]):
                s_ref[0, h] = st_ref[h].T

    return first, last


def _hgrn_prompt_kernel(q_ref, fa_ref, v_ref, gate_ref, lb_ref, gain_ref, o_ref, s_ref, st_ref, b_ref, g_ref, k_ref,
                        **kw):
    first, last = _state_step_edges(st_ref, s_ref)
    first()
    _recurrence_block(q_ref, k_ref, v_ref, gate_ref, g_ref, gain_ref, o_ref, st_ref, b_ref, fa_ref=fa_ref,
                      lb=_lower_bound(lb_ref[...], HGRN_LAYER), **kw)
    last()


def _hgrn_prompt(pa, lb_logits, gain, *, bsz, seq, heads, dk, dv, tb, q_scale):
    nt = seq // tb
    kw, vw = heads * dk, heads * dv
    col = lambda cb: (lambda b, t: (b * nt + t, cb))
    const = lambda b, t: (0, 0)
    return pl.pallas_call(
        functools.partial(_hgrn_prompt_kernel, heads=heads, dk=dk, dv=dv, tb=tb, q_scale=q_scale),
        grid=(bsz, nt),
        in_specs=[pl.BlockSpec((tb, kw), col(0)), pl.BlockSpec((tb, kw), col(1)), pl.BlockSpec((tb, vw), col(2)),
                  pl.BlockSpec((tb, vw), col(3)), pl.BlockSpec(lb_logits.shape, const),
                  pl.BlockSpec((1, vw), const)],
        out_specs=[pl.BlockSpec((tb, vw), col(0)), pl.BlockSpec((1, heads, dk, dv), lambda b, t: (b, 0, 0, 0))],
        out_shape=[jax.ShapeDtypeStruct((bsz * seq, vw), BF16), jax.ShapeDtypeStruct((bsz, heads, dk, dv), F32)],
        scratch_shapes=[pltpu.VMEM((heads, dv, dk), F32), pltpu.VMEM((tb, kw), F32), pltpu.VMEM((tb, kw), F32),
                        pltpu.VMEM((tb, kw), F32)],
        compiler_params=_cparams(("arbitrary", "arbitrary")),
        name="hgrn_prompt",
    )(pa, pa, pa, pa, lb_logits, gain)


def _layer1_prompt_kernel(x_ref, oa_ref, ob_ref, woe_ref, gn_ref, w_ref, wr_ref, wg_ref, bg_ref, gain_ref, woo_ref,
                          fn_ref, y_ref, s_ref, st_ref, b_ref, proj_ref, lf_ref, o_ref, *, heads, dk, dv, tb,
                          q_scale):
    first, last = _state_step_edges(st_ref, s_ref)
    first()
    kw, vw = heads * dk, heads * dv
    aw = oa_ref.shape[1]
    x1 = (x_ref[...] + jnp.dot(oa_ref[...], woe_ref[0:aw, :], preferred_element_type=F32)
          + jnp.dot(ob_ref[...], woe_ref[aw:, :], preferred_element_type=F32))
    h = _rmsnorm_rows(x1, gn_ref[...]).astype(BF16)
    _proj_cols(h, w_ref, proj_ref)
    r = _dot_wt(h, wr_ref, 0, LANES)
    z = jnp.dot(r.astype(BF16), wg_ref[...], preferred_element_type=F32) + bg_ref[...]
    lf_ref[...] = _log_sigmoid(z) / GLA_GATE_NORMALIZER
    _recurrence_block(proj_ref.at[:, 0:kw], proj_ref.at[:, kw:2 * kw], proj_ref.at[:, 2 * kw:2 * kw + vw],
                      proj_ref.at[:, 2 * kw + vw:2 * kw + 2 * vw], lf_ref, gain_ref, o_ref, st_ref, b_ref,
                      fa_ref=None, lb=None, heads=heads, dk=dk, dv=dv, tb=tb, q_scale=q_scale)
    y = x1 + jnp.dot(o_ref[...], woo_ref[...], preferred_element_type=F32)
    y_ref[...] = _rmsnorm_rows(y, fn_ref[...])
    last()


def _layer1_prompt(x, o_a, o_b, w, *, bsz, seq, heads, dk, dv, tb, q_scale):
    nt = seq // tb
    kw, vw = heads * dk, heads * dv
    row = lambda b, t: (b * nt + t, 0)
    const = lambda b, t: (0, 0)
    full = lambda a: pl.BlockSpec(a.shape, const)
    weights = [w["w_out_even"], w["norm_odd"], w["w_odd"], w["w_r"], w["w_gate"], w["b_gate"], w["gla_gain"],
               w["w_out_odd"], w["final_norm"]]
    return pl.pallas_call(
        functools.partial(_layer1_prompt_kernel, heads=heads, dk=dk, dv=dv, tb=tb, q_scale=q_scale),
        grid=(bsz, nt),
        in_specs=[pl.BlockSpec((tb, D_MODEL), row), pl.BlockSpec((tb, o_a.shape[1]), row),
                  pl.BlockSpec((tb, o_b.shape[1]), row)] + [full(a) for a in weights],
        out_specs=[pl.BlockSpec((tb, D_MODEL), row), pl.BlockSpec((1, heads, dk, dv), lambda b, t: (b, 0, 0, 0))],
        out_shape=[jax.ShapeDtypeStruct((bsz * seq, D_MODEL), F32),
                   jax.ShapeDtypeStruct((bsz, heads, dk, dv), F32)],
        scratch_shapes=[pltpu.VMEM((heads, dv, dk), F32), pltpu.VMEM((tb, kw), F32),
                        pltpu.VMEM((tb, 2 * kw + 2 * vw), F32), pltpu.VMEM((tb, kw), F32),
                        pltpu.VMEM((tb, vw), BF16)],
        compiler_params=_cparams(("arbitrary", "arbitrary")),
        name="layer1_prompt",
    )(x, o_a, o_b, *weights)


SAMPLE_GROUPS = 1


def _gla_sample_kernel(*refs, hgrn, heads, dk, dv, t_new, q_scale, groups):
    if hgrn:
        q_ref, fa_ref, v_ref, gate_ref, lb_ref, gain_ref, s0_ref, o_ref, s_ref = refs
    else:
        q_ref, k_ref, v_ref, gate_ref, g_ref, gain_ref, s0_ref, o_ref, s_ref = refs
    per = SUBLANES // t_new
    for grp in range(groups):
        rows = slice(grp * SUBLANES, (grp + 1) * SUBLANES)
        for h in range(heads):
            ksl = slice(h * dk, (h + 1) * dk)
            vsl = slice(h * dv, (h + 1) * dv)
            if hgrn:
                g, k = _hgrn_gate(fa_ref[rows, ksl], _lower_bound(lb_ref[...], HGRN_LAYER)[:, ksl])
            else:
                g, k = g_ref[rows, ksl], k_ref[rows, ksl]
            q = q_ref[rows, ksl] * q_scale
            v = v_ref[rows, vsl]
            o_h = None
            for e in range(per):
                o_e, st_new = _exact_group(q, k, v, g, s0_ref[grp * per + e, h].T, e * t_new, (e + 1) * t_new)
                s_ref[grp * per + e, h] = st_new.T
                o_h = o_e if o_h is None else o_h + o_e
            o_ref[rows, vsl] = _head_norm_gate(o_h, gain_ref[:, vsl], gate_ref[rows, vsl])


def _gla_sample(hgrn, arrays, col_blocks, small, gain, s0, *, t_new, heads, dk, dv, q_scale):
    m = arrays[0].shape[0]
    groups = SAMPLE_GROUPS if m % (SAMPLE_GROUPS * SUBLANES) == 0 else 1
    tr = groups * SUBLANES
    per = tr // t_new
    kw, vw = heads * dk, heads * dv
    widths = [kw, kw, vw, vw] + ([] if hgrn else [kw])
    in_specs = [pl.BlockSpec((tr, w), functools.partial(lambda i, cb: (i, cb), cb=cb))
                for w, cb in zip(widths, col_blocks)]
    operands = list(arrays)
    if hgrn:
        in_specs.append(pl.BlockSpec(small.shape, lambda i: (0, 0)))
        operands.append(small)
    in_specs.append(pl.BlockSpec((1, vw), lambda i: (0, 0)))
    operands.append(gain)
    in_specs.append(pl.BlockSpec((per, heads, dk, dv), lambda i: (i, 0, 0, 0)))
    operands.append(s0)
    return pl.pallas_call(
        functools.partial(_gla_sample_kernel, hgrn=hgrn, heads=heads, dk=dk, dv=dv, t_new=t_new, q_scale=q_scale,
                          groups=groups),
        grid=(m // tr,),
        in_specs=in_specs,
        out_specs=[pl.BlockSpec((tr, vw), lambda i: (i, 0)),
                   pl.BlockSpec((per, heads, dk, dv), lambda i: (i, 0, 0, 0))],
        out_shape=[jax.ShapeDtypeStruct((m, vw), F32), jax.ShapeDtypeStruct(s0.shape, F32)],
        compiler_params=_cparams(("arbitrary",)),
        name="hgrn_sample" if hgrn else "gla_sample",
    )(*operands)


FOX_STRIP = 64


def _fox_prompt_step(qi, ki, before_head, q_ref, k_ref, v_ref, gate_ref, o_ref, m_ref, l_ref, acc_ref, p_ref, a_ref,
                     *, tq, heads, dh):
    ncb = tq // LANES
    aw = 2 * dh

    @pl.when(ki == 0)
    def _():
        m_ref[...] = jnp.full_like(m_ref, -jnp.inf)
        l_ref[...] = jnp.zeros_like(l_ref)
        acc_ref[...] = jnp.zeros_like(acc_ref)

    def step(diag):
        if diag:
            r_i = lax.broadcasted_iota(jnp.int32, (FOX_STRIP, LANES), 0)
            c_i = lax.broadcasted_iota(jnp.int32, (FOX_STRIP, LANES), 1)
        for h in range(heads):
            after_head = before_head(h)
            sl = slice(h * dh, (h + 1) * dh)
            s = lax.dot_general(q_ref[:, h * aw:(h + 1) * aw], k_ref[:, h * aw:(h + 1) * aw], _NT,
                                preferred_element_type=F32)
            for r0 in range(0, tq, FOX_STRIP):
                rows = slice(r0, r0 + FOX_STRIP)
                live = [j for j in range(ncb) if not (diag and j * LANES > r0 + FOX_STRIP - 1)]
                blocks = []
                for j in live:
                    lg = s[rows, j * LANES:(j + 1) * LANES]
                    if diag and (j + 1) * LANES - 1 > r0:
                        lg = jnp.where(r_i + r0 >= c_i + j * LANES, lg, -jnp.inf)
                    blocks.append(lg)
                m_cur = blocks[0]
                for lg in blocks[1:]:
                    m_cur = jnp.maximum(m_cur, lg)
                m_prev = m_ref[h, rows]
                m_new = jnp.maximum(m_prev, jnp.max(m_cur, axis=-1, keepdims=True))
                alpha = jnp.exp2(m_prev - m_new)
                probs = [jnp.exp2(lg - m_new) for lg in blocks]
                row_sum = probs[0]
                for pj in probs[1:]:
                    row_sum = row_sum + pj
                l_ref[h, rows] = alpha * l_ref[h, rows] + jnp.sum(row_sum, axis=-1, keepdims=True)
                m_ref[h, rows] = m_new
                a_ref[rows] = alpha
                for j, pj in zip(live, probs):
                    p_ref[rows, j * LANES:(j + 1) * LANES] = pj.astype(BF16)
                for j in range(ncb):
                    if j not in live:
                        p_ref[rows, j * LANES:(j + 1) * LANES] = jnp.zeros((FOX_STRIP, LANES), BF16)
            acc_ref[:, sl] = a_ref[...] * acc_ref[:, sl] + jnp.dot(p_ref[...], v_ref[:, sl],
                                                                   preferred_element_type=F32)
            after_head()

    @pl.when(ki < qi)
    def _():
        step(False)

    @pl.when(ki == qi)
    def _():
        step(True)
        for h in range(heads):
            sl = slice(h * dh, (h + 1) * dh)
            o_ref[:, sl] = (acc_ref[:, sl] / l_ref[h] * _silu(gate_ref[:, sl])).astype(o_ref.dtype)


N_PROMPT_REFS = 4
N_SAMPLE_REFS = 9


def _fox_kernel(qi_ref, ki_ref, pt_ref, *refs, tq, heads, dh, sample_kw):
    prompt_in = refs[:N_PROMPT_REFS]
    sample_in = refs[N_PROMPT_REFS:N_PROMPT_REFS + N_SAMPLE_REFS]
    o_ref, os_ref = refs[N_PROMPT_REFS + N_SAMPLE_REFS:N_PROMPT_REFS + N_SAMPLE_REFS + 2]
    scratch = refs[N_PROMPT_REFS + N_SAMPLE_REFS + 2:]
    prompt_scratch, sample_scratch = scratch[:5], scratch[5:]
    p = pl.program_id(1)
    step = pl.program_id(0) * pl.num_programs(1) + p

    def before_head(h):
        return _fox_sample_unit(step * heads + h, pt_ref, *sample_in, os_ref, *sample_scratch, heads=heads, dh=dh,
                                **sample_kw)

    _fox_prompt_step(qi_ref[p], ki_ref[p], before_head, *prompt_in, o_ref, *prompt_scratch, tq=tq, heads=heads, dh=dh)


def _fox_attention(qa, ka, vb, gate, page_table, k_pages, v_pages, lf_pages, q_s, k_new, v_new, gate_s, cn_col,
                   cn_row, *, bsz, seq, tq, pages_per_unit):
    nq = seq // tq
    pairs = [(qi, ki) for qi in range(nq) for ki in range(qi + 1)]
    qi_tab = jnp.asarray(np.array([p[0] for p in pairs], np.int32))
    ki_tab = jnp.asarray(np.array([p[1] for p in pairs], np.int32))
    w = B_WIDTH
    dh = B_HEAD_DIM
    n_pool = lf_pages.shape[0]
    sfx = _page_suffix(lf_pages, B_HEADS)
    db, n_pages = page_table.shape
    pp = pages_per_unit
    n_units = db * (n_pages // pp)
    assert bsz * len(pairs) * B_HEADS >= n_units, "not enough prompt steps to carry the sample page units"
    nrow = q_s.shape[1]
    pw = PAGE_SIZE * B_HEADS
    q_map = lambda b, p, qt, kt, pt: (b * nq + qt[p], 0)
    k_map = lambda b, p, qt, kt, pt: (b * nq + kt[p], 0)
    whole = lambda a: pl.BlockSpec(a.shape, lambda b, p, qt, kt, pt: (0,) * a.ndim)
    hbm = pl.BlockSpec(memory_space=pl.ANY)
    n_slots = 2
    grid_spec = pltpu.PrefetchScalarGridSpec(
        num_scalar_prefetch=3,
        grid=(bsz, len(pairs)),
        in_specs=[pl.BlockSpec((tq, 2 * w), q_map), pl.BlockSpec((tq, 2 * w), k_map),
                  pl.BlockSpec((tq, w), k_map), pl.BlockSpec((tq, w), q_map),
                  hbm, hbm, hbm, whole(q_s), whole(k_new), whole(v_new), whole(gate_s), whole(cn_col),
                  whole(cn_row)],
        out_specs=[pl.BlockSpec((tq, w), q_map),
                   pl.BlockSpec((db, nrow, dh), lambda b, p, qt, kt, pt: (0, 0, 0))],
        scratch_shapes=[pltpu.VMEM((B_HEADS, tq, LANES), F32), pltpu.VMEM((B_HEADS, tq, LANES), F32),
                        pltpu.VMEM((tq, w), F32), pltpu.VMEM((tq, tq), BF16), pltpu.VMEM((tq, LANES), F32),
                        pltpu.VMEM((n_slots, pp, pw, dh), F32), pltpu.VMEM((n_slots, pp, pw, dh), F32),
                        pltpu.VMEM((n_slots, pp, 1, 2 * pw), F32), pltpu.SemaphoreType.DMA((3, n_slots)),
                        pltpu.VMEM((nrow, 1), F32), pltpu.VMEM((nrow, 1), F32), pltpu.VMEM((nrow, dh), F32),
                        pltpu.VMEM((nrow, PAGE_SIZE), F32)],
    )
    sample_kw = dict(pp=pp, t_new=nrow // B_HEADS, n_pages=n_pages, n_units=n_units)
    return pl.pallas_call(
        functools.partial(_fox_kernel, tq=tq, heads=B_HEADS, dh=dh, sample_kw=sample_kw),
        grid_spec=grid_spec,
        out_shape=[jax.ShapeDtypeStruct((bsz * seq, w), BF16), jax.ShapeDtypeStruct((db, nrow, dh), F32)],
        compiler_params=_cparams(("arbitrary", "arbitrary")),
        name="fox_attention",
    )(qi_tab, ki_tab, page_table, qa, ka, vb, gate, k_pages, v_pages, sfx, q_s, k_new, v_new, gate_s, cn_col,
      cn_row)


def _page_suffix_kernel(lf_ref, later_ref, total_ref, out_ref):
    lf = lf_ref[...]
    pw = lf.shape[1]
    out_ref[:, 0:pw] = _x_dot01(lf, later_ref[...])
    out_ref[:, pw:2 * pw] = _x_dot01(lf, total_ref[...])


def _page_suffix(lf_pages, heads):
    n_pool, pw = lf_pages.shape
    tm = 512 if n_pool % 512 == 0 else n_pool
    j = np.arange(pw)
    in_head, in_tok = j[:, None] // (pw // heads), j[:, None] % (pw // heads)
    out_head, out_tok = j[None, :] // (pw // heads), j[None, :] % (pw // heads)
    head_eq = in_head == out_head
    later = jnp.asarray(head_eq & (in_tok > out_tok), BF16)
    total = jnp.asarray(head_eq, BF16)
    row = lambda i: (i, 0)
    const = lambda i: (0, 0)
    return pl.pallas_call(
        _page_suffix_kernel,
        grid=(n_pool // tm,),
        in_specs=[pl.BlockSpec((tm, pw), row), pl.BlockSpec((pw, pw), const), pl.BlockSpec((pw, pw), const)],
        out_specs=pl.BlockSpec((tm, 2 * pw), row),
        out_shape=jax.ShapeDtypeStruct((n_pool, 2 * pw), F32),
        compiler_params=_cparams(("arbitrary",)),
        name="page_suffix",
    )(lf_pages, later, total)


def _fox_sample_unit(u, pt_ref, k_hbm, v_hbm, sfx_hbm, q_ref, kn_ref, vn_ref, gate_ref, cn_col_ref, cn_row_ref, o_ref,
                     kbuf, vbuf, sbuf, sem, m_ref, l_ref, acc_ref, carry_ref, *, pp, heads, dh, t_new, n_pages,
                     n_units):
    nj = n_pages // pp
    valid = u < n_units
    uc = jnp.minimum(u, n_units - 1)
    b = lax.div(uc, nj)
    j = lax.rem(uc, nj)
    nrow = t_new * heads
    pw = PAGE_SIZE * heads
    slot = lax.rem(uc, 2)

    def page_copies(uu, sl):
        bb = lax.div(uu, nj)
        jj = lax.rem(uu, nj)
        copies = []
        for i in range(pp):
            page = pt_ref[bb, n_pages - 1 - (jj * pp + i)]
            sources = (k_hbm.at[page], v_hbm.at[page], sfx_hbm.at[pl.ds(page, 1)])
            for kind, (src, dst) in enumerate(zip(sources, (kbuf, vbuf, sbuf))):
                copies.append(pltpu.make_async_copy(src, dst.at[sl, i], sem.at[kind, sl]))
        return copies

    @pl.when(u == 0)
    def _():
        for cp in page_copies(u, slot):
            cp.start()

    @pl.when(u + 1 < n_units)
    def _():
        for cp in page_copies(u + 1, 1 - slot):
            cp.start()

    @pl.when(valid)
    def _():
        for cp in page_copies(u, slot):
            cp.wait()

    k_refs = [kbuf.at[slot, i] for i in range(pp)]
    v_refs = [vbuf.at[slot, i] for i in range(pp)]
    sfx_refs = [sbuf.at[slot, i] for i in range(pp)]

    @pl.when(j == 0)
    def _():
        m_ref[...] = jnp.full_like(m_ref, -jnp.inf)
        l_ref[...] = jnp.zeros_like(l_ref)
        acc_ref[...] = jnp.zeros_like(acc_ref)
        carry_ref[...] = jnp.zeros_like(carry_ref)

    q = q_ref[b]
    cn_col = cn_col_ref[b] * LOG2E
    row_head = lax.broadcasted_iota(jnp.int32, (nrow, dh), 0) % heads
    q_wide = jnp.concatenate([jnp.where(row_head == h, q, jnp.zeros_like(q)) for h in range(heads)], axis=1)

    def page_wide(ref):
        return jnp.concatenate([ref[pl.ds(h, PAGE_SIZE, stride=heads), :] for h in range(heads)],
                               axis=1).astype(BF16)

    def per_row_head(row):
        by_head = jnp.concatenate([row[:, h * PAGE_SIZE:(h + 1) * PAGE_SIZE] for h in range(heads)], axis=0)
        return jnp.concatenate([by_head] * t_new, axis=0)

    def own_head(wide):
        out = None
        for h in range(heads):
            part = jnp.where(row_head == h, wide[:, h * dh:(h + 1) * dh], 0.0)
            out = part if out is None else out + part
        return out

    def online(logit_list, weighted_values):
        m_cur = logit_list[0]
        for lg in logit_list[1:]:
            m_cur = jnp.maximum(m_cur, lg)
        m_prev = m_ref[...]
        m_new = jnp.maximum(m_prev, jnp.max(m_cur, axis=-1, keepdims=True))
        alpha = jnp.exp2(m_prev - m_new)
        probs = [jnp.exp2(lg - m_new) for lg in logit_list]
        row_sum = probs[0]
        for pj in probs[1:]:
            row_sum = row_sum + pj
        l_ref[...] = alpha * l_ref[...] + jnp.sum(row_sum, axis=-1, keepdims=True)
        acc_ref[...] = alpha * acc_ref[...] + weighted_values([pj.astype(BF16) for pj in probs])
        m_ref[...] = m_new

    logit_list = []
    carry = carry_ref[...]
    for i in range(pp):
        suffix = (carry + per_row_head(sfx_refs[i][:, 0:pw])) * LOG2E
        carry = carry + per_row_head(sfx_refs[i][:, pw:2 * pw])
        s = lax.dot_general(q_wide, page_wide(k_refs[i]), _NT, preferred_element_type=F32)
        logit_list.append(s + cn_col + suffix)
    carry_ref[...] = carry

    def page_values(probs):
        wide = None
        for pj, v_ref in zip(probs, v_refs):
            t = jnp.dot(pj, page_wide(v_ref), preferred_element_type=F32)
            wide = t if wide is None else wide + t
        return own_head(wide)

    online(logit_list, page_values)

    def finish():
        @pl.when(jnp.logical_and(valid, j == nj - 1))
        def _():
            s = lax.dot_general(q, kn_ref[b].astype(BF16), _NT, preferred_element_type=F32)
            logits = s + cn_col - cn_row_ref[b] * LOG2E
            r = lax.broadcasted_iota(jnp.int32, (nrow, nrow), 0)
            c = lax.broadcasted_iota(jnp.int32, (nrow, nrow), 1)
            keep = jnp.logical_and((r % heads) == (c % heads), (r // heads) >= (c // heads))
            online([jnp.where(keep, logits, -jnp.inf)],
                   lambda probs: jnp.dot(probs[0], vn_ref[b].astype(BF16), preferred_element_type=F32))
            o_ref[b] = acc_ref[...] / l_ref[...] * _silu(gate_ref[b])

    return finish


def _out_proj_kernel(*refs, n_in, final):
    ins = refs[:n_in]
    w_ref, x_ref = refs[n_in], refs[n_in + 1]
    y = x_ref[...]
    k0 = 0
    for a_ref in ins:
        kw = a_ref.shape[1]
        y = y + jnp.dot(a_ref[...].astype(BF16), w_ref[k0:k0 + kw, :], preferred_element_type=F32)
        k0 += kw
    if final:
        g_ref, o_ref = refs[n_in + 2], refs[n_in + 3]
        o_ref[...] = _rmsnorm_rows(y, g_ref[...])
    else:
        refs[n_in + 2][...] = y


def _out_proj(ins, w, x, final_gain, *, tm):
    m = x.shape[0]
    const = lambda i: (0, 0)
    row = lambda i: (i, 0)
    in_specs = [pl.BlockSpec((tm, a.shape[1]), row) for a in ins]
    in_specs += [pl.BlockSpec(w.shape, const), pl.BlockSpec((tm, D_MODEL), row)]
    operands = list(ins) + [w, x]
    if final_gain is not None:
        in_specs.append(pl.BlockSpec((1, D_MODEL), const))
        operands.append(final_gain)
    return pl.pallas_call(
        functools.partial(_out_proj_kernel, n_in=len(ins), final=final_gain is not None),
        grid=(m // tm,),
        in_specs=in_specs,
        out_specs=pl.BlockSpec((tm, D_MODEL), row),
        out_shape=jax.ShapeDtypeStruct((m, D_MODEL), F32),
        compiler_params=_cparams(("arbitrary",)),
        name="out_proj_final" if final_gain is not None else "out_proj",
    )(*operands)


def _pad_cols(w, n):
    return jnp.pad(w, ((0, 0), (0, n - w.shape[1])))


def _prep_weights(weights):
    (norm_even, w_in_even, b_fox_f, lb_logits, hgrn_gain, w_out_even, norm_odd, w_in_odd, w_gla_gate, b_gla_gate,
     gla_gain, w_out_odd, final_norm) = weights
    n_even = 4 * A_WIDTH + 4 * B_WIDTH
    n_odd = 2 * C_KEY_WIDTH + 2 * C_VAL_WIDTH
    w_even_t = w_in_even[0].T.astype(BF16)
    w_odd_t = w_in_odd[0].T.astype(BF16)
    pad_rows = lambda a: jnp.pad(a, ((0, LANES - a.shape[0]), (0, 0)))
    return dict(
        norm_even=norm_even[0].reshape(1, D_MODEL),
        w_even=w_even_t[:n_even],
        w_fb=pad_rows(w_even_t[n_even:]),
        b_fox=_pad_cols(b_fox_f[0].reshape(1, B_HEADS), LANES),
        lb_logits=lb_logits,
        hgrn_gain=hgrn_gain[0].reshape(1, A_WIDTH),
        w_out_even=w_out_even[0].astype(BF16),
        norm_odd=norm_odd[0].reshape(1, D_MODEL),
        w_odd=w_odd_t[:n_odd],
        w_r=pad_rows(w_odd_t[n_odd:]),
        w_gate=jnp.pad(w_gla_gate[0], ((0, LANES - C_GATE_RANK), (0, 0))).astype(BF16),
        b_gate=b_gla_gate[0].reshape(1, C_KEY_WIDTH),
        gla_gain=gla_gain[0].reshape(1, C_VAL_WIDTH),
        w_out_odd=w_out_odd[0].astype(BF16),
        final_norm=final_norm.reshape(1, D_MODEL),
    )


def _forward(x_prompt, x_sample, ctx, w):
    bsz, seq, _ = x_prompt.shape
    db, t_new, _ = x_sample.shape
    mp, ms = bsz * seq, db * t_new
    xp = x_prompt.reshape(mp, D_MODEL)
    xs = x_sample.reshape(ms, D_MODEL)
    hg = dict(heads=A_HEADS, dk=A_HEAD_DIM, dv=A_HEAD_DIM, q_scale=1.0)
    gl = dict(heads=C_HEADS, dk=C_KEY_DIM, dv=C_VAL_DIM, q_scale=C_KEY_DIM ** -0.5)

    even = lambda x2, s, tm: _even_proj(x2, w["norm_even"], w["w_even"], w["w_fb"], w["b_fox"], seq=s, tm=tm)
    pa_p, gate_p, qa_p, ka_p, vb_p, krows_p, vrows_p, lf_p, _ = even(xp, seq, 512)
    pa_s, gate_s, qa_s, _, _, krows_s, vrows_s, lf_s, c_s = even(xs, t_new, ms)

    oa_p, sa_p = _hgrn_prompt(pa_p, w["lb_logits"], w["hgrn_gain"], bsz=bsz, seq=seq, tb=512, **hg)
    oa_s, sa_s = _gla_sample(True, [pa_s] * 4, [0, 1, 2, 3], w["lb_logits"], w["hgrn_gain"], ctx["state_hgrn"],
                             t_new=t_new, **hg)
    nrow = t_new * B_HEADS
    rows = lambda a: a.reshape(db, nrow, B_HEAD_DIM)
    cn = c_s[:, :B_HEADS].reshape(db, nrow)
    q_rows = qa_s.reshape(db, nrow, 2 * B_HEAD_DIM)[:, :, :B_HEAD_DIM]
    ob_p, ob_s = _fox_attention(qa_p, ka_p, vb_p, gate_p, ctx["page_table"], ctx["k_pages"], ctx["v_pages"],
                                ctx["lf_pages"], q_rows, rows(krows_s), rows(vrows_s), rows(gate_s),
                                cn.reshape(db, nrow, 1), cn.reshape(db, 1, nrow), bsz=bsz, seq=seq,
                                tq=min(seq, 1024), pages_per_unit=16)

    y_p, sc_p = _layer1_prompt(xp, oa_p, ob_p, w, bsz=bsz, seq=seq, tb=512, **gl)
    x1_s = _out_proj([oa_s, ob_s.reshape(ms, B_WIDTH)], w["w_out_even"], xs, None, tm=ms)
    proj1, lf1 = _odd_proj(x1_s, w["norm_odd"], w["w_odd"], w["w_r"], w["w_gate"], w["b_gate"], tm=ms)
    oc_s, sc_s = _gla_sample(False, [proj1, proj1, proj1, proj1, lf1], [0, 1, 1, 2, 0], None, w["gla_gain"],
                             ctx["state_gla"], t_new=t_new, **gl)
    y_s = _out_proj([oc_s], w["w_out_odd"], x1_s, w["final_norm"], tm=ms)

    heads4 = lambda a, n, s: a.reshape(n, s, B_HEADS, B_HEAD_DIM)
    logf = lambda a, n, s: a[:, :B_HEADS].reshape(n, s, B_HEADS)
    prompt = (y_p.reshape(bsz, seq, D_MODEL), heads4(krows_p, bsz, seq), heads4(vrows_p, bsz, seq),
              logf(lf_p, bsz, seq), sa_p, sc_p)
    sample = (y_s.reshape(db, t_new, D_MODEL), heads4(krows_s, db, t_new), heads4(vrows_s, db, t_new),
              logf(lf_s, db, t_new), sa_s, sc_s)
    return prompt, sample


def kernel(x_prompt, x_sample, cache_fox_k, cache_fox_v, cache_fox_logf, state_hgrn, state_gla, page_table,
           norm_even, w_in_even, b_fox_f, lb_logits, hgrn_gain, w_out_even, norm_odd, w_in_odd, w_gla_gate,
           b_gla_gate, gla_gain, w_out_odd, final_norm):
    weights = _prep_weights((norm_even, w_in_even, b_fox_f, lb_logits, hgrn_gain, w_out_even, norm_odd, w_in_odd,
                             w_gla_gate, b_gla_gate, gla_gain, w_out_odd, final_norm))
    bsz, seq, _ = x_prompt.shape
    n_pool = cache_fox_k.shape[1]
    pw = PAGE_SIZE * B_HEADS
    ctx = dict(
        page_table=page_table,
        k_pages=cache_fox_k[0].reshape(n_pool, pw, B_HEAD_DIM),
        v_pages=cache_fox_v[0].reshape(n_pool, pw, B_HEAD_DIM),
        lf_pages=cache_fox_logf[0].transpose(0, 2, 1).reshape(n_pool, pw),
        state_hgrn=state_hgrn[0], state_gla=state_gla[0])
    (y_p, kp, vp, lfp, hgrn_p, gla_p), (y_s, ks, vs, lfs, hgrn_s, gla_s) = _forward(x_prompt, x_sample, ctx, weights)
    n_pp = seq // PAGE_SIZE
    fox_k_prompt = kp.reshape(1, bsz, n_pp, PAGE_SIZE, B_HEADS, B_HEAD_DIM)
    fox_v_prompt = vp.reshape(1, bsz, n_pp, PAGE_SIZE, B_HEADS, B_HEAD_DIM)
    fox_logf_prompt = lfp.reshape(1, bsz, n_pp, PAGE_SIZE, B_HEADS)
    return (y_p, y_s, fox_k_prompt, fox_v_prompt, fox_logf_prompt, hgrn_p[None], gla_p[None],
            ks[None], vs[None], lfs[None], hgrn_s[None], gla_s[None])
```

```python
import functools

import numpy as np
import jax
import jax.numpy as jnp
from jax import lax
from jax.experimental import pallas as pl
from jax.experimental.pallas import tpu as pltpu

F32 = jnp.float32
BF16 = jnp.bfloat16

D_MODEL = 1024
PAGE_SIZE = 128
A_HEADS = 4
A_HEAD_DIM = 128
A_WIDTH = 512
B_HEADS = 4
B_HEAD_DIM = 128
B_WIDTH = 512
C_HEADS = 4
C_KEY_WIDTH = 512
C_VAL_WIDTH = 1024
C_KEY_DIM = 128
C_VAL_DIM = 256
C_GATE_RANK = 16
GLA_GATE_NORMALIZER = 16.0
EPS = 1e-6
HGRN_LAYER = 0
LOG2E = 1.4426950408889634
AUG_PIECES = 3
N_EVEN_MAIN = 4 * A_WIDTH + 4 * B_WIDTH
N_ODD_MAIN = 2 * C_KEY_WIDTH + 2 * C_VAL_WIDTH

LANES = 128
SUBLANES = 8
VMEM_LIMIT = 56 * 1024 * 1024
CHUNK = 64
MAX_CHUNK_LOG_DECAY = 60.0

_NT = (((1,), (1,)), ((), ()))
_TN = (((0,), (0,)), ((), ()))


def _cparams(sem):
    return pltpu.CompilerParams(dimension_semantics=sem, vmem_limit_bytes=VMEM_LIMIT)


def _sigmoid(x):
    return 1.0 / (1.0 + jnp.exp(-x))


def _log_sigmoid(x):
    return jnp.minimum(x, 0.0) - jnp.log1p(jnp.exp(-jnp.abs(x)))


def _silu(x):
    return x * _sigmoid(x)


def _rmsnorm_rows(x, g):
    return x * lax.rsqrt(jnp.mean(x * x, axis=-1, keepdims=True) + EPS) * g


def _split3(x):
    p1 = x.astype(BF16)
    r1 = x - p1.astype(F32)
    p2 = r1.astype(BF16)
    p3 = (r1 - p2.astype(F32)).astype(BF16)
    return p1, p2, p3


def _dot01(m01, x):
    acc = None
    for p in _split3(x):
        t = jnp.dot(m01, p, preferred_element_type=F32)
        acc = t if acc is None else acc + t
    return acc


def _x_dot01(x, m01):
    acc = None
    for p in _split3(x):
        t = jnp.dot(p, m01, preferred_element_type=F32)
        acc = t if acc is None else acc + t
    return acc


def _lower_tri(n, seq):
    r = lax.broadcasted_iota(jnp.int32, (n, n), 0)
    c = lax.broadcasted_iota(jnp.int32, (n, n), 1)
    keep = r >= c
    if seq < n:
        keep = jnp.logical_and(keep, (r // seq) == (c // seq))
    return jnp.where(keep, 1.0, 0.0).astype(BF16)


def _dot_wt(h, wt_ref, c0, n):
    return lax.dot_general(h, wt_ref[c0:c0 + n, :], _NT, preferred_element_type=F32)


def _proj_cols(h, wt_ref, out_ref):
    step = 512
    for c0 in range(0, wt_ref.shape[0], step):
        out_ref[:, c0:c0 + step] = _dot_wt(h, wt_ref, c0, step)


def _aug_selectors():
    sel_q = np.zeros((LANES, B_WIDTH), np.float32)
    sel_k = np.zeros((LANES, B_WIDTH), np.float32)
    for p in range(AUG_PIECES):
        for hd in range(B_HEADS):
            sel_q[p * B_HEADS + hd, hd * B_HEAD_DIM + p] = 1.0
            sel_k[p * B_HEADS + hd, hd * B_HEAD_DIM + AUG_PIECES + p] = -1.0
    return jnp.asarray(sel_q, BF16), jnp.asarray(sel_k, BF16)


def _even_proj_kernel(x_ref, g_ref, w_ref, wfb_ref, bfox_ref, selq_ref, selk_ref, pa_ref, gate_ref, qa_ref, ka_ref,
                      vb_ref, kout_ref, vout_ref, lf_ref, c_ref, carry_ref, *, tm, seq):
    i = pl.program_id(0)
    h = _rmsnorm_rows(x_ref[...], g_ref[...]).astype(BF16)
    na = 4 * A_WIDTH
    bw = B_WIDTH
    dh = B_HEAD_DIM
    for c0 in range(0, na, 512):
        pa_ref[:, c0:c0 + 512] = _dot_wt(h, w_ref, c0, 512)
    q = (_dot_wt(h, w_ref, na, bw) * (dh ** -0.5 * LOG2E)).astype(BF16)
    for hd in range(B_HEADS):
        qa_ref[:, 2 * hd * dh:(2 * hd + 1) * dh] = q[:, hd * dh:(hd + 1) * dh]
    for j, out_ref in ((1, kout_ref), (2, vout_ref)):
        kv = _dot_wt(h, w_ref, na + j * bw, bw)
        kv16 = kv.astype(BF16)
        if j == 1:
            for hd in range(B_HEADS):
                ka_ref[:, 2 * hd * dh:(2 * hd + 1) * dh] = kv16[:, hd * dh:(hd + 1) * dh]
        else:
            vb_ref[...] = kv16
        for hd in range(B_HEADS):
            out_ref[pl.ds(hd, tm, stride=B_HEADS), :] = kv[:, hd * dh:(hd + 1) * dh]
    gate_ref[...] = _dot_wt(h, w_ref, na + 3 * bw, bw)
    fb = _dot_wt(h, wfb_ref, 0, LANES) + bfox_ref[...]
    lane = lax.broadcasted_iota(jnp.int32, fb.shape, 1)
    lf = jnp.where(lane < B_HEADS, _log_sigmoid(fb), 0.0)
    lf_ref[...] = lf
    cs = _dot01(_lower_tri(tm, seq), lf)
    if seq > tm:
        @pl.when((i * tm) % seq == 0)
        def _():
            carry_ref[...] = jnp.zeros_like(carry_ref)
        cs = cs + carry_ref[0:1, :]
        carry_ref[0:1, :] = cs[tm - 1:tm, :]
    c_ref[...] = cs
    pos = lax.broadcasted_iota(jnp.int32, (1, bw), 1) % dh
    aug_q = jnp.where(jnp.logical_and(pos >= AUG_PIECES, pos < 2 * AUG_PIECES), 1.0, 0.0)
    aug_k = jnp.where(pos < AUG_PIECES, 1.0, 0.0)
    packed = None
    for p, piece in enumerate(_split3(cs * LOG2E)):
        shifted = piece.astype(F32) if p == 0 else pltpu.roll(piece.astype(F32), p * B_HEADS, 1)
        packed = shifted if packed is None else packed + shifted
    packed = packed.astype(BF16)
    aug_q = (aug_q + jnp.dot(packed, selq_ref[...], preferred_element_type=F32)).astype(BF16)
    aug_k = (aug_k + jnp.dot(packed, selk_ref[...], preferred_element_type=F32)).astype(BF16)
    for hd in range(B_HEADS):
        qa_ref[:, (2 * hd + 1) * dh:(2 * hd + 2) * dh] = aug_q[:, hd * dh:(hd + 1) * dh]
        ka_ref[:, (2 * hd + 1) * dh:(2 * hd + 2) * dh] = aug_k[:, hd * dh:(hd + 1) * dh]


def _even_proj(x, g, w_main, w_fb, b_fox, *, seq, tm):
    m = x.shape[0]
    const = lambda i: (0, 0)
    row = lambda i: (i, 0)
    sel_q, sel_k = _aug_selectors()
    return pl.pallas_call(
        functools.partial(_even_proj_kernel, tm=tm, seq=seq),
        grid=(m // tm,),
        in_specs=[pl.BlockSpec((tm, D_MODEL), row), pl.BlockSpec((1, D_MODEL), const),
                  pl.BlockSpec((N_EVEN_MAIN, D_MODEL), const), pl.BlockSpec(w_fb.shape, const),
                  pl.BlockSpec((1, LANES), const), pl.BlockSpec(sel_q.shape, const),
                  pl.BlockSpec(sel_k.shape, const)],
        out_specs=[pl.BlockSpec((tm, 4 * A_WIDTH), row), pl.BlockSpec((tm, B_WIDTH), row),
                   pl.BlockSpec((tm, 2 * B_WIDTH), row), pl.BlockSpec((tm, 2 * B_WIDTH), row),
                   pl.BlockSpec((tm, B_WIDTH), row), pl.BlockSpec((tm * B_HEADS, B_HEAD_DIM), row),
                   pl.BlockSpec((tm * B_HEADS, B_HEAD_DIM), row), pl.BlockSpec((tm, LANES), row),
                   pl.BlockSpec((tm, LANES), row)],
        out_shape=[jax.ShapeDtypeStruct((m, 4 * A_WIDTH), F32), jax.ShapeDtypeStruct((m, B_WIDTH), F32),
                   jax.ShapeDtypeStruct((m, 2 * B_WIDTH), BF16), jax.ShapeDtypeStruct((m, 2 * B_WIDTH), BF16),
                   jax.ShapeDtypeStruct((m, B_WIDTH), BF16),
                   jax.ShapeDtypeStruct((m * B_HEADS, B_HEAD_DIM), F32),
                   jax.ShapeDtypeStruct((m * B_HEADS, B_HEAD_DIM), F32),
                   jax.ShapeDtypeStruct((m, LANES), F32), jax.ShapeDtypeStruct((m, LANES), F32)],
        scratch_shapes=[pltpu.VMEM((SUBLANES, LANES), F32)],
        compiler_params=_cparams(("arbitrary",)),
        name="even_proj",
    )(x, g, w_main, w_fb, b_fox, sel_q, sel_k)


def _odd_proj_kernel(x_ref, g_ref, w_ref, wr_ref, wg_ref, bg_ref, proj_ref, lf_ref):
    h = _rmsnorm_rows(x_ref[...], g_ref[...]).astype(BF16)
    _proj_cols(h, w_ref, proj_ref)
    r = _dot_wt(h, wr_ref, 0, LANES)
    z = jnp.dot(r.astype(BF16), wg_ref[...], preferred_element_type=F32) + bg_ref[...]
    lf_ref[...] = _log_sigmoid(z) / GLA_GATE_NORMALIZER


def _odd_proj(x, g, w_main, w_r, w_gate, b_gate, *, tm):
    m = x.shape[0]
    n = N_ODD_MAIN
    const = lambda i: (0, 0)
    row = lambda i: (i, 0)
    return pl.pallas_call(
        _odd_proj_kernel,
        grid=(m // tm,),
        in_specs=[pl.BlockSpec((tm, D_MODEL), row), pl.BlockSpec((1, D_MODEL), const),
                  pl.BlockSpec((n, D_MODEL), const), pl.BlockSpec(w_r.shape, const),
                  pl.BlockSpec((LANES, C_KEY_WIDTH), const), pl.BlockSpec((1, C_KEY_WIDTH), const)],
        out_specs=[pl.BlockSpec((tm, n), row), pl.BlockSpec((tm, C_KEY_WIDTH), row)],
        out_shape=[jax.ShapeDtypeStruct((m, n), F32), jax.ShapeDtypeStruct((m, C_KEY_WIDTH), F32)],
        compiler_params=_cparams(("arbitrary",)),
        name="odd_proj",
    )(x, g, w_main, w_r, w_gate, b_gate)


def _lower_bound(logits, layer):
    e = jnp.exp(logits - jnp.max(logits, axis=0, keepdims=True))
    return jnp.sum(e[:layer + 1, :], axis=0, keepdims=True) / jnp.sum(e, axis=0, keepdims=True)


def _hgrn_gate(fa, lb):
    f = lb + (1.0 - lb) * _sigmoid(fa)
    return jnp.log(f), 1.0 - f


def _exact_group(q, k, v, g, st, lo, hi):
    n = SUBLANES
    row = lax.broadcasted_iota(jnp.int32, (n, 1), 0)
    valid = jnp.logical_and(row >= lo, row < hi)
    q = jnp.where(valid, q, 0.0)
    k = jnp.where(valid, k, 0.0)
    g = jnp.where(valid, g, 0.0)
    b = g
    for sh in (1, 2, 4):
        b = b + jnp.where(row >= sh, pltpu.roll(b, sh, 0), 0.0)
    o = lax.dot_general((q * jnp.exp(b)).astype(BF16), st.astype(BF16), _NT, preferred_element_type=F32)
    for s in range(lo, hi):
        w = jnp.exp(jnp.minimum(b - b[s:s + 1, :], 0.0))
        a = jnp.sum(q * k[s:s + 1, :] * w, axis=-1, keepdims=True)
        o = o + jnp.where(row >= s, a, 0.0) * v[s:s + 1, :]
    b_last = b[n - 1:n, :]
    k_hat = k * jnp.exp(b_last - b)
    st_new = st * jnp.exp(b_last) + lax.dot_general(v.astype(BF16), k_hat.astype(BF16), _TN,
                                                    preferred_element_type=F32)
    return o, st_new


def _head_norm_gate(o, gain, gate):
    y = o * lax.rsqrt(jnp.mean(o * o, axis=-1, keepdims=True) + EPS) * gain
    return y * _silu(gate)


def _recurrence_block(q_ref, k_ref, v_ref, gate_ref, g_ref, gain_ref, o_ref, st_ref, b_ref, *, fa_ref, lb, heads, dk,
                      dv, tb, q_scale):
    n_chunks = tb // CHUNK

    tri = _lower_tri(CHUNK, CHUNK)
    b_min = None
    for c in range(n_chunks):
        rows = slice(c * CHUNK, (c + 1) * CHUNK)
        if fa_ref is not None:
            g, k = _hgrn_gate(fa_ref[rows, :], lb)
            g_ref[rows, :] = g
            k_ref[rows, :] = k
        else:
            g = g_ref[rows, :]
        b = _dot01(tri, g)
        b_ref[rows, :] = b
        b_last = b[CHUNK - 1:CHUNK, :]
        b_min = b_last if b_min is None else jnp.minimum(b_min, b_last)
    chunk_form_ok = jnp.min(b_min) >= -MAX_CHUNK_LOG_DECAY

    def finish(rows, h, o_h):
        vsl = slice(h * dv, (h + 1) * dv)
        o_ref[rows, vsl] = _head_norm_gate(o_h, gain_ref[:, vsl], gate_ref[rows, vsl]).astype(o_ref.dtype)

    @pl.when(chunk_form_ok)
    def _():
        r_i = lax.broadcasted_iota(jnp.int32, (CHUNK, CHUNK), 0)
        c_i = lax.broadcasted_iota(jnp.int32, (CHUNK, CHUNK), 1)
        causal = r_i >= c_i

        for c in range(n_chunks):
            rows = slice(c * CHUNK, (c + 1) * CHUNK)
            b = b_ref[rows, :]
            e_b = jnp.exp(b)
            b_last = b[CHUNK - 1:CHUNK, :]
            e_last = jnp.exp(b_last)
            q_t = q_ref[rows, :] * q_scale * e_b
            k_t = k_ref[rows, :] * jnp.exp(-b)
            k_hat = k_t * e_last
            for h in range(heads):
                ksl = slice(h * dk, (h + 1) * dk)
                vsl = slice(h * dv, (h + 1) * dv)
                qh = q_t[:, ksl].astype(BF16)
                vh = v_ref[rows, vsl].astype(BF16)
                a = lax.dot_general(qh, k_t[:, ksl].astype(BF16), _NT, preferred_element_type=F32)
                a = jnp.where(causal, a, 0.0).astype(BF16)
                st = st_ref[h]
                o_h = jnp.dot(a, vh, preferred_element_type=F32) + lax.dot_general(
                    qh, st.astype(BF16), _NT, preferred_element_type=F32)
                st_ref[h] = st * e_last[:, ksl] + lax.dot_general(
                    vh, k_hat[:, ksl].astype(BF16), _TN, preferred_element_type=F32)
                finish(rows, h, o_h)

    @pl.when(jnp.logical_not(chunk_form_ok))
    def _():
        pair = 2 * SUBLANES

        def group(i, carry):
            for h in range(heads):
                ksl = slice(h * dk, (h + 1) * dk)
                vsl = slice(h * dv, (h + 1) * dv)
                outs = []
                for half in range(2):
                    rows = pl.ds(pl.multiple_of(i * pair + half * SUBLANES, SUBLANES), SUBLANES)
                    o_h, st_new = _exact_group(q_ref[rows, ksl] * q_scale, k_ref[rows, ksl], v_ref[rows, vsl],
                                               g_ref[rows, ksl], st_ref[h], 0, SUBLANES)
                    st_ref[h] = st_new
                    outs.append(o_h)
                finish(pl.ds(pl.multiple_of(i * pair, pair), pair), h, jnp.concatenate(outs, axis=0))
            return carry

        lax.fori_loop(0, tb // pair, group, 0)


def _state_step_edges(st_ref, s_ref):
    t = pl.program_id(1)

    def first():
        @pl.when(t == 0)
        def _():
            st_ref[...] = jnp.zeros_like(st_ref)

    def last():
        @pl.when(t == pl.num_programs(1) - 1)
        def _():
            for h in range(st_ref.shape[0]):
                s_ref[0, h] = st_ref[h].T

    return first, last


def _hgrn_prompt_kernel(q_ref, fa_ref, v_ref, gate_ref, lb_ref, gain_ref, o_ref, s_ref, st_ref, b_ref, g_ref, k_ref,
                        **kw):
    first, last = _state_step_edges(st_ref, s_ref)
    first()
    _recurrence_block(q_ref, k_ref, v_ref, gate_ref, g_ref, gain_ref, o_ref, st_ref, b_ref, fa_ref=fa_ref,
                      lb=_lower_bound(lb_ref[...], HGRN_LAYER), **kw)
    last()


def _hgrn_prompt(pa, lb_logits, gain, *, bsz, seq, heads, dk, dv, tb, q_scale):
    nt = seq // tb
    kw, vw = heads * dk, heads * dv
    col = lambda cb: (lambda b, t: (b * nt + t, cb))
    const = lambda b, t: (0, 0)
    return pl.pallas_call(
        functools.partial(_hgrn_prompt_kernel, heads=heads, dk=dk, dv=dv, tb=tb, q_scale=q_scale),
        grid=(bsz, nt),
        in_specs=[pl.BlockSpec((tb, kw), col(0)), pl.BlockSpec((tb, kw), col(1)), pl.BlockSpec((tb, vw), col(2)),
                  pl.BlockSpec((tb, vw), col(3)), pl.BlockSpec(lb_logits.shape, const),
                  pl.BlockSpec((1, vw), const)],
        out_specs=[pl.BlockSpec((tb, vw), col(0)), pl.BlockSpec((1, heads, dk, dv), lambda b, t: (b, 0, 0, 0))],
        out_shape=[jax.ShapeDtypeStruct((bsz * seq, vw), BF16), jax.ShapeDtypeStruct((bsz, heads, dk, dv), F32)],
        scratch_shapes=[pltpu.VMEM((heads, dv, dk), F32), pltpu.VMEM((tb, kw), F32), pltpu.VMEM((tb, kw), F32),
                        pltpu.VMEM((tb, kw), F32)],
        compiler_params=_cparams(("arbitrary", "arbitrary")),
        name="hgrn_prompt",
    )(pa, pa, pa, pa, lb_logits, gain)


def _layer1_prompt_kernel(x_ref, oa_ref, ob_ref, woe_ref, gn_ref, w_ref, wr_ref, wg_ref, bg_ref, gain_ref, woo_ref,
                          fn_ref, y_ref, s_ref, st_ref, b_ref, proj_ref, lf_ref, o_ref, *, heads, dk, dv, tb,
                          q_scale):
    first, last = _state_step_edges(st_ref, s_ref)
    first()
    kw, vw = heads * dk, heads * dv
    aw = oa_ref.shape[1]
    x1 = (x_ref[...] + jnp.dot(oa_ref[...], woe_ref[0:aw, :], preferred_element_type=F32)
          + jnp.dot(ob_ref[...], woe_ref[aw:, :], preferred_element_type=F32))
    h = _rmsnorm_rows(x1, gn_ref[...]).astype(BF16)
    _proj_cols(h, w_ref, proj_ref)
    r = _dot_wt(h, wr_ref, 0, LANES)
    z = jnp.dot(r.astype(BF16), wg_ref[...], preferred_element_type=F32) + bg_ref[...]
    lf_ref[...] = _log_sigmoid(z) / GLA_GATE_NORMALIZER
    _recurrence_block(proj_ref.at[:, 0:kw], proj_ref.at[:, kw:2 * kw], proj_ref.at[:, 2 * kw:2 * kw + vw],
                      proj_ref.at[:, 2 * kw + vw:2 * kw + 2 * vw], lf_ref, gain_ref, o_ref, st_ref, b_ref,
                      fa_ref=None, lb=None, heads=heads, dk=dk, dv=dv, tb=tb, q_scale=q_scale)
    y = x1 + jnp.dot(o_ref[...], woo_ref[...], preferred_element_type=F32)
    y_ref[...] = _rmsnorm_rows(y, fn_ref[...])
    last()


def _layer1_prompt(x, o_a, o_b, w, *, bsz, seq, heads, dk, dv, tb, q_scale):
    nt = seq // tb
    kw, vw = heads * dk, heads * dv
    row = lambda b, t: (b * nt + t, 0)
    const = lambda b, t: (0, 0)
    full = lambda a: pl.BlockSpec((N_ODD_MAIN, D_MODEL) if a is w["w_odd"] else a.shape, const)
    weights = [w["w_out_even"], w["norm_odd"], w["w_odd"], w["w_r"], w["w_gate"], w["b_gate"], w["gla_gain"],
               w["w_out_odd"], w["final_norm"]]
    return pl.pallas_call(
        functools.partial(_layer1_prompt_kernel, heads=heads, dk=dk, dv=dv, tb=tb, q_scale=q_scale),
        grid=(bsz, nt),
        in_specs=[pl.BlockSpec((tb, D_MODEL), row), pl.BlockSpec((tb, o_a.shape[1]), row),
                  pl.BlockSpec((tb, o_b.shape[1]), row)] + [full(a) for a in weights],
        out_specs=[pl.BlockSpec((tb, D_MODEL), row), pl.BlockSpec((1, heads, dk, dv), lambda b, t: (b, 0, 0, 0))],
        out_shape=[jax.ShapeDtypeStruct((bsz * seq, D_MODEL), F32),
                   jax.ShapeDtypeStruct((bsz, heads, dk, dv), F32)],
        scratch_shapes=[pltpu.VMEM((heads, dv, dk), F32), pltpu.VMEM((tb, kw), F32),
                        pltpu.VMEM((tb, 2 * kw + 2 * vw), F32), pltpu.VMEM((tb, kw), F32),
                        pltpu.VMEM((tb, vw), BF16)],
        compiler_params=_cparams(("arbitrary", "arbitrary")),
        name="layer1_prompt",
    )(x, o_a, o_b, *weights)


SAMPLE_GROUPS = 1


def _gla_sample_kernel(*refs, hgrn, heads, dk, dv, t_new, q_scale, groups):
    if hgrn:
        q_ref, fa_ref, v_ref, gate_ref, lb_ref, gain_ref, s0_ref, o_ref, s_ref = refs
    else:
        q_ref, k_ref, v_ref, gate_ref, g_ref, gain_ref, s0_ref, o_ref, s_ref = refs
    per = SUBLANES // t_new
    for grp in range(groups):
        rows = slice(grp * SUBLANES, (grp + 1) * SUBLANES)
        for h in range(heads):
            ksl = slice(h * dk, (h + 1) * dk)
            vsl = slice(h * dv, (h + 1) * dv)
            if hgrn:
                g, k = _hgrn_gate(fa_ref[rows, ksl], _lower_bound(lb_ref[...], HGRN_LAYER)[:, ksl])
            else:
                g, k = g_ref[rows, ksl], k_ref[rows, ksl]
            q = q_ref[rows, ksl] * q_scale
            v = v_ref[rows, vsl]
            o_h = None
            for e in range(per):
                o_e, st_new = _exact_group(q, k, v, g, s0_ref[grp * per + e, h].T, e * t_new, (e + 1) * t_new)
                s_ref[grp * per + e, h] = st_new.T
                o_h = o_e if o_h is None else o_h + o_e
            o_ref[rows, vsl] = _head_norm_gate(o_h, gain_ref[:, vsl], gate_ref[rows, vsl])


def _gla_sample(hgrn, arrays, col_blocks, small, gain, s0, *, t_new, heads, dk, dv, q_scale):
    m = arrays[0].shape[0]
    groups = SAMPLE_GROUPS if m % (SAMPLE_GROUPS * SUBLANES) == 0 else 1
    tr = groups * SUBLANES
    per = tr // t_new
    kw, vw = heads * dk, heads * dv
    widths = [kw, kw, vw, vw] + ([] if hgrn else [kw])
    in_specs = [pl.BlockSpec((tr, w), functools.partial(lambda i, cb: (i, cb), cb=cb))
                for w, cb in zip(widths, col_blocks)]
    operands = list(arrays)
    if hgrn:
        in_specs.append(pl.BlockSpec(small.shape, lambda i: (0, 0)))
        operands.append(small)
    in_specs.append(pl.BlockSpec((1, vw), lambda i: (0, 0)))
    operands.append(gain)
    in_specs.append(pl.BlockSpec((per, heads, dk, dv), lambda i: (i, 0, 0, 0)))
    operands.append(s0)
    return pl.pallas_call(
        functools.partial(_gla_sample_kernel, hgrn=hgrn, heads=heads, dk=dk, dv=dv, t_new=t_new, q_scale=q_scale,
                          groups=groups),
        grid=(m // tr,),
        in_specs=in_specs,
        out_specs=[pl.BlockSpec((tr, vw), lambda i: (i, 0)),
                   pl.BlockSpec((per, heads, dk, dv), lambda i: (i, 0, 0, 0))],
        out_shape=[jax.ShapeDtypeStruct((m, vw), F32), jax.ShapeDtypeStruct(s0.shape, F32)],
        compiler_params=_cparams(("arbitrary",)),
        name="hgrn_sample" if hgrn else "gla_sample",
    )(*operands)


FOX_STRIP = 64


def _fox_prompt_step(qi, ki, before_head, q_ref, k_ref, v_ref, gate_ref, o_ref, m_ref, l_ref, acc_ref, p_ref, a_ref,
                     *, tq, heads, dh):
    aw = 2 * dh

    @pl.when(ki == 0)
    def _():
        m_ref[...] = jnp.full_like(m_ref, -jnp.inf)
        l_ref[...] = jnp.zeros_like(l_ref)
        acc_ref[...] = jnp.zeros_like(acc_ref)

    def step(diag):
        if diag:
            r_i = lax.broadcasted_iota(jnp.int32, (FOX_STRIP, LANES), 0)
            c_i = lax.broadcasted_iota(jnp.int32, (FOX_STRIP, LANES), 1)
        half = tq // 2
        segments = [(0, half, half), (half, tq, tq)] if diag and half % LANES == 0 else [(0, tq, tq)]
        for h in range(heads):
            after_head = before_head(h)
            sl = slice(h * dh, (h + 1) * dh)
            hw = slice(h * aw, (h + 1) * aw)
            for lo, hi, ext in segments:
                s = lax.dot_general(q_ref[lo:hi, hw], k_ref[0:ext, hw], _NT, preferred_element_type=F32)
                for r0 in range(lo, hi, FOX_STRIP):
                    rows = slice(r0, r0 + FOX_STRIP)
                    srows = slice(r0 - lo, r0 - lo + FOX_STRIP)
                    live = [j for j in range(ext // LANES) if not (diag and j * LANES > r0 + FOX_STRIP - 1)]
                    blocks = []
                    for j in live:
                        lg = s[srows, j * LANES:(j + 1) * LANES]
                        if diag and (j + 1) * LANES - 1 > r0:
                            lg = jnp.where(r_i + r0 >= c_i + j * LANES, lg, -jnp.inf)
                        blocks.append(lg)
                    m_cur = blocks[0]
                    for lg in blocks[1:]:
                        m_cur = jnp.maximum(m_cur, lg)
                    m_prev = m_ref[h, rows]
                    m_new = jnp.maximum(m_prev, jnp.max(m_cur, axis=-1, keepdims=True))
                    alpha = jnp.exp2(m_prev - m_new)
                    probs = [jnp.exp2(lg - m_new) for lg in blocks]
                    row_sum = probs[0]
                    for pj in probs[1:]:
                        row_sum = row_sum + pj
                    l_ref[h, rows] = alpha * l_ref[h, rows] + jnp.sum(row_sum, axis=-1, keepdims=True)
                    m_ref[h, rows] = m_new
                    a_ref[rows] = alpha
                    for j, pj in zip(live, probs):
                        p_ref[rows, j * LANES:(j + 1) * LANES] = pj.astype(BF16)
                    for j in range(ext // LANES):
                        if j not in live:
                            p_ref[rows, j * LANES:(j + 1) * LANES] = jnp.zeros((FOX_STRIP, LANES), BF16)
                acc_ref[lo:hi, sl] = a_ref[lo:hi] * acc_ref[lo:hi, sl] + jnp.dot(
                    p_ref[lo:hi, 0:ext], v_ref[0:ext, sl], preferred_element_type=F32)
            after_head()

    @pl.when(ki < qi)
    def _():
        step(False)

    @pl.when(ki == qi)
    def _():
        step(True)
        for h in range(heads):
            sl = slice(h * dh, (h + 1) * dh)
            o_ref[:, sl] = (acc_ref[:, sl] / l_ref[h] * _silu(gate_ref[:, sl])).astype(o_ref.dtype)


N_PROMPT_REFS = 4
N_SAMPLE_REFS = 9


def _fox_kernel(qi_ref, ki_ref, pt_ref, *refs, tq, heads, dh, sample_kw):
    prompt_in = refs[:N_PROMPT_REFS]
    sample_in = refs[N_PROMPT_REFS:N_PROMPT_REFS + N_SAMPLE_REFS]
    o_ref, os_ref = refs[N_PROMPT_REFS + N_SAMPLE_REFS:N_PROMPT_REFS + N_SAMPLE_REFS + 2]
    scratch = refs[N_PROMPT_REFS + N_SAMPLE_REFS + 2:]
    prompt_scratch, sample_scratch = scratch[:5], scratch[5:]
    p = pl.program_id(1)
    step = pl.program_id(0) * pl.num_programs(1) + p

    def before_head(h):
        return _fox_sample_unit(step * heads + h, pt_ref, *sample_in, os_ref, *sample_scratch, heads=heads, dh=dh,
                                **sample_kw)

    _fox_prompt_step(qi_ref[p], ki_ref[p], before_head, *prompt_in, o_ref, *prompt_scratch, tq=tq, heads=heads, dh=dh)


def _fox_attention(qa, ka, vb, gate, page_table, k_pages, v_pages, lf_pages, q_s, k_new, v_new, gate_s, cn_col,
                   cn_row, *, bsz, seq, tq, pages_per_unit):
    nq = seq // tq
    pairs = [(qi, ki) for qi in range(nq) for ki in range(qi + 1)]
    qi_tab = jnp.asarray(np.array([p[0] for p in pairs], np.int32))
    ki_tab = jnp.asarray(np.array([p[1] for p in pairs], np.int32))
    w = B_WIDTH
    dh = B_HEAD_DIM
    n_pool = lf_pages.shape[0]
    sfx = _page_suffix(lf_pages, B_HEADS)
    db, n_pages = page_table.shape
    pp = pages_per_unit
    n_units = db * (n_pages // pp)
    assert bsz * len(pairs) * B_HEADS >= n_units, "not enough prompt steps to carry the sample page units"
    nrow = q_s.shape[1]
    pw = PAGE_SIZE * B_HEADS
    q_map = lambda b, p, qt, kt, pt: (b * nq + qt[p], 0)
    k_map = lambda b, p, qt, kt, pt: (b * nq + kt[p], 0)
    whole = lambda a: pl.BlockSpec(a.shape, lambda b, p, qt, kt, pt: (0,) * a.ndim)
    hbm = pl.BlockSpec(memory_space=pl.ANY)
    n_slots = 2
    grid_spec = pltpu.PrefetchScalarGridSpec(
        num_scalar_prefetch=3,
        grid=(bsz, len(pairs)),
        in_specs=[pl.BlockSpec((tq, 2 * w), q_map), pl.BlockSpec((tq, 2 * w), k_map),
                  pl.BlockSpec((tq, w), k_map), pl.BlockSpec((tq, w), q_map),
                  hbm, hbm, hbm, whole(q_s), whole(k_new), whole(v_new), whole(gate_s), whole(cn_col),
                  whole(cn_row)],
        out_specs=[pl.BlockSpec((tq, w), q_map),
                   pl.BlockSpec((db, nrow, dh), lambda b, p, qt, kt, pt: (0, 0, 0))],
        scratch_shapes=[pltpu.VMEM((B_HEADS, tq, LANES), F32), pltpu.VMEM((B_HEADS, tq, LANES), F32),
                        pltpu.VMEM((tq, w), F32), pltpu.VMEM((tq, tq), BF16), pltpu.VMEM((tq, LANES), F32),
                        pltpu.VMEM((n_slots, pp, pw, dh), F32), pltpu.VMEM((n_slots, pp, pw, dh), F32),
                        pltpu.VMEM((n_slots, pp, 1, 2 * pw), F32), pltpu.SemaphoreType.DMA((3, n_slots)),
                        pltpu.VMEM((nrow, 1), F32), pltpu.VMEM((nrow, 1), F32), pltpu.VMEM((nrow, dh), F32),
                        pltpu.VMEM((nrow, PAGE_SIZE), F32)],
    )
    sample_kw = dict(pp=pp, t_new=nrow // B_HEADS, n_pages=n_pages, n_units=n_units)
    return pl.pallas_call(
        functools.partial(_fox_kernel, tq=tq, heads=B_HEADS, dh=dh, sample_kw=sample_kw),
        grid_spec=grid_spec,
        out_shape=[jax.ShapeDtypeStruct((bsz * seq, w), BF16), jax.ShapeDtypeStruct((db, nrow, dh), F32)],
        compiler_params=_cparams(("arbitrary", "arbitrary")),
        name="fox_attention",
    )(qi_tab, ki_tab, page_table, qa, ka, vb, gate, k_pages, v_pages, sfx, q_s, k_new, v_new, gate_s, cn_col,
      cn_row)


def _page_suffix_kernel(lf_ref, later_ref, total_ref, out_ref):
    lf = lf_ref[...]
    pw = lf.shape[1]
    out_ref[:, 0:pw] = _x_dot01(lf, later_ref[...])
    out_ref[:, pw:2 * pw] = _x_dot01(lf, total_ref[...])


def _page_suffix(lf_pages, heads):
    n_pool, pw = lf_pages.shape
    tm = 512 if n_pool % 512 == 0 else n_pool
    j = np.arange(pw)
    in_head, in_tok = j[:, None] // (pw // heads), j[:, None] % (pw // heads)
    out_head, out_tok = j[None, :] // (pw // heads), j[None, :] % (pw // heads)
    head_eq = in_head == out_head
    later = jnp.asarray(head_eq & (in_tok > out_tok), BF16)
    total = jnp.asarray(head_eq, BF16)
    row = lambda i: (i, 0)
    const = lambda i: (0, 0)
    return pl.pallas_call(
        _page_suffix_kernel,
        grid=(n_pool // tm,),
        in_specs=[pl.BlockSpec((tm, pw), row), pl.BlockSpec((pw, pw), const), pl.BlockSpec((pw, pw), const)],
        out_specs=pl.BlockSpec((tm, 2 * pw), row),
        out_shape=jax.ShapeDtypeStruct((n_pool, 2 * pw), F32),
        compiler_params=_cparams(("arbitrary",)),
        name="page_suffix",
    )(lf_pages, later, total)


def _fox_sample_unit(u, pt_ref, k_hbm, v_hbm, sfx_hbm, q_ref, kn_ref, vn_ref, gate_ref, cn_col_ref, cn_row_ref, o_ref,
                     kbuf, vbuf, sbuf, sem, m_ref, l_ref, acc_ref, carry_ref, *, pp, heads, dh, t_new, n_pages,
                     n_units):
    nj = n_pages // pp
    valid = u < n_units
    uc = jnp.minimum(u, n_units - 1)
    b = lax.div(uc, nj)
    j = lax.rem(uc, nj)
    nrow = t_new * heads
    pw = PAGE_SIZE * heads
    slot = lax.rem(uc, 2)

    def page_copies(uu, sl):
        bb = lax.div(uu, nj)
        jj = lax.rem(uu, nj)
        copies = []
        for i in range(pp):
            page = pt_ref[bb, n_pages - 1 - (jj * pp + i)]
            sources = (k_hbm.at[page], v_hbm.at[page], sfx_hbm.at[pl.ds(page, 1)])
            for kind, (src, dst) in enumerate(zip(sources, (kbuf, vbuf, sbuf))):
                copies.append(pltpu.make_async_copy(src, dst.at[sl, i], sem.at[kind, sl]))
        return copies

    @pl.when(u == 0)
    def _():
        for cp in page_copies(u, slot):
            cp.start()

    @pl.when(u + 1 < n_units)
    def _():
        for cp in page_copies(u + 1, 1 - slot):
            cp.start()

    @pl.when(valid)
    def _():
        for cp in page_copies(u, slot):
            cp.wait()

    k_refs = [kbuf.at[slot, i] for i in range(pp)]
    v_refs = [vbuf.at[slot, i] for i in range(pp)]
    sfx_refs = [sbuf.at[slot, i] for i in range(pp)]

    @pl.when(j == 0)
    def _():
        m_ref[...] = jnp.full_like(m_ref, -jnp.inf)
        l_ref[...] = jnp.zeros_like(l_ref)
        acc_ref[...] = jnp.zeros_like(acc_ref)
        carry_ref[...] = jnp.zeros_like(carry_ref)

    q = q_ref[b]
    cn_col = cn_col_ref[b] * LOG2E
    row_head = lax.broadcasted_iota(jnp.int32, (nrow, dh), 0) % heads
    q_wide = jnp.concatenate([jnp.where(row_head == h, q, jnp.zeros_like(q)) for h in range(heads)], axis=1)

    def page_wide(ref):
        return jnp.concatenate([ref[pl.ds(h, PAGE_SIZE, stride=heads), :] for h in range(heads)],
                               axis=1).astype(BF16)

    def per_row_head(row):
        by_head = jnp.concatenate([row[:, h * PAGE_SIZE:(h + 1) * PAGE_SIZE] for h in range(heads)], axis=0)
        return jnp.concatenate([by_head] * t_new, axis=0)

    def own_head(wide):
        out = None
        for h in range(heads):
            part = jnp.where(row_head == h, wide[:, h * dh:(h + 1) * dh], 0.0)
            out = part if out is None else out + part
        return out

    def online(logit_list, weighted_values):
        m_cur = logit_list[0]
        for lg in logit_list[1:]:
            m_cur = jnp.maximum(m_cur, lg)
        m_prev = m_ref[...]
        m_new = jnp.maximum(m_prev, jnp.max(m_cur, axis=-1, keepdims=True))
        alpha = jnp.exp2(m_prev - m_new)
        probs = [jnp.exp2(lg - m_new) for lg in logit_list]
        row_sum = probs[0]
        for pj in probs[1:]:
            row_sum = row_sum + pj
        l_ref[...] = alpha * l_ref[...] + jnp.sum(row_sum, axis=-1, keepdims=True)
        acc_ref[...] = alpha * acc_ref[...] + weighted_values([pj.astype(BF16) for pj in probs])
        m_ref[...] = m_new

    logit_list = []
    carry = carry_ref[...]
    for i in range(pp):
        suffix = (carry + per_row_head(sfx_refs[i][:, 0:pw])) * LOG2E
        carry = carry + per_row_head(sfx_refs[i][:, pw:2 * pw])
        s = lax.dot_general(q_wide, page_wide(k_refs[i]), _NT, preferred_element_type=F32)
        logit_list.append(s + cn_col + suffix)
    carry_ref[...] = carry

    def page_values(probs):
        wide = None
        for pj, v_ref in zip(probs, v_refs):
            t = jnp.dot(pj, page_wide(v_ref), preferred_element_type=F32)
            wide = t if wide is None else wide + t
        return own_head(wide)

    online(logit_list, page_values)

    def finish():
        @pl.when(jnp.logical_and(valid, j == nj - 1))
        def _():
            s = lax.dot_general(q, kn_ref[b].astype(BF16), _NT, preferred_element_type=F32)
            logits = s + cn_col - cn_row_ref[b] * LOG2E
            r = lax.broadcasted_iota(jnp.int32, (nrow, nrow), 0)
            c = lax.broadcasted_iota(jnp.int32, (nrow, nrow), 1)
            keep = jnp.logical_and((r % heads) == (c % heads), (r // heads) >= (c // heads))
            online([jnp.where(keep, logits, -jnp.inf)],
                   lambda probs: jnp.dot(probs[0], vn_ref[b].astype(BF16), preferred_element_type=F32))
            o_ref[b] = acc_ref[...] / l_ref[...] * _silu(gate_ref[b])

    return finish


def _out_proj_kernel(*refs, n_in, final):
    ins = refs[:n_in]
    w_ref, x_ref = refs[n_in], refs[n_in + 1]
    y = x_ref[...]
    k0 = 0
    for a_ref in ins:
        kw = a_ref.shape[1]
        y = y + jnp.dot(a_ref[...].astype(BF16), w_ref[k0:k0 + kw, :], preferred_element_type=F32)
        k0 += kw
    if final:
        g_ref, o_ref = refs[n_in + 2], refs[n_in + 3]
        o_ref[...] = _rmsnorm_rows(y, g_ref[...])
    else:
        refs[n_in + 2][...] = y


def _out_proj(ins, w, x, final_gain, *, tm):
    m = x.shape[0]
    const = lambda i: (0, 0)
    row = lambda i: (i, 0)
    in_specs = [pl.BlockSpec((tm, a.shape[1]), row) for a in ins]
    in_specs += [pl.BlockSpec(w.shape, const), pl.BlockSpec((tm, D_MODEL), row)]
    operands = list(ins) + [w, x]
    if final_gain is not None:
        in_specs.append(pl.BlockSpec((1, D_MODEL), const))
        operands.append(final_gain)
    return pl.pallas_call(
        functools.partial(_out_proj_kernel, n_in=len(ins), final=final_gain is not None),
        grid=(m // tm,),
        in_specs=in_specs,
        out_specs=pl.BlockSpec((tm, D_MODEL), row),
        out_shape=jax.ShapeDtypeStruct((m, D_MODEL), F32),
        compiler_params=_cparams(("arbitrary",)),
        name="out_proj_final" if final_gain is not None else "out_proj",
    )(*operands)


def _pad_cols(w, n):
    return jnp.pad(w, ((0, 0), (0, n - w.shape[1])))


def _prep_weights(weights):
    (norm_even, w_in_even, b_fox_f, lb_logits, hgrn_gain, w_out_even, norm_odd, w_in_odd, w_gla_gate, b_gla_gate,
     gla_gain, w_out_odd, final_norm) = weights
    w_even_t = w_in_even[0].T.astype(BF16)
    w_odd_t = w_in_odd[0].T.astype(BF16)
    pad_rows = lambda a: jnp.pad(a, ((0, LANES - a.shape[0]), (0, 0)))
    return dict(
        norm_even=norm_even[0].reshape(1, D_MODEL),
        w_even=w_even_t,
        w_fb=pad_rows(w_even_t[N_EVEN_MAIN:]),
        b_fox=_pad_cols(b_fox_f[0].reshape(1, B_HEADS), LANES),
        lb_logits=lb_logits,
        hgrn_gain=hgrn_gain[0].reshape(1, A_WIDTH),
        w_out_even=w_out_even[0].astype(BF16),
        norm_odd=norm_odd[0].reshape(1, D_MODEL),
        w_odd=w_odd_t,
        w_r=pad_rows(w_odd_t[N_ODD_MAIN:]),
        w_gate=jnp.pad(w_gla_gate[0], ((0, LANES - C_GATE_RANK), (0, 0))).astype(BF16),
        b_gate=b_gla_gate[0].reshape(1, C_KEY_WIDTH),
        gla_gain=gla_gain[0].reshape(1, C_VAL_WIDTH),
        w_out_odd=w_out_odd[0].astype(BF16),
        final_norm=final_norm.reshape(1, D_MODEL),
    )


def _forward(x_prompt, x_sample, ctx, w):
    bsz, seq, _ = x_prompt.shape
    db, t_new, _ = x_sample.shape
    mp, ms = bsz * seq, db * t_new
    xp = x_prompt.reshape(mp, D_MODEL)
    xs = x_sample.reshape(ms, D_MODEL)
    hg = dict(heads=A_HEADS, dk=A_HEAD_DIM, dv=A_HEAD_DIM, q_scale=1.0)
    gl = dict(heads=C_HEADS, dk=C_KEY_DIM, dv=C_VAL_DIM, q_scale=C_KEY_DIM ** -0.5)

    even = lambda x2, s, tm: _even_proj(x2, w["norm_even"], w["w_even"], w["w_fb"], w["b_fox"], seq=s, tm=tm)
    pa_p, gate_p, qa_p, ka_p, vb_p, krows_p, vrows_p, lf_p, _ = even(xp, seq, 512)
    pa_s, gate_s, qa_s, _, _, krows_s, vrows_s, lf_s, c_s = even(xs, t_new, ms)

    oa_p, sa_p = _hgrn_prompt(pa_p, w["lb_logits"], w["hgrn_gain"], bsz=bsz, seq=seq, tb=512, **hg)
    oa_s, sa_s = _gla_sample(True, [pa_s] * 4, [0, 1, 2, 3], w["lb_logits"], w["hgrn_gain"], ctx["state_hgrn"],
                             t_new=t_new, **hg)
    nrow = t_new * B_HEADS
    rows = lambda a: a.reshape(db, nrow, B_HEAD_DIM)
    cn = c_s[:, :B_HEADS].reshape(db, nrow)
    q_rows = qa_s.reshape(db, nrow, 2 * B_HEAD_DIM)[:, :, :B_HEAD_DIM]
    ob_p, ob_s = _fox_attention(qa_p, ka_p, vb_p, gate_p, ctx["page_table"], ctx["k_pages"], ctx["v_pages"],
                                ctx["lf_pages"], q_rows, rows(krows_s), rows(vrows_s), rows(gate_s),
                                cn.reshape(db, nrow, 1), cn.reshape(db, 1, nrow), bsz=bsz, seq=seq,
                                tq=min(seq, 1024), pages_per_unit=8)

    y_p, sc_p = _layer1_prompt(xp, oa_p, ob_p, w, bsz=bsz, seq=seq, tb=512, **gl)
    x1_s = _out_proj([oa_s, ob_s.reshape(ms, B_WIDTH)], w["w_out_even"], xs, None, tm=ms)
    proj1, lf1 = _odd_proj(x1_s, w["norm_odd"], w["w_odd"], w["w_r"], w["w_gate"], w["b_gate"], tm=ms)
    oc_s, sc_s = _gla_sample(False, [proj1, proj1, proj1, proj1, lf1], [0, 1, 1, 2, 0], None, w["gla_gain"],
                             ctx["state_gla"], t_new=t_new, **gl)
    y_s = _out_proj([oc_s], w["w_out_odd"], x1_s, w["final_norm"], tm=ms)

    heads4 = lambda a, n, s: a.reshape(n, s, B_HEADS, B_HEAD_DIM)
    logf = lambda a, n, s: a[:, :B_HEADS].reshape(n, s, B_HEADS)
    prompt = (y_p.reshape(bsz, seq, D_MODEL), heads4(krows_p, bsz, seq), heads4(vrows_p, bsz, seq),
              logf(lf_p, bsz, seq), sa_p, sc_p)
    sample = (y_s.reshape(db, t_new, D_MODEL), heads4(krows_s, db, t_new), heads4(vrows_s, db, t_new),
              logf(lf_s, db, t_new), sa_s, sc_s)
    return prompt, sample


def kernel(x_prompt, x_sample, cache_fox_k, cache_fox_v, cache_fox_logf, state_hgrn, state_gla, page_table,
           norm_even, w_in_even, b_fox_f, lb_logits, hgrn_gain, w_out_even, norm_odd, w_in_odd, w_gla_gate,
           b_gla_gate, gla_gain, w_out_odd, final_norm):
    weights = _prep_weights((norm_even, w_in_even, b_fox_f, lb_logits, hgrn_gain, w_out_even, norm_odd, w_in_odd,
                             w_gla_gate, b_gla_gate, gla_gain, w_out_odd, final_norm))
    bsz, seq, _ = x_prompt.shape
    n_pool = cache_fox_k.shape[1]
    pw = PAGE_SIZE * B_HEADS
    ctx = dict(
        page_table=page_table,
        k_pages=cache_fox_k[0].reshape(n_pool, pw, B_HEAD_DIM),
        v_pages=cache_fox_v[0].reshape(n_pool, pw, B_HEAD_DIM),
        lf_pages=cache_fox_logf[0].transpose(0, 2, 1).reshape(n_pool, pw),
        state_hgrn=state_hgrn[0], state_gla=state_gla[0])
    (y_p, kp, vp, lfp, hgrn_p, gla_p), (y_s, ks, vs, lfs, hgrn_s, gla_s) = _forward(x_prompt, x_sample, ctx, weights)
    n_pp = seq // PAGE_SIZE
    fox_k_prompt = kp.reshape(1, bsz, n_pp, PAGE_SIZE, B_HEADS, B_HEAD_DIM)
    fox_v_prompt = vp.reshape(1, bsz, n_pp, PAGE_SIZE, B_HEADS, B_HEAD_DIM)
    fox_logf_prompt = lfp.reshape(1, bsz, n_pp, PAGE_SIZE, B_HEADS)
    return (y_p, y_s, fox_k_prompt, fox_v_prompt, fox_logf_prompt, hgrn_p[None], gla_p[None],
            ks[None], vs[None], lfs[None], hgrn_s[None], gla_s[None])
```

```python
import functools

import numpy as np
import jax
import jax.numpy as jnp
from jax import lax
from jax.experimental import pallas as pl
from jax.experimental.pallas import tpu as pltpu

F32 = jnp.float32
BF16 = jnp.bfloat16

D_MODEL = 1024
PAGE_SIZE = 128
A_HEADS = 4
A_HEAD_DIM = 128
A_WIDTH = 512
B_HEADS = 4
B_HEAD_DIM = 128
B_WIDTH = 512
C_HEADS = 4
C_KEY_WIDTH = 512
C_VAL_WIDTH = 1024
C_KEY_DIM = 128
C_VAL_DIM = 256
C_GATE_RANK = 16
GLA_GATE_NORMALIZER = 16.0
EPS = 1e-6
HGRN_LAYER = 0
LOG2E = 1.4426950408889634
AUG_PIECES = 3
N_EVEN_MAIN = 4 * A_WIDTH + 4 * B_WIDTH
N_ODD_MAIN = 2 * C_KEY_WIDTH + 2 * C_VAL_WIDTH

LANES = 128
SUBLANES = 8
VMEM_LIMIT = 56 * 1024 * 1024
CHUNK = 64
MAX_CHUNK_LOG_DECAY = 60.0

_NT = (((1,), (1,)), ((), ()))
_TN = (((0,), (0,)), ((), ()))


def _cparams(sem):
    return pltpu.CompilerParams(dimension_semantics=sem, vmem_limit_bytes=VMEM_LIMIT)


def _sigmoid(x):
    return 1.0 / (1.0 + jnp.exp(-x))


def _log_sigmoid(x):
    return jnp.minimum(x, 0.0) - jnp.log1p(jnp.exp(-jnp.abs(x)))


def _silu(x):
    return x * _sigmoid(x)


def _rmsnorm_rows(x, g):
    return x * lax.rsqrt(jnp.mean(x * x, axis=-1, keepdims=True) + EPS) * g


def _split3(x):
    p1 = x.astype(BF16)
    r1 = x - p1.astype(F32)
    p2 = r1.astype(BF16)
    p3 = (r1 - p2.astype(F32)).astype(BF16)
    return p1, p2, p3


def _dot01(m01, x):
    acc = None
    for p in _split3(x):
        t = jnp.dot(m01, p, preferred_element_type=F32)
        acc = t if acc is None else acc + t
    return acc


def _x_dot01(x, m01):
    acc = None
    for p in _split3(x):
        t = jnp.dot(p, m01, preferred_element_type=F32)
        acc = t if acc is None else acc + t
    return acc


def _lower_tri(n, seq):
    r = lax.broadcasted_iota(jnp.int32, (n, n), 0)
    c = lax.broadcasted_iota(jnp.int32, (n, n), 1)
    keep = r >= c
    if seq < n:
        keep = jnp.logical_and(keep, (r // seq) == (c // seq))
    return jnp.where(keep, 1.0, 0.0).astype(BF16)


def _dot_wt(h, wt_ref, c0, n):
    return lax.dot_general(h, wt_ref[c0:c0 + n, :], _NT, preferred_element_type=F32)


def _proj_cols(h, wt_ref, out_ref):
    step = 512
    for c0 in range(0, wt_ref.shape[0], step):
        out_ref[:, c0:c0 + step] = _dot_wt(h, wt_ref, c0, step)


def _aug_selectors():
    sel_q = np.zeros((LANES, B_WIDTH), np.float32)
    sel_k = np.zeros((LANES, B_WIDTH), np.float32)
    for p in range(AUG_PIECES):
        for hd in range(B_HEADS):
            sel_q[p * B_HEADS + hd, hd * B_HEAD_DIM + p] = 1.0
            sel_k[p * B_HEADS + hd, hd * B_HEAD_DIM + AUG_PIECES + p] = -1.0
    return jnp.asarray(sel_q, BF16), jnp.asarray(sel_k, BF16)


def _even_proj_kernel(x_ref, g_ref, w_ref, wfb_ref, bfox_ref, selq_ref, selk_ref, pa_ref, gate_ref, qa_ref, ka_ref,
                      vb_ref, kout_ref, vout_ref, lf_ref, c_ref, carry_ref, *, tm, seq):
    i = pl.program_id(0)
    h = _rmsnorm_rows(x_ref[...], g_ref[...]).astype(BF16)
    na = 4 * A_WIDTH
    bw = B_WIDTH
    dh = B_HEAD_DIM
    for c0 in range(0, na, 512):
        pa_ref[:, c0:c0 + 512] = _dot_wt(h, w_ref, c0, 512)
    q = (_dot_wt(h, w_ref, na, bw) * (dh ** -0.5 * LOG2E)).astype(BF16)
    for hd in range(B_HEADS):
        qa_ref[:, 2 * hd * dh:(2 * hd + 1) * dh] = q[:, hd * dh:(hd + 1) * dh]
    for j, out_ref in ((1, kout_ref), (2, vout_ref)):
        kv = _dot_wt(h, w_ref, na + j * bw, bw)
        kv16 = kv.astype(BF16)
        if j == 1:
            for hd in range(B_HEADS):
                ka_ref[:, 2 * hd * dh:(2 * hd + 1) * dh] = kv16[:, hd * dh:(hd + 1) * dh]
        else:
            vb_ref[...] = kv16
        for hd in range(B_HEADS):
            out_ref[pl.ds(hd, tm, stride=B_HEADS), :] = kv[:, hd * dh:(hd + 1) * dh]
    gate_ref[...] = _dot_wt(h, w_ref, na + 3 * bw, bw)
    fb = _dot_wt(h, wfb_ref, 0, LANES) + bfox_ref[...]
    lane = lax.broadcasted_iota(jnp.int32, fb.shape, 1)
    lf = jnp.where(lane < B_HEADS, _log_sigmoid(fb), 0.0)
    lf_ref[...] = lf
    cs = _dot01(_lower_tri(tm, seq), lf)
    if seq > tm:
        @pl.when((i * tm) % seq == 0)
        def _():
            carry_ref[...] = jnp.zeros_like(carry_ref)
        cs = cs + carry_ref[0:1, :]
        carry_ref[0:1, :] = cs[tm - 1:tm, :]
    c_ref[...] = cs
    pos = lax.broadcasted_iota(jnp.int32, (1, bw), 1) % dh
    aug_q = jnp.where(jnp.logical_and(pos >= AUG_PIECES, pos < 2 * AUG_PIECES), 1.0, 0.0)
    aug_k = jnp.where(pos < AUG_PIECES, 1.0, 0.0)
    packed = None
    for p, piece in enumerate(_split3(cs * LOG2E)):
        shifted = piece.astype(F32) if p == 0 else pltpu.roll(piece.astype(F32), p * B_HEADS, 1)
        packed = shifted if packed is None else packed + shifted
    packed = packed.astype(BF16)
    aug_q = (aug_q + jnp.dot(packed, selq_ref[...], preferred_element_type=F32)).astype(BF16)
    aug_k = (aug_k + jnp.dot(packed, selk_ref[...], preferred_element_type=F32)).astype(BF16)
    for hd in range(B_HEADS):
        qa_ref[:, (2 * hd + 1) * dh:(2 * hd + 2) * dh] = aug_q[:, hd * dh:(hd + 1) * dh]
        ka_ref[:, (2 * hd + 1) * dh:(2 * hd + 2) * dh] = aug_k[:, hd * dh:(hd + 1) * dh]


def _even_proj(x, g, w_main, w_fb, b_fox, *, seq, tm):
    m = x.shape[0]
    const = lambda i: (0, 0)
    row = lambda i: (i, 0)
    sel_q, sel_k = _aug_selectors()
    return pl.pallas_call(
        functools.partial(_even_proj_kernel, tm=tm, seq=seq),
        grid=(m // tm,),
        in_specs=[pl.BlockSpec((tm, D_MODEL), row), pl.BlockSpec((1, D_MODEL), const),
                  pl.BlockSpec((N_EVEN_MAIN, D_MODEL), const), pl.BlockSpec(w_fb.shape, const),
                  pl.BlockSpec((1, LANES), const), pl.BlockSpec(sel_q.shape, const),
                  pl.BlockSpec(sel_k.shape, const)],
        out_specs=[pl.BlockSpec((tm, 4 * A_WIDTH), row), pl.BlockSpec((tm, B_WIDTH), row),
                   pl.BlockSpec((tm, 2 * B_WIDTH), row), pl.BlockSpec((tm, 2 * B_WIDTH), row),
                   pl.BlockSpec((tm, B_WIDTH), row), pl.BlockSpec((tm * B_HEADS, B_HEAD_DIM), row),
                   pl.BlockSpec((tm * B_HEADS, B_HEAD_DIM), row), pl.BlockSpec((tm, LANES), row),
                   pl.BlockSpec((tm, LANES), row)],
        out_shape=[jax.ShapeDtypeStruct((m, 4 * A_WIDTH), F32), jax.ShapeDtypeStruct((m, B_WIDTH), F32),
                   jax.ShapeDtypeStruct((m, 2 * B_WIDTH), BF16), jax.ShapeDtypeStruct((m, 2 * B_WIDTH), BF16),
                   jax.ShapeDtypeStruct((m, B_WIDTH), BF16),
                   jax.ShapeDtypeStruct((m * B_HEADS, B_HEAD_DIM), F32),
                   jax.ShapeDtypeStruct((m * B_HEADS, B_HEAD_DIM), F32),
                   jax.ShapeDtypeStruct((m, LANES), F32), jax.ShapeDtypeStruct((m, LANES), F32)],
        scratch_shapes=[pltpu.VMEM((SUBLANES, LANES), F32)],
        compiler_params=_cparams(("arbitrary",)),
        name="even_proj",
    )(x, g, w_main, w_fb, b_fox, sel_q, sel_k)


def _odd_proj_kernel(x_ref, g_ref, w_ref, wr_ref, wg_ref, bg_ref, proj_ref, lf_ref):
    h = _rmsnorm_rows(x_ref[...], g_ref[...]).astype(BF16)
    _proj_cols(h, w_ref, proj_ref)
    r = _dot_wt(h, wr_ref, 0, LANES)
    z = jnp.dot(r.astype(BF16), wg_ref[...], preferred_element_type=F32) + bg_ref[...]
    lf_ref[...] = _log_sigmoid(z) / GLA_GATE_NORMALIZER


def _odd_proj(x, g, w_main, w_r, w_gate, b_gate, *, tm):
    m = x.shape[0]
    n = N_ODD_MAIN
    const = lambda i: (0, 0)
    row = lambda i: (i, 0)
    return pl.pallas_call(
        _odd_proj_kernel,
        grid=(m // tm,),
        in_specs=[pl.BlockSpec((tm, D_MODEL), row), pl.BlockSpec((1, D_MODEL), const),
                  pl.BlockSpec((n, D_MODEL), const), pl.BlockSpec(w_r.shape, const),
                  pl.BlockSpec((LANES, C_KEY_WIDTH), const), pl.BlockSpec((1, C_KEY_WIDTH), const)],
        out_specs=[pl.BlockSpec((tm, n), row), pl.BlockSpec((tm, C_KEY_WIDTH), row)],
        out_shape=[jax.ShapeDtypeStruct((m, n), F32), jax.ShapeDtypeStruct((m, C_KEY_WIDTH), F32)],
        compiler_params=_cparams(("arbitrary",)),
        name="odd_proj",
    )(x, g, w_main, w_r, w_gate, b_gate)


def _lower_bound(logits, layer):
    e = jnp.exp(logits - jnp.max(logits, axis=0, keepdims=True))
    return jnp.sum(e[:layer + 1, :], axis=0, keepdims=True) / jnp.sum(e, axis=0, keepdims=True)


def _hgrn_gate(fa, lb):
    f = lb + (1.0 - lb) * _sigmoid(fa)
    return jnp.log(f), 1.0 - f


def _exact_group(q, k, v, g, st, lo, hi):
    n = SUBLANES
    row = lax.broadcasted_iota(jnp.int32, (n, 1), 0)
    valid = jnp.logical_and(row >= lo, row < hi)
    q = jnp.where(valid, q, 0.0)
    k = jnp.where(valid, k, 0.0)
    g = jnp.where(valid, g, 0.0)
    b = g
    for sh in (1, 2, 4):
        b = b + jnp.where(row >= sh, pltpu.roll(b, sh, 0), 0.0)
    o = lax.dot_general((q * jnp.exp(b)).astype(BF16), st.astype(BF16), _NT, preferred_element_type=F32)
    for s in range(lo, hi):
        w = jnp.exp(jnp.minimum(b - b[s:s + 1, :], 0.0))
        a = jnp.sum(q * k[s:s + 1, :] * w, axis=-1, keepdims=True)
        o = o + jnp.where(row >= s, a, 0.0) * v[s:s + 1, :]
    b_last = b[n - 1:n, :]
    k_hat = k * jnp.exp(b_last - b)
    st_new = st * jnp.exp(b_last) + lax.dot_general(v.astype(BF16), k_hat.astype(BF16), _TN,
                                                    preferred_element_type=F32)
    return o, st_new


def _head_norm_gate(o, gain, gate):
    y = o * lax.rsqrt(jnp.mean(o * o, axis=-1, keepdims=True) + EPS) * gain
    return y * _silu(gate)


def _recurrence_block(q_ref, k_ref, v_ref, gate_ref, g_ref, gain_ref, o_ref, st_ref, b_ref, *, fa_ref, lb, heads, dk,
                      dv, tb, q_scale):
    n_chunks = tb // CHUNK

    tri = _lower_tri(CHUNK, CHUNK)
    b_min = None
    for c in range(n_chunks):
        rows = slice(c * CHUNK, (c + 1) * CHUNK)
        if fa_ref is not None:
            g, k = _hgrn_gate(fa_ref[rows, :], lb)
            g_ref[rows, :] = g
            k_ref[rows, :] = k
        else:
            g = g_ref[rows, :]
        b = _dot01(tri, g)
        b_ref[rows, :] = b
        b_last = b[CHUNK - 1:CHUNK, :]
        b_min = b_last if b_min is None else jnp.minimum(b_min, b_last)
    chunk_form_ok = jnp.min(b_min) >= -MAX_CHUNK_LOG_DECAY

    def finish(rows, h, o_h):
        vsl = slice(h * dv, (h + 1) * dv)
        o_ref[rows, vsl] = _head_norm_gate(o_h, gain_ref[:, vsl], gate_ref[rows, vsl]).astype(o_ref.dtype)

    @pl.when(chunk_form_ok)
    def _():
        r_i = lax.broadcasted_iota(jnp.int32, (CHUNK, CHUNK), 0)
        c_i = lax.broadcasted_iota(jnp.int32, (CHUNK, CHUNK), 1)
        causal = r_i >= c_i

        for c in range(n_chunks):
            rows = slice(c * CHUNK, (c + 1) * CHUNK)
            b = b_ref[rows, :]
            e_b = jnp.exp(b)
            b_last = b[CHUNK - 1:CHUNK, :]
            e_last = jnp.exp(b_last)
            q_t = q_ref[rows, :] * q_scale * e_b
            k_t = k_ref[rows, :] * jnp.exp(-b)
            k_hat = k_t * e_last
            for h in range(heads):
                ksl = slice(h * dk, (h + 1) * dk)
                vsl = slice(h * dv, (h + 1) * dv)
                qh = q_t[:, ksl].astype(BF16)
                vh = v_ref[rows, vsl].astype(BF16)
                a = lax.dot_general(qh, k_t[:, ksl].astype(BF16), _NT, preferred_element_type=F32)
                a = jnp.where(causal, a, 0.0).astype(BF16)
                st = st_ref[h]
                o_h = jnp.dot(a, vh, preferred_element_type=F32) + lax.dot_general(
                    qh, st.astype(BF16), _NT, preferred_element_type=F32)
                st_ref[h] = st * e_last[:, ksl] + lax.dot_general(
                    vh, k_hat[:, ksl].astype(BF16), _TN, preferred_element_type=F32)
                finish(rows, h, o_h)

    @pl.when(jnp.logical_not(chunk_form_ok))
    def _():
        pair = 2 * SUBLANES

        def group(i, carry):
            for h in range(heads):
                ksl = slice(h * dk, (h + 1) * dk)
                vsl = slice(h * dv, (h + 1) * dv)
                outs = []
                for half in range(2):
                    rows = pl.ds(pl.multiple_of(i * pair + half * SUBLANES, SUBLANES), SUBLANES)
                    o_h, st_new = _exact_group(q_ref[rows, ksl] * q_scale, k_ref[rows, ksl], v_ref[rows, vsl],
                                               g_ref[rows, ksl], st_ref[h], 0, SUBLANES)
                    st_ref[h] = st_new
                    outs.append(o_h)
                finish(pl.ds(pl.multiple_of(i * pair, pair), pair), h, jnp.concatenate(outs, axis=0))
            return carry

        lax.fori_loop(0, tb // pair, group, 0)


def _state_step_edges(st_ref, s_ref):
    t = pl.program_id(1)

    def first():
        @pl.when(t == 0)
        def _():
            st_ref[...] = jnp.zeros_like(st_ref)

    def last():
        @pl.when(t == pl.num_programs(1) - 1)
        def _():
            for h in range(st_ref.shape[0]):
                s_ref[0, h] = st_ref[h].T

    return first, last


def _hgrn_prompt_kernel(q_ref, fa_ref, v_ref, gate_ref, lb_ref, gain_ref, o_ref, s_ref, st_ref, b_ref, g_ref, k_ref,
                        **kw):
    first, last = _state_step_edges(st_ref, s_ref)
    first()
    _recurrence_block(q_ref, k_ref, v_ref, gate_ref, g_ref, gain_ref, o_ref, st_ref, b_ref, fa_ref=fa_ref,
                      lb=_lower_bound(lb_ref[...], HGRN_LAYER), **kw)
    last()


def _hgrn_prompt(pa, lb_logits, gain, *, bsz, seq, heads, dk, dv, tb, q_scale):
    nt = seq // tb
    kw, vw = heads * dk, heads * dv
    col = lambda cb: (lambda b, t: (b * nt + t, cb))
    const = lambda b, t: (0, 0)
    return pl.pallas_call(
        functools.partial(_hgrn_prompt_kernel, heads=heads, dk=dk, dv=dv, tb=tb, q_scale=q_scale),
        grid=(bsz, nt),
        in_specs=[pl.BlockSpec((tb, kw), col(0)), pl.BlockSpec((tb, kw), col(1)), pl.BlockSpec((tb, vw), col(2)),
                  pl.BlockSpec((tb, vw), col(3)), pl.BlockSpec(lb_logits.shape, const),
                  pl.BlockSpec((1, vw), const)],
        out_specs=[pl.BlockSpec((tb, vw), col(0)), pl.BlockSpec((1, heads, dk, dv), lambda b, t: (b, 0, 0, 0))],
        out_shape=[jax.ShapeDtypeStruct((bsz * seq, vw), BF16), jax.ShapeDtypeStruct((bsz, heads, dk, dv), F32)],
        scratch_shapes=[pltpu.VMEM((heads, dv, dk), F32), pltpu.VMEM((tb, kw), F32), pltpu.VMEM((tb, kw), F32),
                        pltpu.VMEM((tb, kw), F32)],
        compiler_params=_cparams(("arbitrary", "arbitrary")),
        name="hgrn_prompt",
    )(pa, pa, pa, pa, lb_logits, gain)


def _layer1_prompt_kernel(x_ref, oa_ref, ob_ref, woe_ref, gn_ref, w_ref, wr_ref, wg_ref, bg_ref, gain_ref, woo_ref,
                          fn_ref, y_ref, s_ref, st_ref, b_ref, proj_ref, lf_ref, o_ref, *, heads, dk, dv, tb,
                          q_scale):
    first, last = _state_step_edges(st_ref, s_ref)
    first()
    kw, vw = heads * dk, heads * dv
    aw = oa_ref.shape[1]
    x1 = (x_ref[...] + jnp.dot(oa_ref[...], woe_ref[0:aw, :], preferred_element_type=F32)
          + jnp.dot(ob_ref[...], woe_ref[aw:, :], preferred_element_type=F32))
    h = _rmsnorm_rows(x1, gn_ref[...]).astype(BF16)
    _proj_cols(h, w_ref, proj_ref)
    r = _dot_wt(h, wr_ref, 0, LANES)
    z = jnp.dot(r.astype(BF16), wg_ref[...], preferred_element_type=F32) + bg_ref[...]
    lf_ref[...] = _log_sigmoid(z) / GLA_GATE_NORMALIZER
    _recurrence_block(proj_ref.at[:, 0:kw], proj_ref.at[:, kw:2 * kw], proj_ref.at[:, 2 * kw:2 * kw + vw],
                      proj_ref.at[:, 2 * kw + vw:2 * kw + 2 * vw], lf_ref, gain_ref, o_ref, st_ref, b_ref,
                      fa_ref=None, lb=None, heads=heads, dk=dk, dv=dv, tb=tb, q_scale=q_scale)
    y = x1 + jnp.dot(o_ref[...], woo_ref[...], preferred_element_type=F32)
    y_ref[...] = _rmsnorm_rows(y, fn_ref[...])
    last()


def _layer1_prompt(x, o_a, o_b, w, *, bsz, seq, heads, dk, dv, tb, q_scale):
    nt = seq // tb
    kw, vw = heads * dk, heads * dv
    row = lambda b, t: (b * nt + t, 0)
    const = lambda b, t: (0, 0)
    full = lambda a: pl.BlockSpec((N_ODD_MAIN, D_MODEL) if a is w["w_odd"] else a.shape, const)
    weights = [w["w_out_even"], w["norm_odd"], w["w_odd"], w["w_r"], w["w_gate"], w["b_gate"], w["gla_gain"],
               w["w_out_odd"], w["final_norm"]]
    return pl.pallas_call(
        functools.partial(_layer1_prompt_kernel, heads=heads, dk=dk, dv=dv, tb=tb, q_scale=q_scale),
        grid=(bsz, nt),
        in_specs=[pl.BlockSpec((tb, D_MODEL), row), pl.BlockSpec((tb, o_a.shape[1]), row),
                  pl.BlockSpec((tb, o_b.shape[1]), row)] + [full(a) for a in weights],
        out_specs=[pl.BlockSpec((tb, D_MODEL), row), pl.BlockSpec((1, heads, dk, dv), lambda b, t: (b, 0, 0, 0))],
        out_shape=[jax.ShapeDtypeStruct((bsz * seq, D_MODEL), F32),
                   jax.ShapeDtypeStruct((bsz, heads, dk, dv), F32)],
        scratch_shapes=[pltpu.VMEM((heads, dv, dk), F32), pltpu.VMEM((tb, kw), F32),
                        pltpu.VMEM((tb, 2 * kw + 2 * vw), F32), pltpu.VMEM((tb, kw), F32),
                        pltpu.VMEM((tb, vw), BF16)],
        compiler_params=_cparams(("arbitrary", "arbitrary")),
        name="layer1_prompt",
    )(x, o_a, o_b, *weights)


SAMPLE_GROUPS = 1


def _gla_sample_kernel(*refs, hgrn, heads, dk, dv, t_new, q_scale, groups):
    if hgrn:
        q_ref, fa_ref, v_ref, gate_ref, lb_ref, gain_ref, s0_ref, o_ref, s_ref = refs
    else:
        q_ref, k_ref, v_ref, gate_ref, g_ref, gain_ref, s0_ref, o_ref, s_ref = refs
    per = SUBLANES // t_new
    for grp in range(groups):
        rows = slice(grp * SUBLANES, (grp + 1) * SUBLANES)
        for h in range(heads):
            ksl = slice(h * dk, (h + 1) * dk)
            vsl = slice(h * dv, (h + 1) * dv)
            if hgrn:
                g, k = _hgrn_gate(fa_ref[rows, ksl], _lower_bound(lb_ref[...], HGRN_LAYER)[:, ksl])
            else:
                g, k = g_ref[rows, ksl], k_ref[rows, ksl]
            q = q_ref[rows, ksl] * q_scale
            v = v_ref[rows, vsl]
            o_h = None
            for e in range(per):
                o_e, st_new = _exact_group(q, k, v, g, s0_ref[grp * per + e, h].T, e * t_new, (e + 1) * t_new)
                s_ref[grp * per + e, h] = st_new.T
                o_h = o_e if o_h is None else o_h + o_e
            o_ref[rows, vsl] = _head_norm_gate(o_h, gain_ref[:, vsl], gate_ref[rows, vsl])


def _gla_sample(hgrn, arrays, col_blocks, small, gain, s0, *, t_new, heads, dk, dv, q_scale):
    m = arrays[0].shape[0]
    groups = SAMPLE_GROUPS if m % (SAMPLE_GROUPS * SUBLANES) == 0 else 1
    tr = groups * SUBLANES
    per = tr // t_new
    kw, vw = heads * dk, heads * dv
    widths = [kw, kw, vw, vw] + ([] if hgrn else [kw])
    in_specs = [pl.BlockSpec((tr, w), functools.partial(lambda i, cb: (i, cb), cb=cb))
                for w, cb in zip(widths, col_blocks)]
    operands = list(arrays)
    if hgrn:
        in_specs.append(pl.BlockSpec(small.shape, lambda i: (0, 0)))
        operands.append(small)
    in_specs.append(pl.BlockSpec((1, vw), lambda i: (0, 0)))
    operands.append(gain)
    in_specs.append(pl.BlockSpec((per, heads, dk, dv), lambda i: (i, 0, 0, 0)))
    operands.append(s0)
    return pl.pallas_call(
        functools.partial(_gla_sample_kernel, hgrn=hgrn, heads=heads, dk=dk, dv=dv, t_new=t_new, q_scale=q_scale,
                          groups=groups),
        grid=(m // tr,),
        in_specs=in_specs,
        out_specs=[pl.BlockSpec((tr, vw), lambda i: (i, 0)),
                   pl.BlockSpec((per, heads, dk, dv), lambda i: (i, 0, 0, 0))],
        out_shape=[jax.ShapeDtypeStruct((m, vw), F32), jax.ShapeDtypeStruct(s0.shape, F32)],
        compiler_params=_cparams(("arbitrary",)),
        name="hgrn_sample" if hgrn else "gla_sample",
    )(*operands)


FOX_STRIP = 64


def _fox_prompt_step(qi, ki, before_head, q_ref, k_ref, v_ref, gate_ref, o_ref, m_ref, l_ref, acc_ref, p_ref, a_ref,
                     *, tq, heads, dh):
    aw = 2 * dh

    @pl.when(ki == 0)
    def _():
        m_ref[...] = jnp.full_like(m_ref, -jnp.inf)
        l_ref[...] = jnp.zeros_like(l_ref)
        acc_ref[...] = jnp.zeros_like(acc_ref)

    def step(diag):
        if diag:
            r_i = lax.broadcasted_iota(jnp.int32, (FOX_STRIP, LANES), 0)
            c_i = lax.broadcasted_iota(jnp.int32, (FOX_STRIP, LANES), 1)
        half = tq // 2
        segments = [(0, half, half), (half, tq, tq)] if diag and half % LANES == 0 else [(0, tq, tq)]
        for h in range(heads):
            interleaved_update = before_head(h)
            sl = slice(h * dh, (h + 1) * dh)
            hw = slice(h * aw, (h + 1) * aw)
            for lo, hi, ext in segments:
                s = lax.dot_general(q_ref[lo:hi, hw], k_ref[0:ext, hw], _NT, preferred_element_type=F32)
                if lo == 0:
                    after_head = interleaved_update()
                for r0 in range(lo, hi, FOX_STRIP):
                    rows = slice(r0, r0 + FOX_STRIP)
                    srows = slice(r0 - lo, r0 - lo + FOX_STRIP)
                    live = [j for j in range(ext // LANES) if not (diag and j * LANES > r0 + FOX_STRIP - 1)]
                    blocks = []
                    for j in live:
                        lg = s[srows, j * LANES:(j + 1) * LANES]
                        if diag and (j + 1) * LANES - 1 > r0:
                            lg = jnp.where(r_i + r0 >= c_i + j * LANES, lg, -jnp.inf)
                        blocks.append(lg)
                    m_cur = blocks[0]
                    for lg in blocks[1:]:
                        m_cur = jnp.maximum(m_cur, lg)
                    m_prev = m_ref[h, rows]
                    m_new = jnp.maximum(m_prev, jnp.max(m_cur, axis=-1, keepdims=True))
                    alpha = jnp.exp2(m_prev - m_new)
                    probs = [jnp.exp2(lg - m_new) for lg in blocks]
                    row_sum = probs[0]
                    for pj in probs[1:]:
                        row_sum = row_sum + pj
                    l_ref[h, rows] = alpha * l_ref[h, rows] + jnp.sum(row_sum, axis=-1, keepdims=True)
                    m_ref[h, rows] = m_new
                    a_ref[rows] = alpha
                    for j, pj in zip(live, probs):
                        p_ref[rows, j * LANES:(j + 1) * LANES] = pj.astype(BF16)
                    for j in range(ext // LANES):
                        if j not in live:
                            p_ref[rows, j * LANES:(j + 1) * LANES] = jnp.zeros((FOX_STRIP, LANES), BF16)
                acc_ref[lo:hi, sl] = a_ref[lo:hi] * acc_ref[lo:hi, sl] + jnp.dot(
                    p_ref[lo:hi, 0:ext], v_ref[0:ext, sl], preferred_element_type=F32)
            after_head()

    @pl.when(ki < qi)
    def _():
        step(False)

    @pl.when(ki == qi)
    def _():
        step(True)
        for h in range(heads):
            sl = slice(h * dh, (h + 1) * dh)
            o_ref[:, sl] = (acc_ref[:, sl] / l_ref[h] * _silu(gate_ref[:, sl])).astype(o_ref.dtype)


N_PROMPT_REFS = 4
N_SAMPLE_REFS = 9


def _fox_kernel(qi_ref, ki_ref, pt_ref, *refs, tq, heads, dh, sample_kw):
    prompt_in = refs[:N_PROMPT_REFS]
    sample_in = refs[N_PROMPT_REFS:N_PROMPT_REFS + N_SAMPLE_REFS]
    o_ref, os_ref = refs[N_PROMPT_REFS + N_SAMPLE_REFS:N_PROMPT_REFS + N_SAMPLE_REFS + 2]
    scratch = refs[N_PROMPT_REFS + N_SAMPLE_REFS + 2:]
    prompt_scratch, sample_scratch = scratch[:5], scratch[5:]
    p = pl.program_id(1)
    step = pl.program_id(0) * pl.num_programs(1) + p

    def before_head(h):
        return _fox_sample_unit(step * heads + h, pt_ref, *sample_in, os_ref, *sample_scratch, heads=heads, dh=dh,
                                **sample_kw)

    _fox_prompt_step(qi_ref[p], ki_ref[p], before_head, *prompt_in, o_ref, *prompt_scratch, tq=tq, heads=heads, dh=dh)


def _fox_attention(qa, ka, vb, gate, page_table, k_pages, v_pages, lf_pages, q_s, k_new, v_new, gate_s, cn_col,
                   cn_row, *, bsz, seq, tq, pages_per_unit):
    nq = seq // tq
    pairs = [(qi, ki) for qi in range(nq) for ki in range(qi + 1)]
    qi_tab = jnp.asarray(np.array([p[0] for p in pairs], np.int32))
    ki_tab = jnp.asarray(np.array([p[1] for p in pairs], np.int32))
    w = B_WIDTH
    dh = B_HEAD_DIM
    n_pool = lf_pages.shape[0]
    sfx = _page_suffix(lf_pages, B_HEADS)
    db, n_pages = page_table.shape
    pp = pages_per_unit
    n_units = db * (n_pages // pp)
    assert bsz * len(pairs) * B_HEADS >= n_units, "not enough prompt steps to carry the sample page units"
    nrow = q_s.shape[1]
    pw = PAGE_SIZE * B_HEADS
    q_map = lambda b, p, qt, kt, pt: (b * nq + qt[p], 0)
    k_map = lambda b, p, qt, kt, pt: (b * nq + kt[p], 0)
    whole = lambda a: pl.BlockSpec(a.shape, lambda b, p, qt, kt, pt: (0,) * a.ndim)
    hbm = pl.BlockSpec(memory_space=pl.ANY)
    n_slots = 2
    grid_spec = pltpu.PrefetchScalarGridSpec(
        num_scalar_prefetch=3,
        grid=(bsz, len(pairs)),
        in_specs=[pl.BlockSpec((tq, 2 * w), q_map), pl.BlockSpec((tq, 2 * w), k_map),
                  pl.BlockSpec((tq, w), k_map), pl.BlockSpec((tq, w), q_map),
                  hbm, hbm, hbm, whole(q_s), whole(k_new), whole(v_new), whole(gate_s), whole(cn_col),
                  whole(cn_row)],
        out_specs=[pl.BlockSpec((tq, w), q_map),
                   pl.BlockSpec((db, nrow, dh), lambda b, p, qt, kt, pt: (0, 0, 0))],
        scratch_shapes=[pltpu.VMEM((B_HEADS, tq, LANES), F32), pltpu.VMEM((B_HEADS, tq, LANES), F32),
                        pltpu.VMEM((tq, w), F32), pltpu.VMEM((tq, tq), BF16), pltpu.VMEM((tq, LANES), F32),
                        pltpu.VMEM((n_slots, pp, pw, dh), F32), pltpu.VMEM((n_slots, pp, pw, dh), F32),
                        pltpu.VMEM((n_slots, pp, 1, 2 * pw), F32), pltpu.SemaphoreType.DMA((3, n_slots)),
                        pltpu.VMEM((nrow, 1), F32), pltpu.VMEM((nrow, 1), F32), pltpu.VMEM((nrow, dh), F32),
                        pltpu.VMEM((nrow, PAGE_SIZE), F32)],
    )
    sample_kw = dict(pp=pp, t_new=nrow // B_HEADS, n_pages=n_pages, n_units=n_units)
    return pl.pallas_call(
        functools.partial(_fox_kernel, tq=tq, heads=B_HEADS, dh=dh, sample_kw=sample_kw),
        grid_spec=grid_spec,
        out_shape=[jax.ShapeDtypeStruct((bsz * seq, w), BF16), jax.ShapeDtypeStruct((db, nrow, dh), F32)],
        compiler_params=_cparams(("arbitrary", "arbitrary")),
        name="fox_attention",
    )(qi_tab, ki_tab, page_table, qa, ka, vb, gate, k_pages, v_pages, sfx, q_s, k_new, v_new, gate_s, cn_col,
      cn_row)


def _page_suffix_kernel(lf_ref, later_ref, total_ref, out_ref):
    lf = lf_ref[...]
    pw = lf.shape[1]
    out_ref[:, 0:pw] = _x_dot01(lf, later_ref[...])
    out_ref[:, pw:2 * pw] = _x_dot01(lf, total_ref[...])


def _page_suffix(lf_pages, heads):
    n_pool, pw = lf_pages.shape
    tm = 512 if n_pool % 512 == 0 else n_pool
    j = np.arange(pw)
    in_head, in_tok = j[:, None] // (pw // heads), j[:, None] % (pw // heads)
    out_head, out_tok = j[None, :] // (pw // heads), j[None, :] % (pw // heads)
    head_eq = in_head == out_head
    later = jnp.asarray(head_eq & (in_tok > out_tok), BF16)
    total = jnp.asarray(head_eq, BF16)
    row = lambda i: (i, 0)
    const = lambda i: (0, 0)
    return pl.pallas_call(
        _page_suffix_kernel,
        grid=(n_pool // tm,),
        in_specs=[pl.BlockSpec((tm, pw), row), pl.BlockSpec((pw, pw), const), pl.BlockSpec((pw, pw), const)],
        out_specs=pl.BlockSpec((tm, 2 * pw), row),
        out_shape=jax.ShapeDtypeStruct((n_pool, 2 * pw), F32),
        compiler_params=_cparams(("arbitrary",)),
        name="page_suffix",
    )(lf_pages, later, total)


def _fox_sample_unit(u, pt_ref, k_hbm, v_hbm, sfx_hbm, q_ref, kn_ref, vn_ref, gate_ref, cn_col_ref, cn_row_ref, o_ref,
                     kbuf, vbuf, sbuf, sem, m_ref, l_ref, acc_ref, carry_ref, *, pp, heads, dh, t_new, n_pages,
                     n_units):
    nj = n_pages // pp
    valid = u < n_units
    uc = jnp.minimum(u, n_units - 1)
    b = lax.div(uc, nj)
    j = lax.rem(uc, nj)
    nrow = t_new * heads
    pw = PAGE_SIZE * heads
    slot = lax.rem(uc, 2)

    def page_copies(uu, sl):
        bb = lax.div(uu, nj)
        jj = lax.rem(uu, nj)
        copies = []
        for i in range(pp):
            page = pt_ref[bb, n_pages - 1 - (jj * pp + i)]
            sources = (k_hbm.at[page], v_hbm.at[page], sfx_hbm.at[pl.ds(page, 1)])
            for kind, (src, dst) in enumerate(zip(sources, (kbuf, vbuf, sbuf))):
                copies.append(pltpu.make_async_copy(src, dst.at[sl, i], sem.at[kind, sl]))
        return copies

    @pl.when(u == 0)
    def _():
        for cp in page_copies(u, slot):
            cp.start()

    @pl.when(u + 1 < n_units)
    def _():
        for cp in page_copies(u + 1, 1 - slot):
            cp.start()

    @pl.when(valid)
    def _():
        for cp in page_copies(u, slot):
            cp.wait()

    k_refs = [kbuf.at[slot, i] for i in range(pp)]
    v_refs = [vbuf.at[slot, i] for i in range(pp)]
    sfx_refs = [sbuf.at[slot, i] for i in range(pp)]

    @pl.when(j == 0)
    def _():
        m_ref[...] = jnp.full_like(m_ref, -jnp.inf)
        l_ref[...] = jnp.zeros_like(l_ref)
        acc_ref[...] = jnp.zeros_like(acc_ref)
        carry_ref[...] = jnp.zeros_like(carry_ref)

    q = q_ref[b]
    cn_col = cn_col_ref[b] * LOG2E
    row_head = lax.broadcasted_iota(jnp.int32, (nrow, dh), 0) % heads
    q_wide = jnp.concatenate([jnp.where(row_head == h, q, jnp.zeros_like(q)) for h in range(heads)], axis=1)

    def page_wide(ref):
        return jnp.concatenate([ref[pl.ds(h, PAGE_SIZE, stride=heads), :] for h in range(heads)],
                               axis=1).astype(BF16)

    def per_row_head(row):
        by_head = jnp.concatenate([row[:, h * PAGE_SIZE:(h + 1) * PAGE_SIZE] for h in range(heads)], axis=0)
        return jnp.concatenate([by_head] * t_new, axis=0)

    def own_head(wide):
        out = None
        for h in range(heads):
            part = jnp.where(row_head == h, wide[:, h * dh:(h + 1) * dh], 0.0)
            out = part if out is None else out + part
        return out

    def online(logit_list, weighted_values):
        m_cur = logit_list[0]
        for lg in logit_list[1:]:
            m_cur = jnp.maximum(m_cur, lg)
        m_prev = m_ref[...]
        m_new = jnp.maximum(m_prev, jnp.max(m_cur, axis=-1, keepdims=True))
        alpha = jnp.exp2(m_prev - m_new)
        probs = [jnp.exp2(lg - m_new) for lg in logit_list]
        row_sum = probs[0]
        for pj in probs[1:]:
            row_sum = row_sum + pj
        l_ref[...] = alpha * l_ref[...] + jnp.sum(row_sum, axis=-1, keepdims=True)
        acc_ref[...] = alpha * acc_ref[...] + weighted_values([pj.astype(BF16) for pj in probs])
        m_ref[...] = m_new

    logit_list = []

    def scores():
        carry = carry_ref[...]
        for i in range(pp):
            suffix = (carry + per_row_head(sfx_refs[i][:, 0:pw])) * LOG2E
            carry = carry + per_row_head(sfx_refs[i][:, pw:2 * pw])
            s = lax.dot_general(q_wide, page_wide(k_refs[i]), _NT, preferred_element_type=F32)
            logit_list.append(s + cn_col + suffix)
        carry_ref[...] = carry
        return update

    def page_values(probs):
        wide = None
        for pj, v_ref in zip(probs, v_refs):
            t = jnp.dot(pj, page_wide(v_ref), preferred_element_type=F32)
            wide = t if wide is None else wide + t
        return own_head(wide)

    def update():
        online(logit_list, page_values)
        return finish

    def finish():
        @pl.when(jnp.logical_and(valid, j == nj - 1))
        def _():
            s = lax.dot_general(q, kn_ref[b].astype(BF16), _NT, preferred_element_type=F32)
            logits = s + cn_col - cn_row_ref[b] * LOG2E
            r = lax.broadcasted_iota(jnp.int32, (nrow, nrow), 0)
            c = lax.broadcasted_iota(jnp.int32, (nrow, nrow), 1)
            keep = jnp.logical_and((r % heads) == (c % heads), (r // heads) >= (c // heads))
            online([jnp.where(keep, logits, -jnp.inf)],
                   lambda probs: jnp.dot(probs[0], vn_ref[b].astype(BF16), preferred_element_type=F32))
            o_ref[b] = acc_ref[...] / l_ref[...] * _silu(gate_ref[b])

    return scores()


def _out_proj_kernel(*refs, n_in, final):
    ins = refs[:n_in]
    w_ref, x_ref = refs[n_in], refs[n_in + 1]
    y = x_ref[...]
    k0 = 0
    for a_ref in ins:
        kw = a_ref.shape[1]
        y = y + jnp.dot(a_ref[...].astype(BF16), w_ref[k0:k0 + kw, :], preferred_element_type=F32)
        k0 += kw
    if final:
        g_ref, o_ref = refs[n_in + 2], refs[n_in + 3]
        o_ref[...] = _rmsnorm_rows(y, g_ref[...])
    else:
        refs[n_in + 2][...] = y


def _out_proj(ins, w, x, final_gain, *, tm):
    m = x.shape[0]
    const = lambda i: (0, 0)
    row = lambda i: (i, 0)
    in_specs = [pl.BlockSpec((tm, a.shape[1]), row) for a in ins]
    in_specs += [pl.BlockSpec(w.shape, const), pl.BlockSpec((tm, D_MODEL), row)]
    operands = list(ins) + [w, x]
    if final_gain is not None:
        in_specs.append(pl.BlockSpec((1, D_MODEL), const))
        operands.append(final_gain)
    return pl.pallas_call(
        functools.partial(_out_proj_kernel, n_in=len(ins), final=final_gain is not None),
        grid=(m // tm,),
        in_specs=in_specs,
        out_specs=pl.BlockSpec((tm, D_MODEL), row),
        out_shape=jax.ShapeDtypeStruct((m, D_MODEL), F32),
        compiler_params=_cparams(("arbitrary",)),
        name="out_proj_final" if final_gain is not None else "out_proj",
    )(*operands)


def _pad_cols(w, n):
    return jnp.pad(w, ((0, 0), (0, n - w.shape[1])))


def _prep_weights(weights):
    (norm_even, w_in_even, b_fox_f, lb_logits, hgrn_gain, w_out_even, norm_odd, w_in_odd, w_gla_gate, b_gla_gate,
     gla_gain, w_out_odd, final_norm) = weights
    w_even_t = w_in_even[0].T.astype(BF16)
    w_odd_t = w_in_odd[0].T.astype(BF16)
    pad_rows = lambda a: jnp.pad(a, ((0, LANES - a.shape[0]), (0, 0)))
    return dict(
        norm_even=norm_even[0].reshape(1, D_MODEL),
        w_even=w_even_t,
        w_fb=pad_rows(w_even_t[N_EVEN_MAIN:]),
        b_fox=_pad_cols(b_fox_f[0].reshape(1, B_HEADS), LANES),
        lb_logits=lb_logits,
        hgrn_gain=hgrn_gain[0].reshape(1, A_WIDTH),
        w_out_even=w_out_even[0].astype(BF16),
        norm_odd=norm_odd[0].reshape(1, D_MODEL),
        w_odd=w_odd_t,
        w_r=pad_rows(w_odd_t[N_ODD_MAIN:]),
        w_gate=jnp.pad(w_gla_gate[0], ((0, LANES - C_GATE_RANK), (0, 0))).astype(BF16),
        b_gate=b_gla_gate[0].reshape(1, C_KEY_WIDTH),
        gla_gain=gla_gain[0].reshape(1, C_VAL_WIDTH),
        w_out_odd=w_out_odd[0].astype(BF16),
        final_norm=final_norm.reshape(1, D_MODEL),
    )


def _forward(x_prompt, x_sample, ctx, w):
    bsz, seq, _ = x_prompt.shape
    db, t_new, _ = x_sample.shape
    mp, ms = bsz * seq, db * t_new
    xp = x_prompt.reshape(mp, D_MODEL)
    xs = x_sample.reshape(ms, D_MODEL)
    hg = dict(heads=A_HEADS, dk=A_HEAD_DIM, dv=A_HEAD_DIM, q_scale=1.0)
    gl = dict(heads=C_HEADS, dk=C_KEY_DIM, dv=C_VAL_DIM, q_scale=C_KEY_DIM ** -0.5)

    even = lambda x2, s, tm: _even_proj(x2, w["norm_even"], w["w_even"], w["w_fb"], w["b_fox"], seq=s, tm=tm)
    pa_p, gate_p, qa_p, ka_p, vb_p, krows_p, vrows_p, lf_p, _ = even(xp, seq, 512)
    pa_s, gate_s, qa_s, _, _, krows_s, vrows_s, lf_s, c_s = even(xs, t_new, ms)

    oa_p, sa_p = _hgrn_prompt(pa_p, w["lb_logits"], w["hgrn_gain"], bsz=bsz, seq=seq, tb=512, **hg)
    oa_s, sa_s = _gla_sample(True, [pa_s] * 4, [0, 1, 2, 3], w["lb_logits"], w["hgrn_gain"], ctx["state_hgrn"],
                             t_new=t_new, **hg)
    nrow = t_new * B_HEADS
    rows = lambda a: a.reshape(db, nrow, B_HEAD_DIM)
    cn = c_s[:, :B_HEADS].reshape(db, nrow)
    q_rows = qa_s.reshape(db, nrow, 2 * B_HEAD_DIM)[:, :, :B_HEAD_DIM]
    ob_p, ob_s = _fox_attention(qa_p, ka_p, vb_p, gate_p, ctx["page_table"], ctx["k_pages"], ctx["v_pages"],
                                ctx["lf_pages"], q_rows, rows(krows_s), rows(vrows_s), rows(gate_s),
                                cn.reshape(db, nrow, 1), cn.reshape(db, 1, nrow), bsz=bsz, seq=seq,
                                tq=min(seq, 1024), pages_per_unit=8)

    y_p, sc_p = _layer1_prompt(xp, oa_p, ob_p, w, bsz=bsz, seq=seq, tb=512, **gl)
    x1_s = _out_proj([oa_s, ob_s.reshape(ms, B_WIDTH)], w["w_out_even"], xs, None, tm=ms)
    proj1, lf1 = _odd_proj(x1_s, w["norm_odd"], w["w_odd"], w["w_r"], w["w_gate"], w["b_gate"], tm=ms)
    oc_s, sc_s = _gla_sample(False, [proj1, proj1, proj1, proj1, lf1], [0, 1, 1, 2, 0], None, w["gla_gain"],
                             ctx["state_gla"], t_new=t_new, **gl)
    y_s = _out_proj([oc_s], w["w_out_odd"], x1_s, w["final_norm"], tm=ms)

    heads4 = lambda a, n, s: a.reshape(n, s, B_HEADS, B_HEAD_DIM)
    logf = lambda a, n, s: a[:, :B_HEADS].reshape(n, s, B_HEADS)
    prompt = (y_p.reshape(bsz, seq, D_MODEL), heads4(krows_p, bsz, seq), heads4(vrows_p, bsz, seq),
              logf(lf_p, bsz, seq), sa_p, sc_p)
    sample = (y_s.reshape(db, t_new, D_MODEL), heads4(krows_s, db, t_new), heads4(vrows_s, db, t_new),
              logf(lf_s, db, t_new), sa_s, sc_s)
    return prompt, sample


def kernel(x_prompt, x_sample, cache_fox_k, cache_fox_v, cache_fox_logf, state_hgrn, state_gla, page_table,
           norm_even, w_in_even, b_fox_f, lb_logits, hgrn_gain, w_out_even, norm_odd, w_in_odd, w_gla_gate,
           b_gla_gate, gla_gain, w_out_odd, final_norm):
    weights = _prep_weights((norm_even, w_in_even, b_fox_f, lb_logits, hgrn_gain, w_out_even, norm_odd, w_in_odd,
                             w_gla_gate, b_gla_gate, gla_gain, w_out_odd, final_norm))
    bsz, seq, _ = x_prompt.shape
    n_pool = cache_fox_k.shape[1]
    pw = PAGE_SIZE * B_HEADS
    ctx = dict(
        page_table=page_table,
        k_pages=cache_fox_k[0].reshape(n_pool, pw, B_HEAD_DIM),
        v_pages=cache_fox_v[0].reshape(n_pool, pw, B_HEAD_DIM),
        lf_pages=cache_fox_logf[0].transpose(0, 2, 1).reshape(n_pool, pw),
        state_hgrn=state_hgrn[0], state_gla=state_gla[0])
    (y_p, kp, vp, lfp, hgrn_p, gla_p), (y_s, ks, vs, lfs, hgrn_s, gla_s) = _forward(x_prompt, x_sample, ctx, weights)
    n_pp = seq // PAGE_SIZE
    fox_k_prompt = kp.reshape(1, bsz, n_pp, PAGE_SIZE, B_HEADS, B_HEAD_DIM)
    fox_v_prompt = vp.reshape(1, bsz, n_pp, PAGE_SIZE, B_HEADS, B_HEAD_DIM)
    fox_logf_prompt = lfp.reshape(1, bsz, n_pp, PAGE_SIZE, B_HEADS)
    return (y_p, y_s, fox_k_prompt, fox_v_prompt, fox_logf_prompt, hgrn_p[None], gla_p[None],
            ks[None], vs[None], lfs[None], hgrn_s[None], gla_s[None])
```

```python
import functools

import numpy as np
import jax
import jax.numpy as jnp
from jax import lax
from jax.experimental import pallas as pl
from jax.experimental.pallas import tpu as pltpu

F32 = jnp.float32
BF16 = jnp.bfloat16

D_MODEL = 1024
PAGE_SIZE = 128
A_HEADS = 4
A_HEAD_DIM = 128
A_WIDTH = 512
B_HEADS = 4
B_HEAD_DIM = 128
B_WIDTH = 512
C_HEADS = 4
C_KEY_WIDTH = 512
C_VAL_WIDTH = 1024
C_KEY_DIM = 128
C_VAL_DIM = 256
C_GATE_RANK = 16
GLA_GATE_NORMALIZER = 16.0
EPS = 1e-6
HGRN_LAYER = 0
LOG2E = 1.4426950408889634
AUG_PIECES = 3
N_EVEN_MAIN = 4 * A_WIDTH + 4 * B_WIDTH
N_ODD_MAIN = 2 * C_KEY_WIDTH + 2 * C_VAL_WIDTH

LANES = 128
SUBLANES = 8
VMEM_LIMIT = 56 * 1024 * 1024
CHUNK = 64
MAX_CHUNK_LOG_DECAY = 60.0

_NT = (((1,), (1,)), ((), ()))
_TN = (((0,), (0,)), ((), ()))


def _cparams(sem):
    return pltpu.CompilerParams(dimension_semantics=sem, vmem_limit_bytes=VMEM_LIMIT)


def _sigmoid(x):
    return 1.0 / (1.0 + jnp.exp(-x))


def _log_sigmoid(x):
    return jnp.minimum(x, 0.0) - jnp.log1p(jnp.exp(-jnp.abs(x)))


def _silu(x):
    return x * _sigmoid(x)


def _rmsnorm_rows(x, g):
    return x * lax.rsqrt(jnp.mean(x * x, axis=-1, keepdims=True) + EPS) * g


def _split3(x):
    p1 = x.astype(BF16)
    r1 = x - p1.astype(F32)
    p2 = r1.astype(BF16)
    p3 = (r1 - p2.astype(F32)).astype(BF16)
    return p1, p2, p3


def _dot01(m01, x):
    acc = None
    for p in _split3(x):
        t = jnp.dot(m01, p, preferred_element_type=F32)
        acc = t if acc is None else acc + t
    return acc


def _x_dot01(x, m01):
    acc = None
    for p in _split3(x):
        t = jnp.dot(p, m01, preferred_element_type=F32)
        acc = t if acc is None else acc + t
    return acc


def _lower_tri(n, seq):
    r = lax.broadcasted_iota(jnp.int32, (n, n), 0)
    c = lax.broadcasted_iota(jnp.int32, (n, n), 1)
    keep = r >= c
    if seq < n:
        keep = jnp.logical_and(keep, (r // seq) == (c // seq))
    return jnp.where(keep, 1.0, 0.0).astype(BF16)


def _dot_wt(h, wt_ref, c0, n):
    return lax.dot_general(h, wt_ref[c0:c0 + n, :], _NT, preferred_element_type=F32)


def _proj_cols(h, wt_ref, out_ref):
    step = 512
    for c0 in range(0, wt_ref.shape[0], step):
        out_ref[:, c0:c0 + step] = _dot_wt(h, wt_ref, c0, step)


def _aug_selectors():
    sel_q = np.zeros((LANES, B_WIDTH), np.float32)
    sel_k = np.zeros((LANES, B_WIDTH), np.float32)
    for p in range(AUG_PIECES):
        for hd in range(B_HEADS):
            sel_q[p * B_HEADS + hd, hd * B_HEAD_DIM + p] = 1.0
            sel_k[p * B_HEADS + hd, hd * B_HEAD_DIM + AUG_PIECES + p] = -1.0
    return jnp.asarray(sel_q, BF16), jnp.asarray(sel_k, BF16)


def _even_proj_kernel(x_ref, g_ref, w_ref, wfb_ref, bfox_ref, selq_ref, selk_ref, pa_ref, gate_ref, qa_ref, ka_ref,
                      vb_ref, kout_ref, vout_ref, lf_ref, c_ref, carry_ref, *, tm, seq):
    i = pl.program_id(0)
    na = 4 * A_WIDTH
    bw = B_WIDTH
    dh = B_HEAD_DIM
    if seq > tm:
        @pl.when((i * tm) % seq == 0)
        def _():
            carry_ref[...] = jnp.zeros_like(carry_ref)

    h = _rmsnorm_rows(x_ref[...], g_ref[...]).astype(BF16)
    fb = _dot_wt(h, wfb_ref, 0, LANES) + bfox_ref[...]
    lane = lax.broadcasted_iota(jnp.int32, fb.shape, 1)
    lf = jnp.where(lane < B_HEADS, _log_sigmoid(fb), 0.0)
    lf_ref[...] = lf
    for c0 in range(0, na // 2, 512):
        pa_ref[:, c0:c0 + 512] = _dot_wt(h, w_ref, c0, 512)
    cs = _dot01(_lower_tri(tm, seq), lf)
    if seq > tm:
        cs = cs + carry_ref[0:1, :]
        carry_ref[0:1, :] = cs[tm - 1:tm, :]
    c_ref[...] = cs
    for c0 in range(na // 2, na, 512):
        pa_ref[:, c0:c0 + 512] = _dot_wt(h, w_ref, c0, 512)
    pos = lax.broadcasted_iota(jnp.int32, (1, bw), 1) % dh
    aug_q = jnp.where(jnp.logical_and(pos >= AUG_PIECES, pos < 2 * AUG_PIECES), 1.0, 0.0)
    aug_k = jnp.where(pos < AUG_PIECES, 1.0, 0.0)
    packed = None
    for p, piece in enumerate(_split3(cs * LOG2E)):
        shifted = piece.astype(F32) if p == 0 else pltpu.roll(piece.astype(F32), p * B_HEADS, 1)
        packed = shifted if packed is None else packed + shifted
    packed = packed.astype(BF16)
    aug_q = (aug_q + jnp.dot(packed, selq_ref[...], preferred_element_type=F32)).astype(BF16)
    aug_k = (aug_k + jnp.dot(packed, selk_ref[...], preferred_element_type=F32)).astype(BF16)
    q = (_dot_wt(h, w_ref, na, bw) * (dh ** -0.5 * LOG2E)).astype(BF16)
    for hd in range(B_HEADS):
        qa_ref[:, 2 * hd * dh:(2 * hd + 1) * dh] = q[:, hd * dh:(hd + 1) * dh]
    for j, out_ref in ((1, kout_ref), (2, vout_ref)):
        kv = _dot_wt(h, w_ref, na + j * bw, bw)
        kv16 = kv.astype(BF16)
        if j == 1:
            for hd in range(B_HEADS):
                ka_ref[:, 2 * hd * dh:(2 * hd + 1) * dh] = kv16[:, hd * dh:(hd + 1) * dh]
        else:
            vb_ref[...] = kv16
        for hd in range(B_HEADS):
            out_ref[pl.ds(hd, tm, stride=B_HEADS), :] = kv[:, hd * dh:(hd + 1) * dh]
    gate_ref[...] = _dot_wt(h, w_ref, na + 3 * bw, bw)
    for hd in range(B_HEADS):
        qa_ref[:, (2 * hd + 1) * dh:(2 * hd + 2) * dh] = aug_q[:, hd * dh:(hd + 1) * dh]
        ka_ref[:, (2 * hd + 1) * dh:(2 * hd + 2) * dh] = aug_k[:, hd * dh:(hd + 1) * dh]


def _even_proj(x, g, w_main, w_fb, b_fox, *, seq, tm):
    m = x.shape[0]
    const = lambda i: (0, 0)
    row = lambda i: (i, 0)
    sel_q, sel_k = _aug_selectors()
    return pl.pallas_call(
        functools.partial(_even_proj_kernel, tm=tm, seq=seq),
        grid=(m // tm,),
        in_specs=[pl.BlockSpec((tm, D_MODEL), row), pl.BlockSpec((1, D_MODEL), const),
                  pl.BlockSpec((N_EVEN_MAIN, D_MODEL), const), pl.BlockSpec(w_fb.shape, const),
                  pl.BlockSpec((1, LANES), const), pl.BlockSpec(sel_q.shape, const),
                  pl.BlockSpec(sel_k.shape, const)],
        out_specs=[pl.BlockSpec((tm, 4 * A_WIDTH), row), pl.BlockSpec((tm, B_WIDTH), row),
                   pl.BlockSpec((tm, 2 * B_WIDTH), row), pl.BlockSpec((tm, 2 * B_WIDTH), row),
                   pl.BlockSpec((tm, B_WIDTH), row), pl.BlockSpec((tm * B_HEADS, B_HEAD_DIM), row),
                   pl.BlockSpec((tm * B_HEADS, B_HEAD_DIM), row), pl.BlockSpec((tm, LANES), row),
                   pl.BlockSpec((tm, LANES), row)],
        out_shape=[jax.ShapeDtypeStruct((m, 4 * A_WIDTH), F32), jax.ShapeDtypeStruct((m, B_WIDTH), F32),
                   jax.ShapeDtypeStruct((m, 2 * B_WIDTH), BF16), jax.ShapeDtypeStruct((m, 2 * B_WIDTH), BF16),
                   jax.ShapeDtypeStruct((m, B_WIDTH), BF16),
                   jax.ShapeDtypeStruct((m * B_HEADS, B_HEAD_DIM), F32),
                   jax.ShapeDtypeStruct((m * B_HEADS, B_HEAD_DIM), F32),
                   jax.ShapeDtypeStruct((m, LANES), F32), jax.ShapeDtypeStruct((m, LANES), F32)],
        scratch_shapes=[pltpu.VMEM((SUBLANES, LANES), F32)],
        compiler_params=_cparams(("arbitrary",)),
        name="even_proj",
    )(x, g, w_main, w_fb, b_fox, sel_q, sel_k)


def _odd_proj_kernel(x_ref, g_ref, w_ref, wr_ref, wg_ref, bg_ref, proj_ref, lf_ref):
    h = _rmsnorm_rows(x_ref[...], g_ref[...]).astype(BF16)
    _proj_cols(h, w_ref, proj_ref)
    r = _dot_wt(h, wr_ref, 0, LANES)
    z = jnp.dot(r.astype(BF16), wg_ref[...], preferred_element_type=F32) + bg_ref[...]
    lf_ref[...] = _log_sigmoid(z) / GLA_GATE_NORMALIZER


def _odd_proj(x, g, w_main, w_r, w_gate, b_gate, *, tm):
    m = x.shape[0]
    n = N_ODD_MAIN
    const = lambda i: (0, 0)
    row = lambda i: (i, 0)
    return pl.pallas_call(
        _odd_proj_kernel,
        grid=(m // tm,),
        in_specs=[pl.BlockSpec((tm, D_MODEL), row), pl.BlockSpec((1, D_MODEL), const),
                  pl.BlockSpec((n, D_MODEL), const), pl.BlockSpec(w_r.shape, const),
                  pl.BlockSpec((LANES, C_KEY_WIDTH), const), pl.BlockSpec((1, C_KEY_WIDTH), const)],
        out_specs=[pl.BlockSpec((tm, n), row), pl.BlockSpec((tm, C_KEY_WIDTH), row)],
        out_shape=[jax.ShapeDtypeStruct((m, n), F32), jax.ShapeDtypeStruct((m, C_KEY_WIDTH), F32)],
        compiler_params=_cparams(("arbitrary",)),
        name="odd_proj",
    )(x, g, w_main, w_r, w_gate, b_gate)


def _lower_bound(logits, layer):
    e = jnp.exp(logits - jnp.max(logits, axis=0, keepdims=True))
    return jnp.sum(e[:layer + 1, :], axis=0, keepdims=True) / jnp.sum(e, axis=0, keepdims=True)


def _hgrn_gate(fa, lb):
    f = lb + (1.0 - lb) * _sigmoid(fa)
    return jnp.log(f), 1.0 - f


def _exact_group(q, k, v, g, st, lo, hi):
    n = SUBLANES
    row = lax.broadcasted_iota(jnp.int32, (n, 1), 0)
    valid = jnp.logical_and(row >= lo, row < hi)
    q = jnp.where(valid, q, 0.0)
    k = jnp.where(valid, k, 0.0)
    g = jnp.where(valid, g, 0.0)
    b = g
    for sh in (1, 2, 4):
        b = b + jnp.where(row >= sh, pltpu.roll(b, sh, 0), 0.0)
    o = lax.dot_general((q * jnp.exp(b)).astype(BF16), st.astype(BF16), _NT, preferred_element_type=F32)
    for s in range(lo, hi):
        w = jnp.exp(jnp.minimum(b - b[s:s + 1, :], 0.0))
        a = jnp.sum(q * k[s:s + 1, :] * w, axis=-1, keepdims=True)
        o = o + jnp.where(row >= s, a, 0.0) * v[s:s + 1, :]
    b_last = b[n - 1:n, :]
    k_hat = k * jnp.exp(b_last - b)
    st_new = st * jnp.exp(b_last) + lax.dot_general(v.astype(BF16), k_hat.astype(BF16), _TN,
                                                    preferred_element_type=F32)
    return o, st_new


def _head_norm_gate(o, gain, gate):
    y = o * lax.rsqrt(jnp.mean(o * o, axis=-1, keepdims=True) + EPS) * gain
    return y * _silu(gate)


def _recurrence_block(q_ref, k_ref, v_ref, gate_ref, g_ref, gain_ref, o_ref, st_ref, b_ref, *, fa_ref, lb, heads, dk,
                      dv, tb, q_scale):
    n_chunks = tb // CHUNK

    tri = _lower_tri(CHUNK, CHUNK)
    b_min = None
    for c in range(n_chunks):
        rows = slice(c * CHUNK, (c + 1) * CHUNK)
        if fa_ref is not None:
            g, k = _hgrn_gate(fa_ref[rows, :], lb)
            g_ref[rows, :] = g
            k_ref[rows, :] = k
        else:
            g = g_ref[rows, :]
        b = _dot01(tri, g)
        b_ref[rows, :] = b
        b_last = b[CHUNK - 1:CHUNK, :]
        b_min = b_last if b_min is None else jnp.minimum(b_min, b_last)
    chunk_form_ok = jnp.min(b_min) >= -MAX_CHUNK_LOG_DECAY

    def finish(rows, h, o_h):
        vsl = slice(h * dv, (h + 1) * dv)
        o_ref[rows, vsl] = _head_norm_gate(o_h, gain_ref[:, vsl], gate_ref[rows, vsl]).astype(o_ref.dtype)

    @pl.when(chunk_form_ok)
    def _():
        r_i = lax.broadcasted_iota(jnp.int32, (CHUNK, CHUNK), 0)
        c_i = lax.broadcasted_iota(jnp.int32, (CHUNK, CHUNK), 1)
        causal = r_i >= c_i

        for c in range(n_chunks):
            rows = slice(c * CHUNK, (c + 1) * CHUNK)
            b = b_ref[rows, :]
            e_b = jnp.exp(b)
            b_last = b[CHUNK - 1:CHUNK, :]
            e_last = jnp.exp(b_last)
            q_t = q_ref[rows, :] * q_scale * e_b
            k_t = k_ref[rows, :] * jnp.exp(-b)
            k_hat = k_t * e_last
            ksl = [slice(h * dk, (h + 1) * dk) for h in range(heads)]
            qh = [q_t[:, s].astype(BF16) for s in ksl]
            vh = [v_ref[rows, h * dv:(h + 1) * dv].astype(BF16) for h in range(heads)]
            st = [st_ref[h] for h in range(heads)]
            a = [lax.dot_general(qh[h], k_t[:, ksl[h]].astype(BF16), _NT, preferred_element_type=F32)
                 for h in range(heads)]
            inter = [lax.dot_general(qh[h], st[h].astype(BF16), _NT, preferred_element_type=F32)
                     for h in range(heads)]
            outer = [lax.dot_general(vh[h], k_hat[:, ksl[h]].astype(BF16), _TN, preferred_element_type=F32)
                     for h in range(heads)]
            for h in range(heads):
                st_ref[h] = st[h] * e_last[:, ksl[h]] + outer[h]
            for h in range(heads):
                a_h = jnp.where(causal, a[h], 0.0).astype(BF16)
                finish(rows, h, jnp.dot(a_h, vh[h], preferred_element_type=F32) + inter[h])

    @pl.when(jnp.logical_not(chunk_form_ok))
    def _():
        pair = 2 * SUBLANES

        def group(i, carry):
            for h in range(heads):
                ksl = slice(h * dk, (h + 1) * dk)
                vsl = slice(h * dv, (h + 1) * dv)
                outs = []
                for half in range(2):
                    rows = pl.ds(pl.multiple_of(i * pair + half * SUBLANES, SUBLANES), SUBLANES)
                    o_h, st_new = _exact_group(q_ref[rows, ksl] * q_scale, k_ref[rows, ksl], v_ref[rows, vsl],
                                               g_ref[rows, ksl], st_ref[h], 0, SUBLANES)
                    st_ref[h] = st_new
                    outs.append(o_h)
                finish(pl.ds(pl.multiple_of(i * pair, pair), pair), h, jnp.concatenate(outs, axis=0))
            return carry

        lax.fori_loop(0, tb // pair, group, 0)


def _state_step_edges(st_ref, s_ref):
    t = pl.program_id(1)

    def first():
        @pl.when(t == 0)
        def _():
            st_ref[...] = jnp.zeros_like(st_ref)

    def last():
        @pl.when(t == pl.num_programs(1) - 1)
        def _():
            for h in range(st_ref.shape[0]):
                s_ref[0, h] = st_ref[h].T

    return first, last


def _hgrn_prompt_kernel(q_ref, fa_ref, v_ref, gate_ref, lb_ref, gain_ref, o_ref, s_ref, st_ref, b_ref, g_ref, k_ref,
                        **kw):
    first, last = _state_step_edges(st_ref, s_ref)
    first()
    _recurrence_block(q_ref, k_ref, v_ref, gate_ref, g_ref, gain_ref, o_ref, st_ref, b_ref, fa_ref=fa_ref,
                      lb=_lower_bound(lb_ref[...], HGRN_LAYER), **kw)
    last()


def _hgrn_prompt(pa, lb_logits, gain, *, bsz, seq, heads, dk, dv, tb, q_scale):
    nt = seq // tb
    kw, vw = heads * dk, heads * dv
    col = lambda cb: (lambda b, t: (b * nt + t, cb))
    const = lambda b, t: (0, 0)
    return pl.pallas_call(
        functools.partial(_hgrn_prompt_kernel, heads=heads, dk=dk, dv=dv, tb=tb, q_scale=q_scale),
        grid=(bsz, nt),
        in_specs=[pl.BlockSpec((tb, kw), col(0)), pl.BlockSpec((tb, kw), col(1)), pl.BlockSpec((tb, vw), col(2)),
                  pl.BlockSpec((tb, vw), col(3)), pl.BlockSpec(lb_logits.shape, const),
                  pl.BlockSpec((1, vw), const)],
        out_specs=[pl.BlockSpec((tb, vw), col(0)), pl.BlockSpec((1, heads, dk, dv), lambda b, t: (b, 0, 0, 0))],
        out_shape=[jax.ShapeDtypeStruct((bsz * seq, vw), BF16), jax.ShapeDtypeStruct((bsz, heads, dk, dv), F32)],
        scratch_shapes=[pltpu.VMEM((heads, dv, dk), F32), pltpu.VMEM((tb, kw), F32), pltpu.VMEM((tb, kw), F32),
                        pltpu.VMEM((tb, kw), F32)],
        compiler_params=_cparams(("arbitrary", "arbitrary")),
        name="hgrn_prompt",
    )(pa, pa, pa, pa, lb_logits, gain)


def _layer1_prompt_kernel(x_ref, oa_ref, ob_ref, woe_ref, gn_ref, w_ref, wr_ref, wg_ref, bg_ref, gain_ref, woo_ref,
                          fn_ref, y_ref, s_ref, st_ref, b_ref, proj_ref, lf_ref, o_ref, *, heads, dk, dv, tb,
                          q_scale):
    first, last = _state_step_edges(st_ref, s_ref)
    first()
    kw, vw = heads * dk, heads * dv
    aw = oa_ref.shape[1]
    x1 = (x_ref[...] + jnp.dot(oa_ref[...], woe_ref[0:aw, :], preferred_element_type=F32)
          + jnp.dot(ob_ref[...], woe_ref[aw:, :], preferred_element_type=F32))
    h = _rmsnorm_rows(x1, gn_ref[...]).astype(BF16)
    r = _dot_wt(h, wr_ref, 0, LANES)
    z = jnp.dot(r.astype(BF16), wg_ref[...], preferred_element_type=F32) + bg_ref[...]
    lf_ref[...] = _log_sigmoid(z) / GLA_GATE_NORMALIZER
    _proj_cols(h, w_ref, proj_ref)
    _recurrence_block(proj_ref.at[:, 0:kw], proj_ref.at[:, kw:2 * kw], proj_ref.at[:, 2 * kw:2 * kw + vw],
                      proj_ref.at[:, 2 * kw + vw:2 * kw + 2 * vw], lf_ref, gain_ref, o_ref, st_ref, b_ref,
                      fa_ref=None, lb=None, heads=heads, dk=dk, dv=dv, tb=tb, q_scale=q_scale)
    y = x1 + jnp.dot(o_ref[...], woo_ref[...], preferred_element_type=F32)
    y_ref[...] = _rmsnorm_rows(y, fn_ref[...])
    last()


def _layer1_prompt(x, o_a, o_b, w, *, bsz, seq, heads, dk, dv, tb, q_scale):
    nt = seq // tb
    kw, vw = heads * dk, heads * dv
    row = lambda b, t: (b * nt + t, 0)
    const = lambda b, t: (0, 0)
    full = lambda a: pl.BlockSpec((N_ODD_MAIN, D_MODEL) if a is w["w_odd"] else a.shape, const)
    weights = [w["w_out_even"], w["norm_odd"], w["w_odd"], w["w_r"], w["w_gate"], w["b_gate"], w["gla_gain"],
               w["w_out_odd"], w["final_norm"]]
    return pl.pallas_call(
        functools.partial(_layer1_prompt_kernel, heads=heads, dk=dk, dv=dv, tb=tb, q_scale=q_scale),
        grid=(bsz, nt),
        in_specs=[pl.BlockSpec((tb, D_MODEL), row), pl.BlockSpec((tb, o_a.shape[1]), row),
                  pl.BlockSpec((tb, o_b.shape[1]), row)] + [full(a) for a in weights],
        out_specs=[pl.BlockSpec((tb, D_MODEL), row), pl.BlockSpec((1, heads, dk, dv), lambda b, t: (b, 0, 0, 0))],
        out_shape=[jax.ShapeDtypeStruct((bsz * seq, D_MODEL), F32),
                   jax.ShapeDtypeStruct((bsz, heads, dk, dv), F32)],
        scratch_shapes=[pltpu.VMEM((heads, dv, dk), F32), pltpu.VMEM((tb, kw), F32),
                        pltpu.VMEM((tb, 2 * kw + 2 * vw), F32), pltpu.VMEM((tb, kw), F32),
                        pltpu.VMEM((tb, vw), BF16)],
        compiler_params=_cparams(("arbitrary", "arbitrary")),
        name="layer1_prompt",
    )(x, o_a, o_b, *weights)


SAMPLE_GROUPS = 1


def _gla_sample_kernel(*refs, hgrn, heads, dk, dv, t_new, q_scale, groups):
    if hgrn:
        q_ref, fa_ref, v_ref, gate_ref, lb_ref, gain_ref, s0_ref, o_ref, s_ref = refs
    else:
        q_ref, k_ref, v_ref, gate_ref, g_ref, gain_ref, s0_ref, o_ref, s_ref = refs
    per = SUBLANES // t_new
    for grp in range(groups):
        rows = slice(grp * SUBLANES, (grp + 1) * SUBLANES)
        for h in range(heads):
            ksl = slice(h * dk, (h + 1) * dk)
            vsl = slice(h * dv, (h + 1) * dv)
            if hgrn:
                g, k = _hgrn_gate(fa_ref[rows, ksl], _lower_bound(lb_ref[...], HGRN_LAYER)[:, ksl])
            else:
                g, k = g_ref[rows, ksl], k_ref[rows, ksl]
            q = q_ref[rows, ksl] * q_scale
            v = v_ref[rows, vsl]
            o_h = None
            for e in range(per):
                o_e, st_new = _exact_group(q, k, v, g, s0_ref[grp * per + e, h].T, e * t_new, (e + 1) * t_new)
                s_ref[grp * per + e, h] = st_new.T
                o_h = o_e if o_h is None else o_h + o_e
            o_ref[rows, vsl] = _head_norm_gate(o_h, gain_ref[:, vsl], gate_ref[rows, vsl])


def _gla_sample(hgrn, arrays, col_blocks, small, gain, s0, *, t_new, heads, dk, dv, q_scale):
    m = arrays[0].shape[0]
    groups = SAMPLE_GROUPS if m % (SAMPLE_GROUPS * SUBLANES) == 0 else 1
    tr = groups * SUBLANES
    per = tr // t_new
    kw, vw = heads * dk, heads * dv
    widths = [kw, kw, vw, vw] + ([] if hgrn else [kw])
    in_specs = [pl.BlockSpec((tr, w), functools.partial(lambda i, cb: (i, cb), cb=cb))
                for w, cb in zip(widths, col_blocks)]
    operands = list(arrays)
    if hgrn:
        in_specs.append(pl.BlockSpec(small.shape, lambda i: (0, 0)))
        operands.append(small)
    in_specs.append(pl.BlockSpec((1, vw), lambda i: (0, 0)))
    operands.append(gain)
    in_specs.append(pl.BlockSpec((per, heads, dk, dv), lambda i: (i, 0, 0, 0)))
    operands.append(s0)
    return pl.pallas_call(
        functools.partial(_gla_sample_kernel, hgrn=hgrn, heads=heads, dk=dk, dv=dv, t_new=t_new, q_scale=q_scale,
                          groups=groups),
        grid=(m // tr,),
        in_specs=in_specs,
        out_specs=[pl.BlockSpec((tr, vw), lambda i: (i, 0)),
                   pl.BlockSpec((per, heads, dk, dv), lambda i: (i, 0, 0, 0))],
        out_shape=[jax.ShapeDtypeStruct((m, vw), F32), jax.ShapeDtypeStruct(s0.shape, F32)],
        compiler_params=_cparams(("arbitrary",)),
        name="hgrn_sample" if hgrn else "gla_sample",
    )(*operands)


FOX_STRIP = 64


def _fox_prompt_step(qi, ki, before_head, q_ref, k_ref, v_ref, gate_ref, o_ref, m_ref, l_ref, acc_ref, p_ref, a_ref,
                     *, tq, heads, dh):
    aw = 2 * dh

    @pl.when(ki == 0)
    def _():
        m_ref[...] = jnp.full_like(m_ref, -jnp.inf)
        l_ref[...] = jnp.zeros_like(l_ref)
        acc_ref[...] = jnp.zeros_like(acc_ref)

    def step(diag):
        if diag:
            r_i = lax.broadcasted_iota(jnp.int32, (FOX_STRIP, LANES), 0)
            c_i = lax.broadcasted_iota(jnp.int32, (FOX_STRIP, LANES), 1)
        half = tq // 2
        segments = [(0, half, half), (half, tq, tq)] if diag and half % LANES == 0 else [(0, tq, tq)]
        for h in range(heads):
            interleaved_update = before_head(h)
            sl = slice(h * dh, (h + 1) * dh)
            hw = slice(h * aw, (h + 1) * aw)
            for lo, hi, ext in segments:
                s = lax.dot_general(q_ref[lo:hi, hw], k_ref[0:ext, hw], _NT, preferred_element_type=F32)
                if lo == 0:
                    after_head = interleaved_update()
                for r0 in range(lo, hi, FOX_STRIP):
                    rows = slice(r0, r0 + FOX_STRIP)
                    srows = slice(r0 - lo, r0 - lo + FOX_STRIP)
                    live = [j for j in range(ext // LANES) if not (diag and j * LANES > r0 + FOX_STRIP - 1)]
                    blocks = []
                    for j in live:
                        lg = s[srows, j * LANES:(j + 1) * LANES]
                        if diag and (j + 1) * LANES - 1 > r0:
                            lg = jnp.where(r_i + r0 >= c_i + j * LANES, lg, -jnp.inf)
                        blocks.append(lg)
                    m_cur = blocks[0]
                    for lg in blocks[1:]:
                        m_cur = jnp.maximum(m_cur, lg)
                    m_prev = m_ref[h, rows]
                    m_new = jnp.maximum(m_prev, jnp.max(m_cur, axis=-1, keepdims=True))
                    alpha = jnp.exp2(m_prev - m_new)
                    probs = [jnp.exp2(lg - m_new) for lg in blocks]
                    row_sum = probs[0]
                    for pj in probs[1:]:
                        row_sum = row_sum + pj
                    l_ref[h, rows] = alpha * l_ref[h, rows] + jnp.sum(row_sum, axis=-1, keepdims=True)
                    m_ref[h, rows] = m_new
                    a_ref[rows] = alpha
                    for j, pj in zip(live, probs):
                        p_ref[rows, j * LANES:(j + 1) * LANES] = pj.astype(BF16)
                    for j in range(ext // LANES):
                        if j not in live:
                            p_ref[rows, j * LANES:(j + 1) * LANES] = jnp.zeros((FOX_STRIP, LANES), BF16)
                acc_ref[lo:hi, sl] = a_ref[lo:hi] * acc_ref[lo:hi, sl] + jnp.dot(
                    p_ref[lo:hi, 0:ext], v_ref[0:ext, sl], preferred_element_type=F32)
            after_head()

    @pl.when(ki < qi)
    def _():
        step(False)

    @pl.when(ki == qi)
    def _():
        step(True)
        for h in range(heads):
            sl = slice(h * dh, (h + 1) * dh)
            o_ref[:, sl] = (acc_ref[:, sl] / l_ref[h] * _silu(gate_ref[:, sl])).astype(o_ref.dtype)


N_PROMPT_REFS = 4
N_SAMPLE_REFS = 9


def _fox_kernel(qi_ref, ki_ref, pt_ref, *refs, tq, heads, dh, sample_kw):
    prompt_in = refs[:N_PROMPT_REFS]
    sample_in = refs[N_PROMPT_REFS:N_PROMPT_REFS + N_SAMPLE_REFS]
    o_ref, os_ref = refs[N_PROMPT_REFS + N_SAMPLE_REFS:N_PROMPT_REFS + N_SAMPLE_REFS + 2]
    scratch = refs[N_PROMPT_REFS + N_SAMPLE_REFS + 2:]
    prompt_scratch, sample_scratch = scratch[:5], scratch[5:]
    p = pl.program_id(1)
    step = pl.program_id(0) * pl.num_programs(1) + p

    def before_head(h):
        return _fox_sample_unit(step * heads + h, pt_ref, *sample_in, os_ref, *sample_scratch, heads=heads, dh=dh,
                                **sample_kw)

    _fox_prompt_step(qi_ref[p], ki_ref[p], before_head, *prompt_in, o_ref, *prompt_scratch, tq=tq, heads=heads, dh=dh)


def _fox_attention(qa, ka, vb, gate, page_table, k_pages, v_pages, lf_pages, q_s, k_new, v_new, gate_s, cn_col,
                   cn_row, *, bsz, seq, tq, pages_per_unit):
    nq = seq // tq
    pairs = [(qi, ki) for qi in range(nq) for ki in range(qi + 1)]
    qi_tab = jnp.asarray(np.array([p[0] for p in pairs], np.int32))
    ki_tab = jnp.asarray(np.array([p[1] for p in pairs], np.int32))
    w = B_WIDTH
    dh = B_HEAD_DIM
    n_pool = lf_pages.shape[0]
    sfx = _page_suffix(lf_pages, B_HEADS)
    db, n_pages = page_table.shape
    pp = pages_per_unit
    n_units = db * (n_pages // pp)
    assert bsz * len(pairs) * B_HEADS >= n_units, "not enough prompt steps to carry the sample page units"
    nrow = q_s.shape[1]
    pw = PAGE_SIZE * B_HEADS
    q_map = lambda b, p, qt, kt, pt: (b * nq + qt[p], 0)
    k_map = lambda b, p, qt, kt, pt: (b * nq + kt[p], 0)
    whole = lambda a: pl.BlockSpec(a.shape, lambda b, p, qt, kt, pt: (0,) * a.ndim)
    hbm = pl.BlockSpec(memory_space=pl.ANY)
    n_slots = 2
    grid_spec = pltpu.PrefetchScalarGridSpec(
        num_scalar_prefetch=3,
        grid=(bsz, len(pairs)),
        in_specs=[pl.BlockSpec((tq, 2 * w), q_map), pl.BlockSpec((tq, 2 * w), k_map),
                  pl.BlockSpec((tq, w), k_map), pl.BlockSpec((tq, w), q_map),
                  hbm, hbm, hbm, whole(q_s), whole(k_new), whole(v_new), whole(gate_s), whole(cn_col),
                  whole(cn_row)],
        out_specs=[pl.BlockSpec((tq, w), q_map),
                   pl.BlockSpec((db, nrow, dh), lambda b, p, qt, kt, pt: (0, 0, 0))],
        scratch_shapes=[pltpu.VMEM((B_HEADS, tq, LANES), F32), pltpu.VMEM((B_HEADS, tq, LANES), F32),
                        pltpu.VMEM((tq, w), F32), pltpu.VMEM((tq, tq), BF16), pltpu.VMEM((tq, LANES), F32),
                        pltpu.VMEM((n_slots, pp, pw, dh), F32), pltpu.VMEM((n_slots, pp, pw, dh), F32),
                        pltpu.VMEM((n_slots, pp, 1, 2 * pw), F32), pltpu.SemaphoreType.DMA((3, n_slots)),
                        pltpu.VMEM((nrow, 1), F32), pltpu.VMEM((nrow, 1), F32), pltpu.VMEM((nrow, dh), F32),
                        pltpu.VMEM((nrow, PAGE_SIZE), F32)],
    )
    sample_kw = dict(pp=pp, t_new=nrow // B_HEADS, n_pages=n_pages, n_units=n_units)
    return pl.pallas_call(
        functools.partial(_fox_kernel, tq=tq, heads=B_HEADS, dh=dh, sample_kw=sample_kw),
        grid_spec=grid_spec,
        out_shape=[jax.ShapeDtypeStruct((bsz * seq, w), BF16), jax.ShapeDtypeStruct((db, nrow, dh), F32)],
        compiler_params=_cparams(("arbitrary", "arbitrary")),
        name="fox_attention",
    )(qi_tab, ki_tab, page_table, qa, ka, vb, gate, k_pages, v_pages, sfx, q_s, k_new, v_new, gate_s, cn_col,
      cn_row)


def _page_suffix_kernel(lf_ref, later_ref, total_ref, out_ref):
    lf = lf_ref[...]
    pw = lf.shape[1]
    out_ref[:, 0:pw] = _x_dot01(lf, later_ref[...])
    out_ref[:, pw:2 * pw] = _x_dot01(lf, total_ref[...])


def _page_suffix(lf_pages, heads):
    n_pool, pw = lf_pages.shape
    tm = 512 if n_pool % 512 == 0 else n_pool
    j = np.arange(pw)
    in_head, in_tok = j[:, None] // (pw // heads), j[:, None] % (pw // heads)
    out_head, out_tok = j[None, :] // (pw // heads), j[None, :] % (pw // heads)
    head_eq = in_head == out_head
    later = jnp.asarray(head_eq & (in_tok > out_tok), BF16)
    total = jnp.asarray(head_eq, BF16)
    row = lambda i: (i, 0)
    const = lambda i: (0, 0)
    return pl.pallas_call(
        _page_suffix_kernel,
        grid=(n_pool // tm,),
        in_specs=[pl.BlockSpec((tm, pw), row), pl.BlockSpec((pw, pw), const), pl.BlockSpec((pw, pw), const)],
        out_specs=pl.BlockSpec((tm, 2 * pw), row),
        out_shape=jax.ShapeDtypeStruct((n_pool, 2 * pw), F32),
        compiler_params=_cparams(("arbitrary",)),
        name="page_suffix",
    )(lf_pages, later, total)


def _fox_sample_unit(u, pt_ref, k_hbm, v_hbm, sfx_hbm, q_ref, kn_ref, vn_ref, gate_ref, cn_col_ref, cn_row_ref, o_ref,
                     kbuf, vbuf, sbuf, sem, m_ref, l_ref, acc_ref, carry_ref, *, pp, heads, dh, t_new, n_pages,
                     n_units):
    nj = n_pages // pp
    valid = u < n_units
    uc = jnp.minimum(u, n_units - 1)
    b = lax.div(uc, nj)
    j = lax.rem(uc, nj)
    nrow = t_new * heads
    pw = PAGE_SIZE * heads
    slot = lax.rem(uc, 2)

    def page_copies(uu, sl):
        bb = lax.div(uu, nj)
        jj = lax.rem(uu, nj)
        copies = []
        for i in range(pp):
            page = pt_ref[bb, n_pages - 1 - (jj * pp + i)]
            sources = (k_hbm.at[page], v_hbm.at[page], sfx_hbm.at[pl.ds(page, 1)])
            for kind, (src, dst) in enumerate(zip(sources, (kbuf, vbuf, sbuf))):
                copies.append(pltpu.make_async_copy(src, dst.at[sl, i], sem.at[kind, sl]))
        return copies

    @pl.when(u == 0)
    def _():
        for cp in page_copies(u, slot):
            cp.start()

    @pl.when(u + 1 < n_units)
    def _():
        for cp in page_copies(u + 1, 1 - slot):
            cp.start()

    @pl.when(valid)
    def _():
        for cp in page_copies(u, slot):
            cp.wait()

    k_refs = [kbuf.at[slot, i] for i in range(pp)]
    v_refs = [vbuf.at[slot, i] for i in range(pp)]
    sfx_refs = [sbuf.at[slot, i] for i in range(pp)]

    @pl.when(j == 0)
    def _():
        m_ref[...] = jnp.full_like(m_ref, -jnp.inf)
        l_ref[...] = jnp.zeros_like(l_ref)
        acc_ref[...] = jnp.zeros_like(acc_ref)
        carry_ref[...] = jnp.zeros_like(carry_ref)

    q = q_ref[b]
    cn_col = cn_col_ref[b] * LOG2E
    row_head = lax.broadcasted_iota(jnp.int32, (nrow, dh), 0) % heads
    q_wide = jnp.concatenate([jnp.where(row_head == h, q, jnp.zeros_like(q)) for h in range(heads)], axis=1)

    def page_wide(ref):
        return jnp.concatenate([ref[pl.ds(h, PAGE_SIZE, stride=heads), :] for h in range(heads)],
                               axis=1).astype(BF16)

    def per_row_head(row):
        by_head = jnp.concatenate([row[:, h * PAGE_SIZE:(h + 1) * PAGE_SIZE] for h in range(heads)], axis=0)
        return jnp.concatenate([by_head] * t_new, axis=0)

    def own_head(wide):
        out = None
        for h in range(heads):
            part = jnp.where(row_head == h, wide[:, h * dh:(h + 1) * dh], 0.0)
            out = part if out is None else out + part
        return out

    def online(logit_list, weighted_values):
        m_cur = logit_list[0]
        for lg in logit_list[1:]:
            m_cur = jnp.maximum(m_cur, lg)
        m_prev = m_ref[...]
        m_new = jnp.maximum(m_prev, jnp.max(m_cur, axis=-1, keepdims=True))
        alpha = jnp.exp2(m_prev - m_new)
        probs = [jnp.exp2(lg - m_new) for lg in logit_list]
        row_sum = probs[0]
        for pj in probs[1:]:
            row_sum = row_sum + pj
        l_ref[...] = alpha * l_ref[...] + jnp.sum(row_sum, axis=-1, keepdims=True)
        acc_ref[...] = alpha * acc_ref[...] + weighted_values([pj.astype(BF16) for pj in probs])
        m_ref[...] = m_new

    logit_list = []

    def scores():
        carry = carry_ref[...]
        for i in range(pp):
            suffix = (carry + per_row_head(sfx_refs[i][:, 0:pw])) * LOG2E
            carry = carry + per_row_head(sfx_refs[i][:, pw:2 * pw])
            s = lax.dot_general(q_wide, page_wide(k_refs[i]), _NT, preferred_element_type=F32)
            logit_list.append(s + cn_col + suffix)
        carry_ref[...] = carry
        return update

    def page_values(probs):
        wide = None
        for pj, v_ref in zip(probs, v_refs):
            t = jnp.dot(pj, page_wide(v_ref), preferred_element_type=F32)
            wide = t if wide is None else wide + t
        return own_head(wide)

    def update():
        online(logit_list, page_values)
        return finish

    def finish():
        @pl.when(jnp.logical_and(valid, j == nj - 1))
        def _():
            s = lax.dot_general(q, kn_ref[b].astype(BF16), _NT, preferred_element_type=F32)
            logits = s + cn_col - cn_row_ref[b] * LOG2E
            r = lax.broadcasted_iota(jnp.int32, (nrow, nrow), 0)
            c = lax.broadcasted_iota(jnp.int32, (nrow, nrow), 1)
            keep = jnp.logical_and((r % heads) == (c % heads), (r // heads) >= (c // heads))
            online([jnp.where(keep, logits, -jnp.inf)],
                   lambda probs: jnp.dot(probs[0], vn_ref[b].astype(BF16), preferred_element_type=F32))
            o_ref[b] = acc_ref[...] / l_ref[...] * _silu(gate_ref[b])

    return scores()


def _out_proj_kernel(*refs, n_in, final):
    ins = refs[:n_in]
    w_ref, x_ref = refs[n_in], refs[n_in + 1]
    y = x_ref[...]
    k0 = 0
    for a_ref in ins:
        kw = a_ref.shape[1]
        y = y + jnp.dot(a_ref[...].astype(BF16), w_ref[k0:k0 + kw, :], preferred_element_type=F32)
        k0 += kw
    if final:
        g_ref, o_ref = refs[n_in + 2], refs[n_in + 3]
        o_ref[...] = _rmsnorm_rows(y, g_ref[...])
    else:
        refs[n_in + 2][...] = y


def _out_proj(ins, w, x, final_gain, *, tm):
    m = x.shape[0]
    const = lambda i: (0, 0)
    row = lambda i: (i, 0)
    in_specs = [pl.BlockSpec((tm, a.shape[1]), row) for a in ins]
    in_specs += [pl.BlockSpec(w.shape, const), pl.BlockSpec((tm, D_MODEL), row)]
    operands = list(ins) + [w, x]
    if final_gain is not None:
        in_specs.append(pl.BlockSpec((1, D_MODEL), const))
        operands.append(final_gain)
    return pl.pallas_call(
        functools.partial(_out_proj_kernel, n_in=len(ins), final=final_gain is not None),
        grid=(m // tm,),
        in_specs=in_specs,
        out_specs=pl.BlockSpec((tm, D_MODEL), row),
        out_shape=jax.ShapeDtypeStruct((m, D_MODEL), F32),
        compiler_params=_cparams(("arbitrary",)),
        name="out_proj_final" if final_gain is not None else "out_proj",
    )(*operands)


def _pad_cols(w, n):
    return jnp.pad(w, ((0, 0), (0, n - w.shape[1])))


def _prep_weights(weights):
    (norm_even, w_in_even, b_fox_f, lb_logits, hgrn_gain, w_out_even, norm_odd, w_in_odd, w_gla_gate, b_gla_gate,
     gla_gain, w_out_odd, final_norm) = weights
    w_even_t = w_in_even[0].T.astype(BF16)
    w_odd_t = w_in_odd[0].T.astype(BF16)
    pad_rows = lambda a: jnp.pad(a, ((0, LANES - a.shape[0]), (0, 0)))
    return dict(
        norm_even=norm_even[0].reshape(1, D_MODEL),
        w_even=w_even_t,
        w_fb=pad_rows(w_even_t[N_EVEN_MAIN:]),
        b_fox=_pad_cols(b_fox_f[0].reshape(1, B_HEADS), LANES),
        lb_logits=lb_logits,
        hgrn_gain=hgrn_gain[0].reshape(1, A_WIDTH),
        w_out_even=w_out_even[0].astype(BF16),
        norm_odd=norm_odd[0].reshape(1, D_MODEL),
        w_odd=w_odd_t,
        w_r=pad_rows(w_odd_t[N_ODD_MAIN:]),
        w_gate=jnp.pad(w_gla_gate[0], ((0, LANES - C_GATE_RANK), (0, 0))).astype(BF16),
        b_gate=b_gla_gate[0].reshape(1, C_KEY_WIDTH),
        gla_gain=gla_gain[0].reshape(1, C_VAL_WIDTH),
        w_out_odd=w_out_odd[0].astype(BF16),
        final_norm=final_norm.reshape(1, D_MODEL),
    )


def _forward(x_prompt, x_sample, ctx, w):
    bsz, seq, _ = x_prompt.shape
    db, t_new, _ = x_sample.shape
    mp, ms = bsz * seq, db * t_new
    xp = x_prompt.reshape(mp, D_MODEL)
    xs = x_sample.reshape(ms, D_MODEL)
    hg = dict(heads=A_HEADS, dk=A_HEAD_DIM, dv=A_HEAD_DIM, q_scale=1.0)
    gl = dict(heads=C_HEADS, dk=C_KEY_DIM, dv=C_VAL_DIM, q_scale=C_KEY_DIM ** -0.5)

    even = lambda x2, s, tm: _even_proj(x2, w["norm_even"], w["w_even"], w["w_fb"], w["b_fox"], seq=s, tm=tm)
    pa_p, gate_p, qa_p, ka_p, vb_p, krows_p, vrows_p, lf_p, _ = even(xp, seq, 512)
    pa_s, gate_s, qa_s, _, _, krows_s, vrows_s, lf_s, c_s = even(xs, t_new, ms)

    oa_p, sa_p = _hgrn_prompt(pa_p, w["lb_logits"], w["hgrn_gain"], bsz=bsz, seq=seq, tb=512, **hg)
    oa_s, sa_s = _gla_sample(True, [pa_s] * 4, [0, 1, 2, 3], w["lb_logits"], w["hgrn_gain"], ctx["state_hgrn"],
                             t_new=t_new, **hg)
    nrow = t_new * B_HEADS
    rows = lambda a: a.reshape(db, nrow, B_HEAD_DIM)
    cn = c_s[:, :B_HEADS].reshape(db, nrow)
    q_rows = qa_s.reshape(db, nrow, 2 * B_HEAD_DIM)[:, :, :B_HEAD_DIM]
    ob_p, ob_s = _fox_attention(qa_p, ka_p, vb_p, gate_p, ctx["page_table"], ctx["k_pages"], ctx["v_pages"],
                                ctx["lf_pages"], q_rows, rows(krows_s), rows(vrows_s), rows(gate_s),
                                cn.reshape(db, nrow, 1), cn.reshape(db, 1, nrow), bsz=bsz, seq=seq,
                                tq=min(seq, 1024), pages_per_unit=8)

    y_p, sc_p = _layer1_prompt(xp, oa_p, ob_p, w, bsz=bsz, seq=seq, tb=512, **gl)
    x1_s = _out_proj([oa_s, ob_s.reshape(ms, B_WIDTH)], w["w_out_even"], xs, None, tm=ms)
    proj1, lf1 = _odd_proj(x1_s, w["norm_odd"], w["w_odd"], w["w_r"], w["w_gate"], w["b_gate"], tm=ms)
    oc_s, sc_s = _gla_sample(False, [proj1, proj1, proj1, proj1, lf1], [0, 1, 1, 2, 0], None, w["gla_gain"],
                             ctx["state_gla"], t_new=t_new, **gl)
    y_s = _out_proj([oc_s], w["w_out_odd"], x1_s, w["final_norm"], tm=ms)

    heads4 = lambda a, n, s: a.reshape(n, s, B_HEADS, B_HEAD_DIM)
    logf = lambda a, n, s: a[:, :B_HEADS].reshape(n, s, B_HEADS)
    prompt = (y_p.reshape(bsz, seq, D_MODEL), heads4(krows_p, bsz, seq), heads4(vrows_p, bsz, seq),
              logf(lf_p, bsz, seq), sa_p, sc_p)
    sample = (y_s.reshape(db, t_new, D_MODEL), heads4(krows_s, db, t_new), heads4(vrows_s, db, t_new),
              logf(lf_s, db, t_new), sa_s, sc_s)
    return prompt, sample


def kernel(x_prompt, x_sample, cache_fox_k, cache_fox_v, cache_fox_logf, state_hgrn, state_gla, page_table,
           norm_even, w_in_even, b_fox_f, lb_logits, hgrn_gain, w_out_even, norm_odd, w_in_odd, w_gla_gate,
           b_gla_gate, gla_gain, w_out_odd, final_norm):
    weights = _prep_weights((norm_even, w_in_even, b_fox_f, lb_logits, hgrn_gain, w_out_even, norm_odd, w_in_odd,
                             w_gla_gate, b_gla_gate, gla_gain, w_out_odd, final_norm))
    bsz, seq, _ = x_prompt.shape
    n_pool = cache_fox_k.shape[1]
    pw = PAGE_SIZE * B_HEADS
    ctx = dict(
        page_table=page_table,
        k_pages=cache_fox_k[0].reshape(n_pool, pw, B_HEAD_DIM),
        v_pages=cache_fox_v[0].reshape(n_pool, pw, B_HEAD_DIM),
        lf_pages=cache_fox_logf[0].transpose(0, 2, 1).reshape(n_pool, pw),
        state_hgrn=state_hgrn[0], state_gla=state_gla[0])
    (y_p, kp, vp, lfp, hgrn_p, gla_p), (y_s, ks, vs, lfs, hgrn_s, gla_s) = _forward(x_prompt, x_sample, ctx, weights)
    n_pp = seq // PAGE_SIZE
    fox_k_prompt = kp.reshape(1, bsz, n_pp, PAGE_SIZE, B_HEADS, B_HEAD_DIM)
    fox_v_prompt = vp.reshape(1, bsz, n_pp, PAGE_SIZE, B_HEADS, B_HEAD_DIM)
    fox_logf_prompt = lfp.reshape(1, bsz, n_pp, PAGE_SIZE, B_HEADS)
    return (y_p, y_s, fox_k_prompt, fox_v_prompt, fox_logf_prompt, hgrn_p[None], gla_p[None],
            ks[None], vs[None], lfs[None], hgrn_s[None], gla_s[None])
```

```python
import functools

import numpy as np
import jax
import jax.numpy as jnp
from jax import lax
from jax.experimental import pallas as pl
from jax.experimental.pallas import tpu as pltpu

F32 = jnp.float32
BF16 = jnp.bfloat16

D_MODEL = 1024
PAGE_SIZE = 128
A_HEADS = 4
A_HEAD_DIM = 128
A_WIDTH = 512
B_HEADS = 4
B_HEAD_DIM = 128
B_WIDTH = 512
C_HEADS = 4
C_KEY_WIDTH = 512
C_VAL_WIDTH = 1024
C_KEY_DIM = 128
C_VAL_DIM = 256
C_GATE_RANK = 16
GLA_GATE_NORMALIZER = 16.0
EPS = 1e-6
HGRN_LAYER = 0
LOG2E = 1.4426950408889634
AUG_PIECES = 3
N_EVEN_MAIN = 4 * A_WIDTH + 4 * B_WIDTH
N_ODD_MAIN = 2 * C_KEY_WIDTH + 2 * C_VAL_WIDTH

LANES = 128
SUBLANES = 8
VMEM_LIMIT = 56 * 1024 * 1024
CHUNK = 64
MAX_CHUNK_LOG_DECAY = 60.0

_NT = (((1,), (1,)), ((), ()))
_TN = (((0,), (0,)), ((), ()))


def _cparams(sem):
    return pltpu.CompilerParams(dimension_semantics=sem, vmem_limit_bytes=VMEM_LIMIT)


def _sigmoid(x):
    return 1.0 / (1.0 + jnp.exp(-x))


def _log_sigmoid(x):
    return jnp.minimum(x, 0.0) - jnp.log1p(jnp.exp(-jnp.abs(x)))


def _silu(x):
    return x * _sigmoid(x)


def _rmsnorm_rows(x, g):
    return x * lax.rsqrt(jnp.mean(x * x, axis=-1, keepdims=True) + EPS) * g


def _split3(x):
    p1 = x.astype(BF16)
    r1 = x - p1.astype(F32)
    p2 = r1.astype(BF16)
    p3 = (r1 - p2.astype(F32)).astype(BF16)
    return p1, p2, p3


def _dot01(m01, x):
    acc = None
    for p in _split3(x):
        t = jnp.dot(m01, p, preferred_element_type=F32)
        acc = t if acc is None else acc + t
    return acc


def _x_dot01(x, m01):
    acc = None
    for p in _split3(x):
        t = jnp.dot(p, m01, preferred_element_type=F32)
        acc = t if acc is None else acc + t
    return acc


def _lower_tri(n, seq):
    r = lax.broadcasted_iota(jnp.int32, (n, n), 0)
    c = lax.broadcasted_iota(jnp.int32, (n, n), 1)
    keep = r >= c
    if seq < n:
        keep = jnp.logical_and(keep, (r // seq) == (c // seq))
    return jnp.where(keep, 1.0, 0.0).astype(BF16)


def _dot_wt(h, wt_ref, c0, n):
    return lax.dot_general(h, wt_ref[c0:c0 + n, :], _NT, preferred_element_type=F32)


def _proj_cols(h, wt_ref, out_ref):
    step = 512
    for c0 in range(0, wt_ref.shape[0], step):
        out_ref[:, c0:c0 + step] = _dot_wt(h, wt_ref, c0, step)


def _aug_selectors():
    sel_q = np.zeros((LANES, B_WIDTH), np.float32)
    sel_k = np.zeros((LANES, B_WIDTH), np.float32)
    for p in range(AUG_PIECES):
        for hd in range(B_HEADS):
            sel_q[p * B_HEADS + hd, hd * B_HEAD_DIM + p] = 1.0
            sel_k[p * B_HEADS + hd, hd * B_HEAD_DIM + AUG_PIECES + p] = -1.0
    return jnp.asarray(sel_q, BF16), jnp.asarray(sel_k, BF16)


def _even_proj_kernel(x_ref, g_ref, w_ref, wfb_ref, bfox_ref, selq_ref, selk_ref, pa_ref, gate_ref, qa_ref, ka_ref,
                      vb_ref, kout_ref, vout_ref, lf_ref, c_ref, carry_ref, *, tm, seq):
    i = pl.program_id(0)
    na = 4 * A_WIDTH
    bw = B_WIDTH
    dh = B_HEAD_DIM
    if seq > tm:
        @pl.when((i * tm) % seq == 0)
        def _():
            carry_ref[...] = jnp.zeros_like(carry_ref)

    h = _rmsnorm_rows(x_ref[...], g_ref[...]).astype(BF16)
    fb = _dot_wt(h, wfb_ref, 0, LANES) + bfox_ref[...]
    lane = lax.broadcasted_iota(jnp.int32, fb.shape, 1)
    lf = jnp.where(lane < B_HEADS, _log_sigmoid(fb), 0.0)
    lf_ref[...] = lf
    for c0 in range(0, na // 2, 512):
        pa_ref[:, c0:c0 + 512] = _dot_wt(h, w_ref, c0, 512)
    cs = _dot01(_lower_tri(tm, seq), lf)
    if seq > tm:
        cs = cs + carry_ref[0:1, :]
        carry_ref[0:1, :] = cs[tm - 1:tm, :]
    c_ref[...] = cs
    for c0 in range(na // 2, na, 512):
        pa_ref[:, c0:c0 + 512] = _dot_wt(h, w_ref, c0, 512)
    pos = lax.broadcasted_iota(jnp.int32, (1, bw), 1) % dh
    aug_q = jnp.where(jnp.logical_and(pos >= AUG_PIECES, pos < 2 * AUG_PIECES), 1.0, 0.0)
    aug_k = jnp.where(pos < AUG_PIECES, 1.0, 0.0)
    packed = None
    for p, piece in enumerate(_split3(cs * LOG2E)):
        shifted = piece.astype(F32) if p == 0 else pltpu.roll(piece.astype(F32), p * B_HEADS, 1)
        packed = shifted if packed is None else packed + shifted
    packed = packed.astype(BF16)
    aug_q = (aug_q + jnp.dot(packed, selq_ref[...], preferred_element_type=F32)).astype(BF16)
    aug_k = (aug_k + jnp.dot(packed, selk_ref[...], preferred_element_type=F32)).astype(BF16)
    q = (_dot_wt(h, w_ref, na, bw) * (dh ** -0.5 * LOG2E)).astype(BF16)
    for hd in range(B_HEADS):
        qa_ref[:, 2 * hd * dh:(2 * hd + 1) * dh] = q[:, hd * dh:(hd + 1) * dh]
    for j, out_ref in ((1, kout_ref), (2, vout_ref)):
        kv = _dot_wt(h, w_ref, na + j * bw, bw)
        kv16 = kv.astype(BF16)
        if j == 1:
            for hd in range(B_HEADS):
                ka_ref[:, 2 * hd * dh:(2 * hd + 1) * dh] = kv16[:, hd * dh:(hd + 1) * dh]
        else:
            vb_ref[...] = kv16
        for hd in range(B_HEADS):
            out_ref[pl.ds(hd, tm, stride=B_HEADS), :] = kv[:, hd * dh:(hd + 1) * dh]
    gate_ref[...] = _dot_wt(h, w_ref, na + 3 * bw, bw)
    for hd in range(B_HEADS):
        qa_ref[:, (2 * hd + 1) * dh:(2 * hd + 2) * dh] = aug_q[:, hd * dh:(hd + 1) * dh]
        ka_ref[:, (2 * hd + 1) * dh:(2 * hd + 2) * dh] = aug_k[:, hd * dh:(hd + 1) * dh]


def _even_proj(x, g, w_main, w_fb, b_fox, *, seq, tm):
    m = x.shape[0]
    const = lambda i: (0, 0)
    row = lambda i: (i, 0)
    sel_q, sel_k = _aug_selectors()
    return pl.pallas_call(
        functools.partial(_even_proj_kernel, tm=tm, seq=seq),
        grid=(m // tm,),
        in_specs=[pl.BlockSpec((tm, D_MODEL), row), pl.BlockSpec((1, D_MODEL), const),
                  pl.BlockSpec((N_EVEN_MAIN, D_MODEL), const), pl.BlockSpec(w_fb.shape, const),
                  pl.BlockSpec((1, LANES), const), pl.BlockSpec(sel_q.shape, const),
                  pl.BlockSpec(sel_k.shape, const)],
        out_specs=[pl.BlockSpec((tm, 4 * A_WIDTH), row), pl.BlockSpec((tm, B_WIDTH), row),
                   pl.BlockSpec((tm, 2 * B_WIDTH), row), pl.BlockSpec((tm, 2 * B_WIDTH), row),
                   pl.BlockSpec((tm, B_WIDTH), row), pl.BlockSpec((tm * B_HEADS, B_HEAD_DIM), row),
                   pl.BlockSpec((tm * B_HEADS, B_HEAD_DIM), row), pl.BlockSpec((tm, LANES), row),
                   pl.BlockSpec((tm, LANES), row)],
        out_shape=[jax.ShapeDtypeStruct((m, 4 * A_WIDTH), F32), jax.ShapeDtypeStruct((m, B_WIDTH), F32),
                   jax.ShapeDtypeStruct((m, 2 * B_WIDTH), BF16), jax.ShapeDtypeStruct((m, 2 * B_WIDTH), BF16),
                   jax.ShapeDtypeStruct((m, B_WIDTH), BF16),
                   jax.ShapeDtypeStruct((m * B_HEADS, B_HEAD_DIM), F32),
                   jax.ShapeDtypeStruct((m * B_HEADS, B_HEAD_DIM), F32),
                   jax.ShapeDtypeStruct((m, LANES), F32), jax.ShapeDtypeStruct((m, LANES), F32)],
        scratch_shapes=[pltpu.VMEM((SUBLANES, LANES), F32)],
        compiler_params=_cparams(("arbitrary",)),
        name="even_proj",
    )(x, g, w_main, w_fb, b_fox, sel_q, sel_k)


def _odd_proj_kernel(x_ref, g_ref, w_ref, wr_ref, wg_ref, bg_ref, proj_ref, lf_ref):
    h = _rmsnorm_rows(x_ref[...], g_ref[...]).astype(BF16)
    _proj_cols(h, w_ref, proj_ref)
    r = _dot_wt(h, wr_ref, 0, LANES)
    z = jnp.dot(r.astype(BF16), wg_ref[...], preferred_element_type=F32) + bg_ref[...]
    lf_ref[...] = _log_sigmoid(z) / GLA_GATE_NORMALIZER


def _odd_proj(x, g, w_main, w_r, w_gate, b_gate, *, tm):
    m = x.shape[0]
    n = N_ODD_MAIN
    const = lambda i: (0, 0)
    row = lambda i: (i, 0)
    return pl.pallas_call(
        _odd_proj_kernel,
        grid=(m // tm,),
        in_specs=[pl.BlockSpec((tm, D_MODEL), row), pl.BlockSpec((1, D_MODEL), const),
                  pl.BlockSpec((n, D_MODEL), const), pl.BlockSpec(w_r.shape, const),
                  pl.BlockSpec((LANES, C_KEY_WIDTH), const), pl.BlockSpec((1, C_KEY_WIDTH), const)],
        out_specs=[pl.BlockSpec((tm, n), row), pl.BlockSpec((tm, C_KEY_WIDTH), row)],
        out_shape=[jax.ShapeDtypeStruct((m, n), F32), jax.ShapeDtypeStruct((m, C_KEY_WIDTH), F32)],
        compiler_params=_cparams(("arbitrary",)),
        name="odd_proj",
    )(x, g, w_main, w_r, w_gate, b_gate)


def _lower_bound(logits, layer):
    e = jnp.exp(logits - jnp.max(logits, axis=0, keepdims=True))
    return jnp.sum(e[:layer + 1, :], axis=0, keepdims=True) / jnp.sum(e, axis=0, keepdims=True)


def _hgrn_gate(fa, lb):
    f = lb + (1.0 - lb) * _sigmoid(fa)
    return jnp.log(f), 1.0 - f


def _exact_groups(items):
    n = SUBLANES
    row = lax.broadcasted_iota(jnp.int32, (n, 1), 0)
    prepped = []
    for q, k, v, g, st, lo, hi in items:
        valid = jnp.logical_and(row >= lo, row < hi)
        q = jnp.where(valid, q, 0.0)
        k = jnp.where(valid, k, 0.0)
        b = jnp.where(valid, g, 0.0)
        for sh in (1, 2, 4):
            b = b + jnp.where(row >= sh, pltpu.roll(b, sh, 0), 0.0)
        prepped.append((q, k, v, b, st, lo, hi))
    inter = [lax.dot_general((q * jnp.exp(b)).astype(BF16), st.astype(BF16), _NT, preferred_element_type=F32)
             for q, k, v, b, st, lo, hi in prepped]
    outer = [lax.dot_general(v.astype(BF16), (k * jnp.exp(b[n - 1:n, :] - b)).astype(BF16), _TN,
                             preferred_element_type=F32)
             for q, k, v, b, st, lo, hi in prepped]
    results = []
    for (q, k, v, b, st, lo, hi), o, upd in zip(prepped, inter, outer):
        for s in range(lo, hi):
            w = jnp.exp(jnp.minimum(b - b[s:s + 1, :], 0.0))
            a = jnp.sum(q * k[s:s + 1, :] * w, axis=-1, keepdims=True)
            o = o + jnp.where(row >= s, a, 0.0) * v[s:s + 1, :]
        results.append((o, st * jnp.exp(b[n - 1:n, :]) + upd))
    return results


def _head_norm_gate(o, gain, gate):
    y = o * lax.rsqrt(jnp.mean(o * o, axis=-1, keepdims=True) + EPS) * gain
    return y * _silu(gate)


def _recurrence_block(q_ref, k_ref, v_ref, gate_ref, g_ref, gain_ref, o_ref, st_ref, b_ref, *, fa_ref, lb, heads, dk,
                      dv, tb, q_scale):
    n_chunks = tb // CHUNK

    tri = _lower_tri(CHUNK, CHUNK)
    b_min = None
    for c in range(n_chunks):
        rows = slice(c * CHUNK, (c + 1) * CHUNK)
        if fa_ref is not None:
            g, k = _hgrn_gate(fa_ref[rows, :], lb)
            g_ref[rows, :] = g
            k_ref[rows, :] = k
        else:
            g = g_ref[rows, :]
        b = _dot01(tri, g)
        b_ref[rows, :] = b
        b_last = b[CHUNK - 1:CHUNK, :]
        b_min = b_last if b_min is None else jnp.minimum(b_min, b_last)
    chunk_form_ok = jnp.min(b_min) >= -MAX_CHUNK_LOG_DECAY

    def finish(rows, h, o_h):
        vsl = slice(h * dv, (h + 1) * dv)
        o_ref[rows, vsl] = _head_norm_gate(o_h, gain_ref[:, vsl], gate_ref[rows, vsl]).astype(o_ref.dtype)

    @pl.when(chunk_form_ok)
    def _():
        r_i = lax.broadcasted_iota(jnp.int32, (CHUNK, CHUNK), 0)
        c_i = lax.broadcasted_iota(jnp.int32, (CHUNK, CHUNK), 1)
        causal = r_i >= c_i

        for c in range(n_chunks):
            rows = slice(c * CHUNK, (c + 1) * CHUNK)
            b = b_ref[rows, :]
            e_b = jnp.exp(b)
            b_last = b[CHUNK - 1:CHUNK, :]
            e_last = jnp.exp(b_last)
            q_t = q_ref[rows, :] * q_scale * e_b
            k_t = k_ref[rows, :] * jnp.exp(-b)
            k_hat = k_t * e_last
            ksl = [slice(h * dk, (h + 1) * dk) for h in range(heads)]
            qh = [q_t[:, s].astype(BF16) for s in ksl]
            vh = [v_ref[rows, h * dv:(h + 1) * dv].astype(BF16) for h in range(heads)]
            st = [st_ref[h] for h in range(heads)]
            a = [lax.dot_general(qh[h], k_t[:, ksl[h]].astype(BF16), _NT, preferred_element_type=F32)
                 for h in range(heads)]
            inter = [lax.dot_general(qh[h], st[h].astype(BF16), _NT, preferred_element_type=F32)
                     for h in range(heads)]
            outer = [lax.dot_general(vh[h], k_hat[:, ksl[h]].astype(BF16), _TN, preferred_element_type=F32)
                     for h in range(heads)]
            for h in range(heads):
                st_ref[h] = st[h] * e_last[:, ksl[h]] + outer[h]
            for h in range(heads):
                a_h = jnp.where(causal, a[h], 0.0).astype(BF16)
                finish(rows, h, jnp.dot(a_h, vh[h], preferred_element_type=F32) + inter[h])

    @pl.when(jnp.logical_not(chunk_form_ok))
    def _():
        pair = 2 * SUBLANES

        def group(i, carry):
            outs = [[] for _ in range(heads)]
            for half in range(2):
                rows = pl.ds(pl.multiple_of(i * pair + half * SUBLANES, SUBLANES), SUBLANES)
                items = [(q_ref[rows, h * dk:(h + 1) * dk] * q_scale, k_ref[rows, h * dk:(h + 1) * dk],
                          v_ref[rows, h * dv:(h + 1) * dv], g_ref[rows, h * dk:(h + 1) * dk], st_ref[h], 0, SUBLANES)
                         for h in range(heads)]
                for h, (o_h, st_new) in enumerate(_exact_groups(items)):
                    st_ref[h] = st_new
                    outs[h].append(o_h)
            for h in range(heads):
                finish(pl.ds(pl.multiple_of(i * pair, pair), pair), h, jnp.concatenate(outs[h], axis=0))
            return carry

        lax.fori_loop(0, tb // pair, group, 0)


def _state_step_edges(st_ref, s_ref):
    t = pl.program_id(1)

    def first():
        @pl.when(t == 0)
        def _():
            st_ref[...] = jnp.zeros_like(st_ref)

    def last():
        @pl.when(t == pl.num_programs(1) - 1)
        def _():
            for h in range(st_ref.shape[0]):
                s_ref[0, h] = st_ref[h].T

    return first, last


def _hgrn_prompt_kernel(q_ref, fa_ref, v_ref, gate_ref, lb_ref, gain_ref, o_ref, s_ref, st_ref, b_ref, g_ref, k_ref,
                        **kw):
    first, last = _state_step_edges(st_ref, s_ref)
    first()
    _recurrence_block(q_ref, k_ref, v_ref, gate_ref, g_ref, gain_ref, o_ref, st_ref, b_ref, fa_ref=fa_ref,
                      lb=_lower_bound(lb_ref[...], HGRN_LAYER), **kw)
    last()


def _hgrn_prompt(pa, lb_logits, gain, *, bsz, seq, heads, dk, dv, tb, q_scale):
    nt = seq // tb
    kw, vw = heads * dk, heads * dv
    col = lambda cb: (lambda b, t: (b * nt + t, cb))
    const = lambda b, t: (0, 0)
    return pl.pallas_call(
        functools.partial(_hgrn_prompt_kernel, heads=heads, dk=dk, dv=dv, tb=tb, q_scale=q_scale),
        grid=(bsz, nt),
        in_specs=[pl.BlockSpec((tb, kw), col(0)), pl.BlockSpec((tb, kw), col(1)), pl.BlockSpec((tb, vw), col(2)),
                  pl.BlockSpec((tb, vw), col(3)), pl.BlockSpec(lb_logits.shape, const),
                  pl.BlockSpec((1, vw), const)],
        out_specs=[pl.BlockSpec((tb, vw), col(0)), pl.BlockSpec((1, heads, dk, dv), lambda b, t: (b, 0, 0, 0))],
        out_shape=[jax.ShapeDtypeStruct((bsz * seq, vw), BF16), jax.ShapeDtypeStruct((bsz, heads, dk, dv), F32)],
        scratch_shapes=[pltpu.VMEM((heads, dv, dk), F32), pltpu.VMEM((tb, kw), F32), pltpu.VMEM((tb, kw), F32),
                        pltpu.VMEM((tb, kw), F32)],
        compiler_params=_cparams(("arbitrary", "arbitrary")),
        name="hgrn_prompt",
    )(pa, pa, pa, pa, lb_logits, gain)


def _layer1_prompt_kernel(x_ref, oa_ref, ob_ref, woe_ref, gn_ref, w_ref, wr_ref, wg_ref, bg_ref, gain_ref, woo_ref,
                          fn_ref, y_ref, s_ref, st_ref, b_ref, proj_ref, lf_ref, o_ref, *, heads, dk, dv, tb,
                          q_scale):
    first, last = _state_step_edges(st_ref, s_ref)
    first()
    kw, vw = heads * dk, heads * dv
    aw = oa_ref.shape[1]
    x1 = (x_ref[...] + jnp.dot(oa_ref[...], woe_ref[0:aw, :], preferred_element_type=F32)
          + jnp.dot(ob_ref[...], woe_ref[aw:, :], preferred_element_type=F32))
    h = _rmsnorm_rows(x1, gn_ref[...]).astype(BF16)
    r = _dot_wt(h, wr_ref, 0, LANES)
    z = jnp.dot(r.astype(BF16), wg_ref[...], preferred_element_type=F32) + bg_ref[...]
    lf_ref[...] = _log_sigmoid(z) / GLA_GATE_NORMALIZER
    _proj_cols(h, w_ref, proj_ref)
    _recurrence_block(proj_ref.at[:, 0:kw], proj_ref.at[:, kw:2 * kw], proj_ref.at[:, 2 * kw:2 * kw + vw],
                      proj_ref.at[:, 2 * kw + vw:2 * kw + 2 * vw], lf_ref, gain_ref, o_ref, st_ref, b_ref,
                      fa_ref=None, lb=None, heads=heads, dk=dk, dv=dv, tb=tb, q_scale=q_scale)
    y = x1 + jnp.dot(o_ref[...], woo_ref[...], preferred_element_type=F32)
    y_ref[...] = _rmsnorm_rows(y, fn_ref[...])
    last()


def _layer1_prompt(x, o_a, o_b, w, *, bsz, seq, heads, dk, dv, tb, q_scale):
    nt = seq // tb
    kw, vw = heads * dk, heads * dv
    row = lambda b, t: (b * nt + t, 0)
    const = lambda b, t: (0, 0)
    full = lambda a: pl.BlockSpec((N_ODD_MAIN, D_MODEL) if a is w["w_odd"] else a.shape, const)
    weights = [w["w_out_even"], w["norm_odd"], w["w_odd"], w["w_r"], w["w_gate"], w["b_gate"], w["gla_gain"],
               w["w_out_odd"], w["final_norm"]]
    return pl.pallas_call(
        functools.partial(_layer1_prompt_kernel, heads=heads, dk=dk, dv=dv, tb=tb, q_scale=q_scale),
        grid=(bsz, nt),
        in_specs=[pl.BlockSpec((tb, D_MODEL), row), pl.BlockSpec((tb, o_a.shape[1]), row),
                  pl.BlockSpec((tb, o_b.shape[1]), row)] + [full(a) for a in weights],
        out_specs=[pl.BlockSpec((tb, D_MODEL), row), pl.BlockSpec((1, heads, dk, dv), lambda b, t: (b, 0, 0, 0))],
        out_shape=[jax.ShapeDtypeStruct((bsz * seq, D_MODEL), F32),
                   jax.ShapeDtypeStruct((bsz, heads, dk, dv), F32)],
        scratch_shapes=[pltpu.VMEM((heads, dv, dk), F32), pltpu.VMEM((tb, kw), F32),
                        pltpu.VMEM((tb, 2 * kw + 2 * vw), F32), pltpu.VMEM((tb, kw), F32),
                        pltpu.VMEM((tb, vw), BF16)],
        compiler_params=_cparams(("arbitrary", "arbitrary")),
        name="layer1_prompt",
    )(x, o_a, o_b, *weights)


SAMPLE_GROUPS = 1


def _gla_sample_kernel(*refs, hgrn, heads, dk, dv, t_new, q_scale, groups):
    if hgrn:
        q_ref, fa_ref, v_ref, gate_ref, lb_ref, gain_ref, s0_ref, o_ref, s_ref = refs
    else:
        q_ref, k_ref, v_ref, gate_ref, g_ref, gain_ref, s0_ref, o_ref, s_ref = refs
    per = SUBLANES // t_new
    for grp in range(groups):
        rows = slice(grp * SUBLANES, (grp + 1) * SUBLANES)
        items = []
        for h in range(heads):
            ksl = slice(h * dk, (h + 1) * dk)
            vsl = slice(h * dv, (h + 1) * dv)
            if hgrn:
                g, k = _hgrn_gate(fa_ref[rows, ksl], _lower_bound(lb_ref[...], HGRN_LAYER)[:, ksl])
            else:
                g, k = g_ref[rows, ksl], k_ref[rows, ksl]
            q = q_ref[rows, ksl] * q_scale
            v = v_ref[rows, vsl]
            for e in range(per):
                items.append((q, k, v, g, s0_ref[grp * per + e, h].T, e * t_new, (e + 1) * t_new))
        results = _exact_groups(items)
        for h in range(heads):
            vsl = slice(h * dv, (h + 1) * dv)
            o_h = None
            for e in range(per):
                o_e, st_new = results[h * per + e]
                s_ref[grp * per + e, h] = st_new.T
                o_h = o_e if o_h is None else o_h + o_e
            o_ref[rows, vsl] = _head_norm_gate(o_h, gain_ref[:, vsl], gate_ref[rows, vsl])


def _gla_sample(hgrn, arrays, col_blocks, small, gain, s0, *, t_new, heads, dk, dv, q_scale):
    m = arrays[0].shape[0]
    groups = SAMPLE_GROUPS if m % (SAMPLE_GROUPS * SUBLANES) == 0 else 1
    tr = groups * SUBLANES
    per = tr // t_new
    kw, vw = heads * dk, heads * dv
    widths = [kw, kw, vw, vw] + ([] if hgrn else [kw])
    in_specs = [pl.BlockSpec((tr, w), functools.partial(lambda i, cb: (i, cb), cb=cb))
                for w, cb in zip(widths, col_blocks)]
    operands = list(arrays)
    if hgrn:
        in_specs.append(pl.BlockSpec(small.shape, lambda i: (0, 0)))
        operands.append(small)
    in_specs.append(pl.BlockSpec((1, vw), lambda i: (0, 0)))
    operands.append(gain)
    in_specs.append(pl.BlockSpec((per, heads, dk, dv), lambda i: (i, 0, 0, 0)))
    operands.append(s0)
    return pl.pallas_call(
        functools.partial(_gla_sample_kernel, hgrn=hgrn, heads=heads, dk=dk, dv=dv, t_new=t_new, q_scale=q_scale,
                          groups=groups),
        grid=(m // tr,),
        in_specs=in_specs,
        out_specs=[pl.BlockSpec((tr, vw), lambda i: (i, 0)),
                   pl.BlockSpec((per, heads, dk, dv), lambda i: (i, 0, 0, 0))],
        out_shape=[jax.ShapeDtypeStruct((m, vw), F32), jax.ShapeDtypeStruct(s0.shape, F32)],
        compiler_params=_cparams(("arbitrary",)),
        name="hgrn_sample" if hgrn else "gla_sample",
    )(*operands)


FOX_STRIP = 64


def _fox_prompt_step(qi, ki, before_head, q_ref, k_ref, v_ref, gate_ref, o_ref, m_ref, l_ref, acc_ref, p_ref, a_ref,
                     *, tq, heads, dh):
    aw = 2 * dh

    @pl.when(ki == 0)
    def _():
        m_ref[...] = jnp.full_like(m_ref, -jnp.inf)
        l_ref[...] = jnp.zeros_like(l_ref)
        acc_ref[...] = jnp.zeros_like(acc_ref)

    def step(diag):
        if diag:
            r_i = lax.broadcasted_iota(jnp.int32, (FOX_STRIP, LANES), 0)
            c_i = lax.broadcasted_iota(jnp.int32, (FOX_STRIP, LANES), 1)
        half = tq // 2
        segments = [(0, half, half), (half, tq, tq)] if diag and half % LANES == 0 else [(0, tq, tq)]
        for h in range(heads):
            interleaved_update = before_head(h)
            sl = slice(h * dh, (h + 1) * dh)
            hw = slice(h * aw, (h + 1) * aw)
            for lo, hi, ext in segments:
                s = lax.dot_general(q_ref[lo:hi, hw], k_ref[0:ext, hw], _NT, preferred_element_type=F32)
                if lo == 0:
                    after_head = interleaved_update()
                for r0 in range(lo, hi, FOX_STRIP):
                    rows = slice(r0, r0 + FOX_STRIP)
                    srows = slice(r0 - lo, r0 - lo + FOX_STRIP)
                    live = [j for j in range(ext // LANES) if not (diag and j * LANES > r0 + FOX_STRIP - 1)]
                    blocks = []
                    for j in live:
                        lg = s[srows, j * LANES:(j + 1) * LANES]
                        if diag and (j + 1) * LANES - 1 > r0:
                            lg = jnp.where(r_i + r0 >= c_i + j * LANES, lg, -jnp.inf)
                        blocks.append(lg)
                    m_cur = blocks[0]
                    for lg in blocks[1:]:
                        m_cur = jnp.maximum(m_cur, lg)
                    m_prev = m_ref[h, rows]
                    m_new = jnp.maximum(m_prev, jnp.max(m_cur, axis=-1, keepdims=True))
                    alpha = jnp.exp2(m_prev - m_new)
                    probs = [jnp.exp2(lg - m_new) for lg in blocks]
                    row_sum = probs[0]
                    for pj in probs[1:]:
                        row_sum = row_sum + pj
                    l_ref[h, rows] = alpha * l_ref[h, rows] + jnp.sum(row_sum, axis=-1, keepdims=True)
                    m_ref[h, rows] = m_new
                    a_ref[rows] = alpha
                    for j, pj in zip(live, probs):
                        p_ref[rows, j * LANES:(j + 1) * LANES] = pj.astype(BF16)
                    for j in range(ext // LANES):
                        if j not in live:
                            p_ref[rows, j * LANES:(j + 1) * LANES] = jnp.zeros((FOX_STRIP, LANES), BF16)
                acc_ref[lo:hi, sl] = a_ref[lo:hi] * acc_ref[lo:hi, sl] + jnp.dot(
                    p_ref[lo:hi, 0:ext], v_ref[0:ext, sl], preferred_element_type=F32)
            after_head()

    @pl.when(ki < qi)
    def _():
        step(False)

    @pl.when(ki == qi)
    def _():
        step(True)
        for h in range(heads):
            sl = slice(h * dh, (h + 1) * dh)
            o_ref[:, sl] = (acc_ref[:, sl] / l_ref[h] * _silu(gate_ref[:, sl])).astype(o_ref.dtype)


N_PROMPT_REFS = 4
N_SAMPLE_REFS = 9


def _fox_kernel(qi_ref, ki_ref, pt_ref, *refs, tq, heads, dh, sample_kw):
    prompt_in = refs[:N_PROMPT_REFS]
    sample_in = refs[N_PROMPT_REFS:N_PROMPT_REFS + N_SAMPLE_REFS]
    o_ref, os_ref = refs[N_PROMPT_REFS + N_SAMPLE_REFS:N_PROMPT_REFS + N_SAMPLE_REFS + 2]
    scratch = refs[N_PROMPT_REFS + N_SAMPLE_REFS + 2:]
    prompt_scratch, sample_scratch = scratch[:5], scratch[5:]
    p = pl.program_id(1)
    step = pl.program_id(0) * pl.num_programs(1) + p

    def before_head(h):
        return _fox_sample_unit(step * heads + h, pt_ref, *sample_in, os_ref, *sample_scratch, heads=heads, dh=dh,
                                **sample_kw)

    _fox_prompt_step(qi_ref[p], ki_ref[p], before_head, *prompt_in, o_ref, *prompt_scratch, tq=tq, heads=heads, dh=dh)


def _fox_attention(qa, ka, vb, gate, page_table, k_pages, v_pages, lf_pages, q_s, k_new, v_new, gate_s, cn_col,
                   cn_row, *, bsz, seq, tq, pages_per_unit):
    nq = seq // tq
    pairs = [(qi, ki) for qi in range(nq) for ki in range(qi + 1)]
    qi_tab = jnp.asarray(np.array([p[0] for p in pairs], np.int32))
    ki_tab = jnp.asarray(np.array([p[1] for p in pairs], np.int32))
    w = B_WIDTH
    dh = B_HEAD_DIM
    n_pool = lf_pages.shape[0]
    sfx = _page_suffix(lf_pages, B_HEADS)
    db, n_pages = page_table.shape
    pp = pages_per_unit
    n_units = db * (n_pages // pp)
    assert bsz * len(pairs) * B_HEADS >= n_units, "not enough prompt steps to carry the sample page units"
    nrow = q_s.shape[1]
    pw = PAGE_SIZE * B_HEADS
    q_map = lambda b, p, qt, kt, pt: (b * nq + qt[p], 0)
    k_map = lambda b, p, qt, kt, pt: (b * nq + kt[p], 0)
    whole = lambda a: pl.BlockSpec(a.shape, lambda b, p, qt, kt, pt: (0,) * a.ndim)
    hbm = pl.BlockSpec(memory_space=pl.ANY)
    n_slots = 2
    grid_spec = pltpu.PrefetchScalarGridSpec(
        num_scalar_prefetch=3,
        grid=(bsz, len(pairs)),
        in_specs=[pl.BlockSpec((tq, 2 * w), q_map), pl.BlockSpec((tq, 2 * w), k_map),
                  pl.BlockSpec((tq, w), k_map), pl.BlockSpec((tq, w), q_map),
                  hbm, hbm, hbm, whole(q_s), whole(k_new), whole(v_new), whole(gate_s), whole(cn_col),
                  whole(cn_row)],
        out_specs=[pl.BlockSpec((tq, w), q_map),
                   pl.BlockSpec((db, nrow, dh), lambda b, p, qt, kt, pt: (0, 0, 0))],
        scratch_shapes=[pltpu.VMEM((B_HEADS, tq, LANES), F32), pltpu.VMEM((B_HEADS, tq, LANES), F32),
                        pltpu.VMEM((tq, w), F32), pltpu.VMEM((tq, tq), BF16), pltpu.VMEM((tq, LANES), F32),
                        pltpu.VMEM((n_slots, pp, pw, dh), F32), pltpu.VMEM((n_slots, pp, pw, dh), F32),
                        pltpu.VMEM((n_slots, pp, 1, 2 * pw), F32), pltpu.SemaphoreType.DMA((3, n_slots)),
                        pltpu.VMEM((nrow, 1), F32), pltpu.VMEM((nrow, 1), F32), pltpu.VMEM((nrow, dh), F32),
                        pltpu.VMEM((nrow, PAGE_SIZE), F32)],
    )
    sample_kw = dict(pp=pp, t_new=nrow // B_HEADS, n_pages=n_pages, n_units=n_units)
    return pl.pallas_call(
        functools.partial(_fox_kernel, tq=tq, heads=B_HEADS, dh=dh, sample_kw=sample_kw),
        grid_spec=grid_spec,
        out_shape=[jax.ShapeDtypeStruct((bsz * seq, w), BF16), jax.ShapeDtypeStruct((db, nrow, dh), F32)],
        compiler_params=_cparams(("arbitrary", "arbitrary")),
        name="fox_attention",
    )(qi_tab, ki_tab, page_table, qa, ka, vb, gate, k_pages, v_pages, sfx, q_s, k_new, v_new, gate_s, cn_col,
      cn_row)


def _page_suffix_kernel(lf_ref, later_ref, total_ref, out_ref):
    lf = lf_ref[...]
    pw = lf.shape[1]
    out_ref[:, 0:pw] = _x_dot01(lf, later_ref[...])
    out_ref[:, pw:2 * pw] = _x_dot01(lf, total_ref[...])


def _page_suffix(lf_pages, heads):
    n_pool, pw = lf_pages.shape
    tm = 512 if n_pool % 512 == 0 else n_pool
    j = np.arange(pw)
    in_head, in_tok = j[:, None] // (pw // heads), j[:, None] % (pw // heads)
    out_head, out_tok = j[None, :] // (pw // heads), j[None, :] % (pw // heads)
    head_eq = in_head == out_head
    later = jnp.asarray(head_eq & (in_tok > out_tok), BF16)
    total = jnp.asarray(head_eq, BF16)
    row = lambda i: (i, 0)
    const = lambda i: (0, 0)
    return pl.pallas_call(
        _page_suffix_kernel,
        grid=(n_pool // tm,),
        in_specs=[pl.BlockSpec((tm, pw), row), pl.BlockSpec((pw, pw), const), pl.BlockSpec((pw, pw), const)],
        out_specs=pl.BlockSpec((tm, 2 * pw), row),
        out_shape=jax.ShapeDtypeStruct((n_pool, 2 * pw), F32),
        compiler_params=_cparams(("arbitrary",)),
        name="page_suffix",
    )(lf_pages, later, total)


def _fox_sample_unit(u, pt_ref, k_hbm, v_hbm, sfx_hbm, q_ref, kn_ref, vn_ref, gate_ref, cn_col_ref, cn_row_ref, o_ref,
                     kbuf, vbuf, sbuf, sem, m_ref, l_ref, acc_ref, carry_ref, *, pp, heads, dh, t_new, n_pages,
                     n_units):
    nj = n_pages // pp
    valid = u < n_units
    uc = jnp.minimum(u, n_units - 1)
    b = lax.div(uc, nj)
    j = lax.rem(uc, nj)
    nrow = t_new * heads
    pw = PAGE_SIZE * heads
    slot = lax.rem(uc, 2)

    def page_copies(uu, sl):
        bb = lax.div(uu, nj)
        jj = lax.rem(uu, nj)
        copies = []
        for i in range(pp):
            page = pt_ref[bb, n_pages - 1 - (jj * pp + i)]
            sources = (k_hbm.at[page], v_hbm.at[page], sfx_hbm.at[pl.ds(page, 1)])
            for kind, (src, dst) in enumerate(zip(sources, (kbuf, vbuf, sbuf))):
                copies.append(pltpu.make_async_copy(src, dst.at[sl, i], sem.at[kind, sl]))
        return copies

    @pl.when(u == 0)
    def _():
        for cp in page_copies(u, slot):
            cp.start()

    @pl.when(u + 1 < n_units)
    def _():
        for cp in page_copies(u + 1, 1 - slot):
            cp.start()

    @pl.when(valid)
    def _():
        for cp in page_copies(u, slot):
            cp.wait()

    k_refs = [kbuf.at[slot, i] for i in range(pp)]
    v_refs = [vbuf.at[slot, i] for i in range(pp)]
    sfx_refs = [sbuf.at[slot, i] for i in range(pp)]

    @pl.when(j == 0)
    def _():
        m_ref[...] = jnp.full_like(m_ref, -jnp.inf)
        l_ref[...] = jnp.zeros_like(l_ref)
        acc_ref[...] = jnp.zeros_like(acc_ref)
        carry_ref[...] = jnp.zeros_like(carry_ref)

    q = q_ref[b]
    cn_col = cn_col_ref[b] * LOG2E
    row_head = lax.broadcasted_iota(jnp.int32, (nrow, dh), 0) % heads
    q_wide = jnp.concatenate([jnp.where(row_head == h, q, jnp.zeros_like(q)) for h in range(heads)], axis=1)

    def page_wide(ref):
        return jnp.concatenate([ref[pl.ds(h, PAGE_SIZE, stride=heads), :] for h in range(heads)],
                               axis=1).astype(BF16)

    def per_row_head(row):
        by_head = jnp.concatenate([row[:, h * PAGE_SIZE:(h + 1) * PAGE_SIZE] for h in range(heads)], axis=0)
        return jnp.concatenate([by_head] * t_new, axis=0)

    def own_head(wide):
        out = None
        for h in range(heads):
            part = jnp.where(row_head == h, wide[:, h * dh:(h + 1) * dh], 0.0)
            out = part if out is None else out + part
        return out

    def online(logit_list, weighted_values):
        m_cur = logit_list[0]
        for lg in logit_list[1:]:
            m_cur = jnp.maximum(m_cur, lg)
        m_prev = m_ref[...]
        m_new = jnp.maximum(m_prev, jnp.max(m_cur, axis=-1, keepdims=True))
        alpha = jnp.exp2(m_prev - m_new)
        probs = [jnp.exp2(lg - m_new) for lg in logit_list]
        row_sum = probs[0]
        for pj in probs[1:]:
            row_sum = row_sum + pj
        l_ref[...] = alpha * l_ref[...] + jnp.sum(row_sum, axis=-1, keepdims=True)
        acc_ref[...] = alpha * acc_ref[...] + weighted_values([pj.astype(BF16) for pj in probs])
        m_ref[...] = m_new

    logit_list = []

    def scores():
        carry = carry_ref[...]
        for i in range(pp):
            suffix = (carry + per_row_head(sfx_refs[i][:, 0:pw])) * LOG2E
            carry = carry + per_row_head(sfx_refs[i][:, pw:2 * pw])
            s = lax.dot_general(q_wide, page_wide(k_refs[i]), _NT, preferred_element_type=F32)
            logit_list.append(s + cn_col + suffix)
        carry_ref[...] = carry
        return update

    def page_values(probs):
        wide = None
        for pj, v_ref in zip(probs, v_refs):
            t = jnp.dot(pj, page_wide(v_ref), preferred_element_type=F32)
            wide = t if wide is None else wide + t
        return own_head(wide)

    def update():
        online(logit_list, page_values)
        return finish

    def finish():
        @pl.when(jnp.logical_and(valid, j == nj - 1))
        def _():
            s = lax.dot_general(q, kn_ref[b].astype(BF16), _NT, preferred_element_type=F32)
            logits = s + cn_col - cn_row_ref[b] * LOG2E
            r = lax.broadcasted_iota(jnp.int32, (nrow, nrow), 0)
            c = lax.broadcasted_iota(jnp.int32, (nrow, nrow), 1)
            keep = jnp.logical_and((r % heads) == (c % heads), (r // heads) >= (c // heads))
            online([jnp.where(keep, logits, -jnp.inf)],
                   lambda probs: jnp.dot(probs[0], vn_ref[b].astype(BF16), preferred_element_type=F32))
            o_ref[b] = acc_ref[...] / l_ref[...] * _silu(gate_ref[b])

    return scores()


def _out_proj_kernel(*refs, n_in, final):
    ins = refs[:n_in]
    w_ref, x_ref = refs[n_in], refs[n_in + 1]
    y = x_ref[...]
    k0 = 0
    for a_ref in ins:
        kw = a_ref.shape[1]
        y = y + jnp.dot(a_ref[...].astype(BF16), w_ref[k0:k0 + kw, :], preferred_element_type=F32)
        k0 += kw
    if final:
        g_ref, o_ref = refs[n_in + 2], refs[n_in + 3]
        o_ref[...] = _rmsnorm_rows(y, g_ref[...])
    else:
        refs[n_in + 2][...] = y


def _out_proj(ins, w, x, final_gain, *, tm):
    m = x.shape[0]
    const = lambda i: (0, 0)
    row = lambda i: (i, 0)
    in_specs = [pl.BlockSpec((tm, a.shape[1]), row) for a in ins]
    in_specs += [pl.BlockSpec(w.shape, const), pl.BlockSpec((tm, D_MODEL), row)]
    operands = list(ins) + [w, x]
    if final_gain is not None:
        in_specs.append(pl.BlockSpec((1, D_MODEL), const))
        operands.append(final_gain)
    return pl.pallas_call(
        functools.partial(_out_proj_kernel, n_in=len(ins), final=final_gain is not None),
        grid=(m // tm,),
        in_specs=in_specs,
        out_specs=pl.BlockSpec((tm, D_MODEL), row),
        out_shape=jax.ShapeDtypeStruct((m, D_MODEL), F32),
        compiler_params=_cparams(("arbitrary",)),
        name="out_proj_final" if final_gain is not None else "out_proj",
    )(*operands)


def _pad_cols(w, n):
    return jnp.pad(w, ((0, 0), (0, n - w.shape[1])))


def _prep_weights(weights):
    (norm_even, w_in_even, b_fox_f, lb_logits, hgrn_gain, w_out_even, norm_odd, w_in_odd, w_gla_gate, b_gla_gate,
     gla_gain, w_out_odd, final_norm) = weights
    w_even_t = w_in_even[0].T.astype(BF16)
    w_odd_t = w_in_odd[0].T.astype(BF16)
    pad_rows = lambda a: jnp.pad(a, ((0, LANES - a.shape[0]), (0, 0)))
    return dict(
        norm_even=norm_even[0].reshape(1, D_MODEL),
        w_even=w_even_t,
        w_fb=pad_rows(w_even_t[N_EVEN_MAIN:]),
        b_fox=_pad_cols(b_fox_f[0].reshape(1, B_HEADS), LANES),
        lb_logits=lb_logits,
        hgrn_gain=hgrn_gain[0].reshape(1, A_WIDTH),
        w_out_even=w_out_even[0].astype(BF16),
        norm_odd=norm_odd[0].reshape(1, D_MODEL),
        w_odd=w_odd_t,
        w_r=pad_rows(w_odd_t[N_ODD_MAIN:]),
        w_gate=jnp.pad(w_gla_gate[0], ((0, LANES - C_GATE_RANK), (0, 0))).astype(BF16),
        b_gate=b_gla_gate[0].reshape(1, C_KEY_WIDTH),
        gla_gain=gla_gain[0].reshape(1, C_VAL_WIDTH),
        w_out_odd=w_out_odd[0].astype(BF16),
        final_norm=final_norm.reshape(1, D_MODEL),
    )


def _forward(x_prompt, x_sample, ctx, w):
    bsz, seq, _ = x_prompt.shape
    db, t_new, _ = x_sample.shape
    mp, ms = bsz * seq, db * t_new
    xp = x_prompt.reshape(mp, D_MODEL)
    xs = x_sample.reshape(ms, D_MODEL)
    hg = dict(heads=A_HEADS, dk=A_HEAD_DIM, dv=A_HEAD_DIM, q_scale=1.0)
    gl = dict(heads=C_HEADS, dk=C_KEY_DIM, dv=C_VAL_DIM, q_scale=C_KEY_DIM ** -0.5)

    even = lambda x2, s, tm: _even_proj(x2, w["norm_even"], w["w_even"], w["w_fb"], w["b_fox"], seq=s, tm=tm)
    pa_p, gate_p, qa_p, ka_p, vb_p, krows_p, vrows_p, lf_p, _ = even(xp, seq, 512)
    pa_s, gate_s, qa_s, _, _, krows_s, vrows_s, lf_s, c_s = even(xs, t_new, ms)

    oa_p, sa_p = _hgrn_prompt(pa_p, w["lb_logits"], w["hgrn_gain"], bsz=bsz, seq=seq, tb=1024, **hg)
    oa_s, sa_s = _gla_sample(True, [pa_s] * 4, [0, 1, 2, 3], w["lb_logits"], w["hgrn_gain"], ctx["state_hgrn"],
                             t_new=t_new, **hg)
    nrow = t_new * B_HEADS
    rows = lambda a: a.reshape(db, nrow, B_HEAD_DIM)
    cn = c_s[:, :B_HEADS].reshape(db, nrow)
    q_rows = qa_s.reshape(db, nrow, 2 * B_HEAD_DIM)[:, :, :B_HEAD_DIM]
    ob_p, ob_s = _fox_attention(qa_p, ka_p, vb_p, gate_p, ctx["page_table"], ctx["k_pages"], ctx["v_pages"],
                                ctx["lf_pages"], q_rows, rows(krows_s), rows(vrows_s), rows(gate_s),
                                cn.reshape(db, nrow, 1), cn.reshape(db, 1, nrow), bsz=bsz, seq=seq,
                                tq=min(seq, 1024), pages_per_unit=8)

    y_p, sc_p = _layer1_prompt(xp, oa_p, ob_p, w, bsz=bsz, seq=seq, tb=512, **gl)
    x1_s = _out_proj([oa_s, ob_s.reshape(ms, B_WIDTH)], w["w_out_even"], xs, None, tm=ms)
    proj1, lf1 = _odd_proj(x1_s, w["norm_odd"], w["w_odd"], w["w_r"], w["w_gate"], w["b_gate"], tm=ms)
    oc_s, sc_s = _gla_sample(False, [proj1, proj1, proj1, proj1, lf1], [0, 1, 1, 2, 0], None, w["gla_gain"],
                             ctx["state_gla"], t_new=t_new, **gl)
    y_s = _out_proj([oc_s], w["w_out_odd"], x1_s, w["final_norm"], tm=ms)

    heads4 = lambda a, n, s: a.reshape(n, s, B_HEADS, B_HEAD_DIM)
    logf = lambda a, n, s: a[:, :B_HEADS].reshape(n, s, B_HEADS)
    prompt = (y_p.reshape(bsz, seq, D_MODEL), heads4(krows_p, bsz, seq), heads4(vrows_p, bsz, seq),
              logf(lf_p, bsz, seq), sa_p, sc_p)
    sample = (y_s.reshape(db, t_new, D_MODEL), heads4(krows_s, db, t_new), heads4(vrows_s, db, t_new),
              logf(lf_s, db, t_new), sa_s, sc_s)
    return prompt, sample


def kernel(x_prompt, x_sample, cache_fox_k, cache_fox_v, cache_fox_logf, state_hgrn, state_gla, page_table,
           norm_even, w_in_even, b_fox_f, lb_logits, hgrn_gain, w_out_even, norm_odd, w_in_odd, w_gla_gate,
           b_gla_gate, gla_gain, w_out_odd, final_norm):
    weights = _prep_weights((norm_even, w_in_even, b_fox_f, lb_logits, hgrn_gain, w_out_even, norm_odd, w_in_odd,
                             w_gla_gate, b_gla_gate, gla_gain, w_out_odd, final_norm))
    bsz, seq, _ = x_prompt.shape
    n_pool = cache_fox_k.shape[1]
    pw = PAGE_SIZE * B_HEADS
    ctx = dict(
        page_table=page_table,
        k_pages=cache_fox_k[0].reshape(n_pool, pw, B_HEAD_DIM),
        v_pages=cache_fox_v[0].reshape(n_pool, pw, B_HEAD_DIM),
        lf_pages=cache_fox_logf[0].transpose(0, 2, 1).reshape(n_pool, pw),
        state_hgrn=state_hgrn[0], state_gla=state_gla[0])
    (y_p, kp, vp, lfp, hgrn_p, gla_p), (y_s, ks, vs, lfs, hgrn_s, gla_s) = _forward(x_prompt, x_sample, ctx, weights)
    n_pp = seq // PAGE_SIZE
    fox_k_prompt = kp.reshape(1, bsz, n_pp, PAGE_SIZE, B_HEADS, B_HEAD_DIM)
    fox_v_prompt = vp.reshape(1, bsz, n_pp, PAGE_SIZE, B_HEADS, B_HEAD_DIM)
    fox_logf_prompt = lfp.reshape(1, bsz, n_pp, PAGE_SIZE, B_HEADS)
    return (y_p, y_s, fox_k_prompt, fox_v_prompt, fox_logf_prompt, hgrn_p[None], gla_p[None],
            ks[None], vs[None], lfs[None], hgrn_s[None], gla_s[None])
```
